```python
import jax, jax.numpy as jnp
from jax import lax
import numpy as np

D_MODEL = 1024
BATCH = 1
SEQ = 16384
DEPTH = 2

N_A_LAYERS = DEPTH // 2
N_B_LAYERS = DEPTH - N_A_LAYERS
EXPAND = 2
D_INNER = EXPAND * D_MODEL
RMS_EPS = 1e-6
GLA_HEADS = 4
GLA_DK = D_MODEL // 2
GLA_DK_HEAD = GLA_DK // GLA_HEADS
GLA_DV_HEAD = D_INNER // GLA_HEADS
GLA_GATE_RANK = 16
GLA_GATE_TAU = 16.0
GLA_CHUNK = 64
A_IN = 2 * GLA_DK + 2 * D_INNER + GLA_GATE_RANK
DIL_GROUPS = ((128, 1), (512, 4), (2048, 16))
N_GROUPS = len(DIL_GROUPS)
DIL_HEADS = 16
DIL_HEAD_DIM = D_INNER // DIL_HEADS
DIL_BLOCK = 128
ALIBI_MAX_EXP = 8.0
B_IN = N_GROUPS * D_INNER + D_INNER
KV_OUT = 2 * N_GROUPS * D_INNER

kernel_name = "yoco_gla_dilated_alibi_hybrid"


def rmsnorm(x, g):
    xf = x.astype(jnp.float32)
    y = xf * lax.rsqrt(jnp.mean(xf * xf, axis=-1, keepdims=True) + RMS_EPS)
    return (y * g.astype(jnp.float32)).astype(x.dtype)


def gla_mixer(h, w_in, w_gate_up, b_gate, g_out, w_out):
    B, S, _ = h.shape
    H, dk, dv, C = GLA_HEADS, GLA_DK_HEAD, GLA_DV_HEAD, GLA_CHUNK
    n = S // C
    proj = h @ w_in
    q, k, v, z, lr = jnp.split(proj, [GLA_DK, 2 * GLA_DK, 2 * GLA_DK + D_INNER, 2 * GLA_DK + 2 * D_INNER], axis=-1)
    log_a = jax.nn.log_sigmoid((lr @ w_gate_up + b_gate).astype(jnp.float32)) / GLA_GATE_TAU

    def chunks(t, d):
        return t.astype(jnp.float32).reshape(B, n, C, H, d).transpose(1, 0, 3, 2, 4)

    qc = chunks(q, dk) * (dk ** -0.5)
    kc, vc, gc = chunks(k, dk), chunks(v, dv), chunks(log_a, dk)
    causal = jnp.tril(jnp.ones((C, C), dtype=bool))

    def step(state, inp):
        qi, ki, vi, gi = inp
        b = jnp.cumsum(gi, axis=2)
        o_inter = jnp.einsum('bhtk,bhkv->bhtv', qi * jnp.exp(b), state)
        diff = b[:, :, :, None, :] - b[:, :, None, :, :]
        decay = jnp.exp(jnp.where(causal[None, None, :, :, None], diff, -jnp.inf))
        attn = jnp.einsum('bhtk,bhsk,bhtsk->bhts', qi, ki, decay)
        o_intra = jnp.einsum('bhts,bhsv->bhtv', attn, vi)
        b_last = b[:, :, -1:, :]
        k_dec = ki * jnp.exp(b_last - b)
        new_state = jnp.exp(b_last[:, :, 0, :])[..., None] * state + jnp.einsum('bhsk,bhsv->bhkv', k_dec, vi)
        return new_state, o_inter + o_intra

    state0 = jnp.zeros((B, H, dk, dv), jnp.float32)
    _, o = lax.scan(step, state0, (qc, kc, vc, gc))
    o = o.transpose(1, 0, 3, 2, 4).reshape(B, S, H, dv)
    o = o * lax.rsqrt(jnp.mean(o * o, axis=-1, keepdims=True) + RMS_EPS) * g_out.astype(jnp.float32)
    o = o.reshape(B, S, D_INNER) * jax.nn.silu(z.astype(jnp.float32))
    return o.astype(h.dtype) @ w_out


def dilated_group(q, k, v, slopes, window, dilation):
    B, S, H, Dh = q.shape
    d = dilation
    n_steps = window // d
    L = S // d
    blk = DIL_BLOCK
    nb = -(-L // blk)
    Lp = nb * blk

    def to_sub(t):
        t = t.astype(jnp.float32).reshape(B, L, d, H, Dh).transpose(0, 2, 1, 3, 4).reshape(B * d, L, H, Dh)
        return jnp.pad(t, ((0, 0), (0, Lp - L), (0, 0), (0, 0)))

    qs, ks, vs = to_sub(q), to_sub(k), to_sub(v)
    qb = qs.reshape(B * d, nb, blk, H, Dh)

    def band(t):
        tp = jnp.pad(t, ((0, 0), (blk, 0), (0, 0), (0, 0)))
        prev = tp[:, :Lp].reshape(B * d, nb, blk, H, Dh)
        return jnp.concatenate([prev, t.reshape(B * d, nb, blk, H, Dh)], axis=2)

    kb, vb = band(ks), band(vs)
    s = jnp.einsum('znqhd,znkhd->znhqk', qb, kb) * (Dh ** -0.5)
    qi = jnp.arange(blk)[:, None]
    ki = jnp.arange(2 * blk)[None, :]
    delta = qi + blk - ki
    key_pos = jnp.arange(nb)[:, None, None] * blk - blk + ki[None]
    valid = (delta >= 0)[None] & (delta <= n_steps)[None] & (key_pos >= 0)
    bias = -slopes[:, None, None] * (delta * d).astype(jnp.float32)[None]
    s = jnp.where(valid[None, :, None], s + bias[None, None], -jnp.inf)
    m = jnp.max(s, axis=-1, keepdims=True)
    p = jnp.exp(s - m)
    l = jnp.sum(p, axis=-1, keepdims=True)
    o = jnp.einsum('znhqk,znkhd->znhqd', p, vb) / l
    lse = (m + jnp.log(l))[..., 0]
    o = o.transpose(0, 1, 3, 2, 4).reshape(B * d, Lp, H, Dh)[:, :L]
    o = o.reshape(B, d, L, H, Dh).transpose(0, 2, 1, 3, 4).reshape(B, S, H, Dh)
    lse = lse.transpose(0, 1, 3, 2).reshape(B * d, Lp, H)[:, :L]
    lse = lse.reshape(B, d, L, H).transpose(0, 2, 1, 3).reshape(B, S, H)
    return o, lse


def dilated_mixer(h, w_in, k_shared, v_shared, w_out):
    B, S, _ = h.shape
    proj = h @ w_in
    qs = proj[..., :N_GROUPS * D_INNER].reshape(B, S, N_GROUPS, DIL_HEADS, DIL_HEAD_DIM)
    z = proj[..., N_GROUPS * D_INNER:]
    n_all = N_GROUPS * DIL_HEADS
    slopes = (2.0 ** (-ALIBI_MAX_EXP * (jnp.arange(n_all, dtype=jnp.float32) + 1.0) / n_all)).reshape(N_GROUPS, DIL_HEADS)
    outs, lses = [], []
    for g, (window, dilation) in enumerate(DIL_GROUPS):
        o_g, lse_g = dilated_group(qs[:, :, g], k_shared[:, :, g], v_shared[:, :, g], slopes[g], window, dilation)
        outs.append(o_g)
        lses.append(lse_g)
    wts = jax.nn.softmax(jnp.stack(lses, axis=0), axis=0)
    o = jnp.sum(wts[..., None] * jnp.stack(outs, axis=0), axis=0)
    o = o.reshape(B, S, D_INNER) * jax.nn.silu(z.astype(jnp.float32))
    return o.astype(h.dtype) @ w_out


def setup_inputs(seed: int = 0) -> dict:
    key = jax.random.key(seed)
    ks = jax.random.split(key, 14)
    f32 = jnp.float32
    nrm = lambda k, shape, fan: jax.random.normal(k, shape, f32) * (fan ** -0.5)
    return {
        "x": jax.random.normal(ks[0], (BATCH, SEQ, D_MODEL), f32),
        "a_norm": 1.0 + 0.02 * jax.random.normal(ks[1], (N_A_LAYERS, D_MODEL), f32),
        "a_w_in": nrm(ks[2], (N_A_LAYERS, D_MODEL, A_IN), D_MODEL),
        "a_w_gate_up": nrm(ks[3], (N_A_LAYERS, GLA_GATE_RANK, GLA_DK), GLA_GATE_RANK),
        "a_b_gate": 0.1 * jax.random.normal(ks[4], (N_A_LAYERS, GLA_DK), f32),
        "a_g_out": 1.0 + 0.02 * jax.random.normal(ks[5], (N_A_LAYERS, GLA_DV_HEAD), f32),
        "a_w_out": nrm(ks[6], (N_A_LAYERS, D_INNER, D_MODEL), D_INNER),
        "kv_norm": 1.0 + 0.02 * jax.random.normal(ks[7], (D_MODEL,), f32),
        "w_kv": nrm(ks[8], (D_MODEL, KV_OUT), D_MODEL),
        "b_norm": 1.0 + 0.02 * jax.random.normal(ks[9], (N_B_LAYERS, D_MODEL), f32),
        "b_w_in": nrm(ks[10], (N_B_LAYERS, D_MODEL, B_IN), D_MODEL),
        "b_w_out": nrm(ks[11], (N_B_LAYERS, D_INNER, D_MODEL), D_INNER),
        "final_norm": 1.0 + 0.02 * jax.random.normal(ks[12], (D_MODEL,), f32),
    }


def reference(x, a_norm, a_w_in, a_w_gate_up, a_b_gate, a_g_out, a_w_out, kv_norm, w_kv, b_norm, b_w_in, b_w_out, final_norm):
    B, S, _ = x.shape
    k_shared = None
    v_shared = None
    for i in range(DEPTH):
        if i < N_A_LAYERS:
            h = rmsnorm(x, a_norm[i])
            x = x + gla_mixer(h, a_w_in[i], a_w_gate_up[i], a_b_gate[i], a_g_out[i], a_w_out[i])
            if i == N_A_LAYERS - 1:
                kv = (rmsnorm(x, kv_norm) @ w_kv).reshape(B, S, 2, N_GROUPS, DIL_HEADS, DIL_HEAD_DIM)
                k_shared, v_shared = kv[:, :, 0], kv[:, :, 1]
        else:
            j = i - N_A_LAYERS
            h = rmsnorm(x, b_norm[j])
            x = x + dilated_mixer(h, b_w_in[j], k_shared, v_shared, b_w_out[j])
    return rmsnorm(x, final_norm)
```

```python
import functools

import jax
import jax.numpy as jnp
from jax import lax
from jax.experimental import pallas as pl
from jax.experimental.pallas import tpu as pltpu

F32 = jnp.float32
BF16 = jnp.bfloat16

RMS_EPS = 1e-6
D_MODEL = 1024
D_INNER = 2048
GLA_HEADS = 4
GLA_DK_HEAD = 128
GLA_DV_HEAD = 512
GLA_DK = GLA_HEADS * GLA_DK_HEAD
GLA_GATE_RANK = 16
GLA_GATE_TAU = 16.0
GLA_CHUNK = 64
GLA_SUB = 16
DIL_GROUPS = ((128, 1), (512, 4), (2048, 16))
N_GROUPS = 3
DIL_HEADS = 16
DIL_HEAD_DIM = 128
DIL_BLOCK = 128
ALIBI_MAX_EXP = 8.0
MASK_DISTANCE = 1e34

LANES = 128
VMEM_LIMIT = 56 * 1024 * 1024

_NT = (((1,), (1,)), ((), ()))


def _norm_proj_kernel(seg_ref, x_ref, g_ref, w_ref, *rest, n_gains, has_aux):
    if has_aux:
        waux_ref, o_ref, aux_ref, h_ref = rest
    else:
        o_ref, h_ref = rest
    n = pl.program_id(1)

    @pl.when(n == 0)
    def _():
        x = x_ref[...]
        y = x * lax.rsqrt(jnp.mean(x * x, axis=-1, keepdims=True) + RMS_EPS)
        for s in range(n_gains):
            h_ref[s] = (y * g_ref[s:s + 1, :]).astype(BF16)
        if has_aux:
            aux_ref[...] = jnp.dot(h_ref[0], waux_ref[...], preferred_element_type=F32)

    h = h_ref[seg_ref[n]]
    o_ref[...] = jnp.dot(h, w_ref[...], preferred_element_type=F32).astype(o_ref.dtype)


def _norm_proj(x, gains, seg_of_block, w, w_aux, *, tm, tn):
    S, D = x.shape
    N = w.shape[1]
    n_gains = gains.shape[0]
    has_aux = w_aux is not None
    grid = (S // tm, N // tn)
    in_specs = [
        pl.BlockSpec((tm, D), lambda m, n, seg: (m, 0)),
        pl.BlockSpec((n_gains, D), lambda m, n, seg: (0, 0)),
        pl.BlockSpec((D, tn), lambda m, n, seg: (0, n)),
    ]
    out_shape = [jax.ShapeDtypeStruct((S, N), BF16)]
    out_specs = [pl.BlockSpec((tm, tn), lambda m, n, seg: (m, n))]
    args = [x, gains, w]
    if has_aux:
        in_specs.append(pl.BlockSpec((D, LANES), lambda m, n, seg: (0, 0)))
        out_shape.append(jax.ShapeDtypeStruct((S, LANES), F32))
        out_specs.append(pl.BlockSpec((tm, LANES), lambda m, n, seg: (m, 0)))
        args.append(w_aux)
    res = pl.pallas_call(
        functools.partial(_norm_proj_kernel, n_gains=n_gains, has_aux=has_aux),
        grid_spec=pltpu.PrefetchScalarGridSpec(
            num_scalar_prefetch=1, grid=grid, in_specs=in_specs, out_specs=out_specs,
            scratch_shapes=[pltpu.VMEM((n_gains, tm, D), BF16)]),
        out_shape=out_shape,
        compiler_params=pltpu.CompilerParams(
            dimension_semantics=("arbitrary", "arbitrary"), vmem_limit_bytes=VMEM_LIMIT),
        name="norm_proj_aux" if has_aux else "norm_proj",
    )(seg_of_block, *args)
    return res if has_aux else res[0]


def _split3(a):
    hi = a.astype(BF16)
    r1 = a - hi.astype(F32)
    mid = r1.astype(BF16)
    lo = (r1 - mid.astype(F32)).astype(BF16)
    return hi, mid, lo


def _gla_head_chunk(b, q, k, v, state):
    C, SUB = GLA_CHUNK, GLA_SUB
    dk = q.shape[1]
    b_last = b[C - 1:C, :]
    o = jnp.dot((q * jnp.exp(b)).astype(BF16), state.astype(BF16), preferred_element_type=F32)

    lane = lax.broadcasted_iota(jnp.int32, (SUB, LANES), 1)
    row = lax.broadcasted_iota(jnp.int32, (SUB, LANES), 0)
    lane_c = lax.broadcasted_iota(jnp.int32, (SUB, C), 1)
    ones =jnp.ones((dk, LANES), BF16)
    a_rows = []
    for i in range(C // SUB):
        r0 = i * SUB
        bq, qq, kk = b[r0:r0 + SUB], q[r0:r0 + SUB], k[r0:r0 + SUB]
        prods = []
        for s in range(SUB):
            dec = jnp.exp(jnp.minimum(bq - bq[s:s + 1, :], 0.0))
            prods.append(qq * dec * kk[s:s + 1, :])
        sums = jnp.dot(jnp.concatenate(prods, axis=0).astype(BF16), ones, preferred_element_type=F32)
        a_i = jnp.zeros((SUB, LANES), F32)
        for s in range(SUB):
            a_i = jnp.where(lane == r0 + s, sums[s * SUB:(s + 1) * SUB, :], a_i)
        a_i = jnp.where(lane <= row + r0, a_i, 0.0)[:, :C]
        if i > 0:
            b_ref = bq[0:1, :]
            qt = (qq * jnp.exp(bq - b_ref)).astype(BF16)
            kt = (k * jnp.exp(jnp.minimum(b_ref - b, 0.0))).astype(BF16)
            off = lax.dot_general(qt, kt, _NT, preferred_element_type=F32)
            a_i = jnp.where(lane_c < r0, off, a_i)
        a_rows.append(a_i)
    a = jnp.concatenate(a_rows, axis=0).astype(BF16)
    o = o + jnp.dot(a, v, preferred_element_type=F32)

    k_dec = k * jnp.exp(b_last - b)
    upd = jnp.dot(k_dec.T.astype(BF16), v, preferred_element_type=F32)
    e_col = jnp.broadcast_to(jnp.exp(b_last), (dk, dk)).T
    e_full = jnp.concatenate([e_col] * (v.shape[1] // dk), axis=1)
    return o, state * e_full + upd


def _gla_kernel(q_ref, k_ref, v_ref, z_ref, lr_ref, x_ref, wgu_ref, bg_ref, gout_ref, wout_ref,
                o_ref, state_ref, y_ref, b_ref):
    tm = q_ref.shape[0]
    C = GLA_CHUNK

    @pl.when(pl.program_id(0) == 0)
    def _():
        state_ref[...] = jnp.zeros_like(state_ref)

    pre = jnp.dot(lr_ref[...].astype(BF16), wgu_ref[...], preferred_element_type=F32) + bg_ref[...]
    log_a = (jnp.minimum(pre, 0.0) - jnp.log1p(jnp.exp(-jnp.abs(pre)))) * (1.0 / GLA_GATE_TAU)
    r = lax.broadcasted_iota(jnp.int32, (tm, tm), 0)
    c = lax.broadcasted_iota(jnp.int32, (tm, tm), 1)
    tri = ((c <= r) & ((c // C) == (r // C))).astype(BF16)
    hi, mid, lo = _split3(log_a)
    b_ref[...] = (jnp.dot(tri, hi, preferred_element_type=F32)
                  + jnp.dot(tri, mid, preferred_element_type=F32)
                  + jnp.dot(tri, lo, preferred_element_type=F32))

    gout = gout_ref[...]

    def chunk_body(ci, carry):
        rows = pl.ds(pl.multiple_of(ci * C, C), C)
        for h in range(GLA_HEADS):
            kl = slice(h * GLA_DK_HEAD, (h + 1) * GLA_DK_HEAD)
            vl = slice(h * GLA_DV_HEAD, (h + 1) * GLA_DV_HEAD)
            q = q_ref[rows, kl].astype(F32) * (GLA_DK_HEAD ** -0.5)
            k = k_ref[rows, kl].astype(F32)
            o, new_state = _gla_head_chunk(b_ref[rows, kl], q, k, v_ref[rows, vl], state_ref[h])
            state_ref[h] = new_state
            o = o * lax.rsqrt(jnp.mean(o * o, axis=-1, keepdims=True) + RMS_EPS) * gout
            z = z_ref[rows, vl].astype(F32)
            y_ref[rows, vl] = (o * (z * jax.nn.sigmoid(z))).astype(BF16)
        return carry

    lax.fori_loop(0, tm // C, chunk_body, 0)
    o_ref[...] = x_ref[...] + jnp.dot(y_ref[...], wout_ref[...], preferred_element_type=F32)


def _gla_layer(x, proj, lr, w_gate_up, b_gate, g_out, w_out, *, tm):
    S, D = x.shape
    qk_blk = 2 * D_INNER // GLA_DK
    return pl.pallas_call(
        _gla_kernel,
        grid=(S // tm,),
        in_specs=[
            pl.BlockSpec((tm, GLA_DK), lambda m: (m, qk_blk)),
            pl.BlockSpec((tm, GLA_DK), lambda m: (m, qk_blk + 1)),
            pl.BlockSpec((tm, D_INNER), lambda m: (m, 0)),
            pl.BlockSpec((tm, D_INNER), lambda m: (m, 1)),
            pl.BlockSpec((tm, LANES), lambda m: (m, 0)),
            pl.BlockSpec((tm, D), lambda m: (m, 0)),
            pl.BlockSpec((LANES, GLA_DK), lambda m: (0, 0)),
            pl.BlockSpec((1, GLA_DK), lambda m: (0, 0)),
            pl.BlockSpec((1, GLA_DV_HEAD), lambda m: (0, 0)),
            pl.BlockSpec((D_INNER, D), lambda m: (0, 0)),
        ],
        out_specs=pl.BlockSpec((tm, D), lambda m: (m, 0)),
        out_shape=jax.ShapeDtypeStruct((S, D), F32),
        scratch_shapes=[
            pltpu.VMEM((GLA_HEADS, GLA_DK_HEAD, GLA_DV_HEAD), F32),
            pltpu.VMEM((tm, D_INNER), BF16),
            pltpu.VMEM((tm, GLA_DK), F32),
        ],
        compiler_params=pltpu.CompilerParams(
            dimension_semantics=("arbitrary",), vmem_limit_bytes=VMEM_LIMIT),
        name="gla",
    )(proj, proj, proj, proj, lr, x, w_gate_up, b_gate, g_out, w_out)


def _dil_attn_kernel(q_ref, kp_ref, kc_ref, vp_ref, vc_ref, o_ref, lse_ref, *, slopes, dilation):
    blk = DIL_BLOCK
    i = pl.program_id(1)
    row = lax.broadcasted_iota(jnp.int32, (blk, blk), 0)
    col = lax.broadcasted_iota(jnp.int32, (blk, blk), 1)
    dist_prev = jnp.where((col >= row) & (i > 0), ((row + blk - col) * dilation).astype(F32), MASK_DISTANCE)
    dist_cur = jnp.where(col <= row, ((row - col) * dilation).astype(F32), MASK_DISTANCE)
    lse_ref[...] = jnp.zeros_like(lse_ref)
    for h in range(DIL_HEADS):
        hl = slice(h * DIL_HEAD_DIM, (h + 1) * DIL_HEAD_DIM)
        q = q_ref[:, hl]
        s_prev = lax.dot_general(q, kp_ref[:, hl], _NT, preferred_element_type=F32) - slopes[h] * dist_prev
        s_cur = lax.dot_general(q, kc_ref[:, hl], _NT, preferred_element_type=F32) - slopes[h] * dist_cur
        m = jnp.max(jnp.maximum(s_prev, s_cur), axis=-1, keepdims=True)
        p_prev = jnp.exp(s_prev - m)
        p_cur = jnp.exp(s_cur - m)
        l = jnp.sum(p_prev + p_cur, axis=-1, keepdims=True)
        acc = (jnp.dot(p_prev.astype(BF16), vp_ref[:, hl], preferred_element_type=F32)
               + jnp.dot(p_cur.astype(BF16), vc_ref[:, hl], preferred_element_type=F32))
        o_ref[:, hl] = (acc / l).astype(o_ref.dtype)
        lse_ref[:, h:h + 1] = m + jnp.log(l)


def _dil_attn_group(proj, g, *, S):
    _, d = DIL_GROUPS[g]
    L = S // d
    nb = L // DIL_BLOCK
    n_all = N_GROUPS * DIL_HEADS
    slopes = tuple(2.0 ** (-ALIBI_MAX_EXP * (g * DIL_HEADS + h + 1.0) / n_all) for h in range(DIL_HEADS))
    ncol = proj.shape[1] // D_INNER
    pv = proj.reshape(L, d * proj.shape[1])
    blk = (DIL_BLOCK, D_INNER)
    o, lse = pl.pallas_call(
        functools.partial(_dil_attn_kernel, slopes=slopes, dilation=d),
        grid=(d, nb),
        in_specs=[
            pl.BlockSpec(blk, lambda r, i: (i, r * ncol + 2 * N_GROUPS + g)),
            pl.BlockSpec(blk, lambda r, i: (jnp.maximum(i - 1, 0), r * ncol + g)),
            pl.BlockSpec(blk, lambda r, i: (i, r * ncol + g)),
            pl.BlockSpec(blk, lambda r, i: (jnp.maximum(i - 1, 0), r * ncol + N_GROUPS + g)),
            pl.BlockSpec(blk, lambda r, i: (i, r * ncol + N_GROUPS + g)),
        ],
        out_specs=[
            pl.BlockSpec(blk, lambda r, i: (i, r)),
            pl.BlockSpec((DIL_BLOCK, LANES), lambda r, i: (i, r)),
        ],
        out_shape=[
            jax.ShapeDtypeStruct((L, d * D_INNER), BF16),
            jax.ShapeDtypeStruct((L, d * LANES), F32),
        ],
        compiler_params=pltpu.CompilerParams(
            dimension_semantics=("arbitrary", "arbitrary"), vmem_limit_bytes=VMEM_LIMIT),
        name=f"dil_attn_g{g}",
    )(pv, pv, pv, pv, pv)
    return o.reshape(S, D_INNER), lse.reshape(S, LANES)


def _combine_kernel(o0_ref, o1_ref, o2_ref, l0_ref, l1_ref, l2_ref, z_ref, x_ref, wout_ref, g_ref,
                    out_ref, y_ref):
    lses = [l0_ref[...], l1_ref[...], l2_ref[...]]
    m = jnp.maximum(jnp.maximum(lses[0], lses[1]), lses[2])
    es = [jnp.exp(l - m) for l in lses]
    den = es[0] + es[1] + es[2]
    ws = [e / den for e in es]
    o_refs = (o0_ref, o1_ref, o2_ref)
    for h in range(DIL_HEADS):
        hl = slice(h * DIL_HEAD_DIM, (h + 1) * DIL_HEAD_DIM)
        acc = ws[0][:, h:h + 1] * o_refs[0][:, hl].astype(F32)
        for g in range(1, N_GROUPS):
            acc = acc + ws[g][:, h:h + 1] * o_refs[g][:, hl].astype(F32)
        z = z_ref[:, hl].astype(F32)
        y_ref[:, hl] = (acc * (z * jax.nn.sigmoid(z))).astype(BF16)
    x = x_ref[...] + jnp.dot(y_ref[...], wout_ref[...], preferred_element_type=F32)
    y = x * lax.rsqrt(jnp.mean(x * x, axis=-1, keepdims=True) + RMS_EPS)
    out_ref[...] = y * g_ref[...]


def _combine(os_, lses, proj, x, w_out, g_final, *, tm):
    S, D = x.shape
    z_blk = proj.shape[1] // D_INNER - 1
    row_blk = lambda m: (m, 0)
    return pl.pallas_call(
        _combine_kernel,
        grid=(S // tm,),
        in_specs=[pl.BlockSpec((tm, D_INNER), row_blk)] * 3 + [pl.BlockSpec((tm, LANES), row_blk)] * 3 + [
            pl.BlockSpec((tm, D_INNER), lambda m: (m, z_blk)),
            pl.BlockSpec((tm, D), row_blk),
            pl.BlockSpec((D_INNER, D), lambda m: (0, 0)),
            pl.BlockSpec((1, D), lambda m: (0, 0)),
        ],
        out_specs=pl.BlockSpec((tm, D), row_blk),
        out_shape=jax.ShapeDtypeStruct((S, D), F32),
        scratch_shapes=[pltpu.VMEM((tm, D_INNER), BF16)],
        compiler_params=pltpu.CompilerParams(
            dimension_semantics=("arbitrary",), vmem_limit_bytes=VMEM_LIMIT),
        name="combine",
    )(*os_, *lses, proj, x, w_out, g_final)


def kernel(x, a_norm, a_w_in, a_w_gate_up, a_b_gate, a_g_out, a_w_out, kv_norm, w_kv, b_norm, b_w_in, b_w_out,
           final_norm):
    B, S, D = x.shape
    assert B == 1 and D == D_MODEL and a_norm.shape[0] == 1 and b_norm.shape[0] == 1
    assert S % (DIL_BLOCK * DIL_GROUPS[-1][1]) == 0
    x0 = x.reshape(S, D)

    w_in = a_w_in[0]
    qk, vz, lr0 = w_in[:, :2 * GLA_DK], w_in[:, 2 * GLA_DK:2 * GLA_DK + 2 * D_INNER], 2 * GLA_DK + 2 * D_INNER
    w_a = jnp.concatenate([vz, qk], axis=1).astype(BF16)
    w_lr = jnp.pad(w_in[:, lr0:], ((0, 0), (0, LANES - GLA_GATE_RANK))).astype(BF16)
    tn_a = w_a.shape[1] // 2
    proj_a, lr = _norm_proj(x0, a_norm, jnp.zeros((2,), jnp.int32), w_a, w_lr, tm=1024, tn=tn_a)
    w_gu = jnp.pad(a_w_gate_up[0], ((0, LANES - GLA_GATE_RANK), (0, 0))).astype(BF16)
    x1 = _gla_layer(x0, proj_a, lr, w_gu, a_b_gate, a_g_out, a_w_out[0].astype(BF16), tm=256)

    w_b = jnp.concatenate([w_kv, b_w_in[0]], axis=1).astype(BF16)
    q_scale = DIL_HEAD_DIM ** -0.5
    gains = jnp.stack([kv_norm, b_norm[0] * q_scale, b_norm[0]])
    seg = jnp.array([0] * (2 * N_GROUPS) + [1] * N_GROUPS + [2], jnp.int32)
    proj_b = _norm_proj(x1, gains, seg, w_b, None, tm=1024, tn=D_INNER)

    outs = [_dil_attn_group(proj_b, g, S=S) for g in range(N_GROUPS)]
    out = _combine([o for o, _ in outs], [l for _, l in outs], proj_b, x1, b_w_out[0].astype(BF16),
                   final_norm.reshape(1, D), tm=512)
    return out.reshape(B, S, D)
```

```python
import functools

import jax
import jax.numpy as jnp
from jax import lax
from jax.experimental import pallas as pl
from jax.experimental.pallas import tpu as pltpu

F32 = jnp.float32
BF16 = jnp.bfloat16

RMS_EPS = 1e-6
D_MODEL = 1024
D_INNER = 2048
GLA_HEADS = 4
GLA_DK_HEAD = 128
GLA_DV_HEAD = 512
GLA_DK = GLA_HEADS * GLA_DK_HEAD
GLA_GATE_RANK = 16
GLA_GATE_TAU = 16.0
GLA_CHUNK = 64
GLA_SUB = 16
DIL_GROUPS = ((128, 1), (512, 4), (2048, 16))
N_GROUPS = 3
DIL_HEADS = 16
DIL_HEAD_DIM = 128
DIL_BLOCK = 128
ALIBI_MAX_EXP = 8.0
MASK_DISTANCE = 1e34
PERM_ROWS = 256

LANES = 128
VMEM_LIMIT = 56 * 1024 * 1024

_NT = (((1,), (1,)), ((), ()))


def _deinterleave_matrix(d, transpose=False):
    n = PERM_ROWS // d
    p = lax.broadcasted_iota(jnp.int32, (PERM_ROWS, PERM_ROWS), 1 if transpose else 0)
    j = lax.broadcasted_iota(jnp.int32, (PERM_ROWS, PERM_ROWS), 0 if transpose else 1)
    return (j == (p % n) * d + p // n).astype(BF16)


def _norm_proj_kernel(var_ref, x_ref, g_ref, w_ref, *rest, variants, has_aux):
    if has_aux:
        waux_ref, o_ref, aux_ref, h_ref = rest
    else:
        o_ref, h_ref = rest
    n = pl.program_id(1)
    tm = x_ref.shape[0]

    @pl.when(n == 0)
    def _():
        x = x_ref[...]
        y = x * lax.rsqrt(jnp.mean(x * x, axis=-1, keepdims=True) + RMS_EPS)
        for gi in sorted({g for g, _ in variants}):
            h = (y * g_ref[gi:gi + 1, :]).astype(BF16)
            for vi, (g, d) in enumerate(variants):
                if g != gi:
                    continue
                if d == 1:
                    h_ref[vi] = h
                else:
                    perm = _deinterleave_matrix(d)
                    for b in range(tm // PERM_ROWS):
                        rows = slice(b * PERM_ROWS, (b + 1) * PERM_ROWS)
                        h_ref[vi, rows, :] = jnp.dot(perm, h[rows], preferred_element_type=F32).astype(BF16)
        if has_aux:
            aux_ref[...] = jnp.dot(h_ref[0], waux_ref[...], preferred_element_type=F32)

    h = h_ref[var_ref[n]]
    o_ref[...] = jnp.dot(h, w_ref[...], preferred_element_type=F32).astype(o_ref.dtype)


def _norm_proj(x, gains, variants, var_of_block, w, w_aux, *, tm, tn):
    S, D = x.shape
    N = w.shape[1]
    n_gains = gains.shape[0]
    has_aux = w_aux is not None
    grid = (S // tm, N // tn)
    in_specs = [
        pl.BlockSpec((tm, D), lambda m, n, seg: (m, 0)),
        pl.BlockSpec((n_gains, D), lambda m, n, seg: (0, 0)),
        pl.BlockSpec((D, tn), lambda m, n, seg: (0, n)),
    ]
    out_shape = [jax.ShapeDtypeStruct((S, N), BF16)]
    out_specs = [pl.BlockSpec((tm, tn), lambda m, n, seg: (m, n))]
    args = [x, gains, w]
    if has_aux:
        in_specs.append(pl.BlockSpec((D, LANES), lambda m, n, seg: (0, 0)))
        out_shape.append(jax.ShapeDtypeStruct((S, LANES), F32))
        out_specs.append(pl.BlockSpec((tm, LANES), lambda m, n, seg: (m, 0)))
        args.append(w_aux)
    res = pl.pallas_call(
        functools.partial(_norm_proj_kernel, variants=tuple(variants), has_aux=has_aux),
        grid_spec=pltpu.PrefetchScalarGridSpec(
            num_scalar_prefetch=1, grid=grid, in_specs=in_specs, out_specs=out_specs,
            scratch_shapes=[pltpu.VMEM((len(variants), tm, D), BF16)]),
        out_shape=out_shape,
        compiler_params=pltpu.CompilerParams(
            dimension_semantics=("arbitrary", "arbitrary"), vmem_limit_bytes=VMEM_LIMIT),
        name="norm_proj_aux" if has_aux else "norm_proj",
    )(var_of_block, *args)
    return res if has_aux else res[0]


def _split3(a):
    hi = a.astype(BF16)
    r1 = a - hi.astype(F32)
    mid = r1.astype(BF16)
    lo = (r1 - mid.astype(F32)).astype(BF16)
    return hi, mid, lo


def _gla_head_chunk(b, q, k, v, state):
    C, SUB = GLA_CHUNK, GLA_SUB
    dk = q.shape[1]
    b_last = b[C - 1:C, :]
    o = jnp.dot((q * jnp.exp(b)).astype(BF16), state.astype(BF16), preferred_element_type=F32)

    lane = lax.broadcasted_iota(jnp.int32, (SUB, LANES), 1)
    row = lax.broadcasted_iota(jnp.int32, (SUB, LANES), 0)
    lane_c = lax.broadcasted_iota(jnp.int32, (SUB, C), 1)
    ones =jnp.ones((dk, LANES), BF16)
    a_rows = []
    for i in range(C // SUB):
        r0 = i * SUB
        bq, qq, kk = b[r0:r0 + SUB], q[r0:r0 + SUB], k[r0:r0 + SUB]
        prods = []
        for s in range(SUB):
            dec = jnp.exp(jnp.minimum(bq - bq[s:s + 1, :], 0.0))
            prods.append(qq * dec * kk[s:s + 1, :])
        sums = jnp.dot(jnp.concatenate(prods, axis=0).astype(BF16), ones, preferred_element_type=F32)
        a_i = jnp.zeros((SUB, LANES), F32)
        for s in range(SUB):
            a_i = jnp.where(lane == r0 + s, sums[s * SUB:(s + 1) * SUB, :], a_i)
        a_i = jnp.where(lane <= row + r0, a_i, 0.0)[:, :C]
        if i > 0:
            b_ref = bq[0:1, :]
            qt = (qq * jnp.exp(bq - b_ref)).astype(BF16)
            kt = (k * jnp.exp(jnp.minimum(b_ref - b, 0.0))).astype(BF16)
            off = lax.dot_general(qt, kt, _NT, preferred_element_type=F32)
            a_i = jnp.where(lane_c < r0, off, a_i)
        a_rows.append(a_i)
    a = jnp.concatenate(a_rows, axis=0).astype(BF16)
    o = o + jnp.dot(a, v, preferred_element_type=F32)

    k_dec = k * jnp.exp(b_last - b)
    upd = jnp.dot(k_dec.T.astype(BF16), v, preferred_element_type=F32)
    e_col = jnp.broadcast_to(jnp.exp(b_last), (dk, dk)).T
    e_full = jnp.concatenate([e_col] * (v.shape[1] // dk), axis=1)
    return o, state * e_full + upd


def _gla_kernel(q_ref, k_ref, v_ref, z_ref, lr_ref, x_ref, wgu_ref, bg_ref, gout_ref, wout_ref,
                o_ref, state_ref, y_ref, b_ref):
    tm = q_ref.shape[0]
    C = GLA_CHUNK

    @pl.when(pl.program_id(0) == 0)
    def _():
        state_ref[...] = jnp.zeros_like(state_ref)

    pre = jnp.dot(lr_ref[...].astype(BF16), wgu_ref[...], preferred_element_type=F32) + bg_ref[...]
    log_a = (jnp.minimum(pre, 0.0) - jnp.log1p(jnp.exp(-jnp.abs(pre)))) * (1.0 / GLA_GATE_TAU)
    r = lax.broadcasted_iota(jnp.int32, (tm, tm), 0)
    c = lax.broadcasted_iota(jnp.int32, (tm, tm), 1)
    tri = ((c <= r) & ((c // C) == (r // C))).astype(BF16)
    hi, mid, lo = _split3(log_a)
    b_ref[...] = (jnp.dot(tri, hi, preferred_element_type=F32)
                  + jnp.dot(tri, mid, preferred_element_type=F32)
                  + jnp.dot(tri, lo, preferred_element_type=F32))

    gout = gout_ref[...]

    def chunk_body(ci, carry):
        rows = pl.ds(pl.multiple_of(ci * C, C), C)
        for h in range(GLA_HEADS):
            kl = slice(h * GLA_DK_HEAD, (h + 1) * GLA_DK_HEAD)
            vl = slice(h * GLA_DV_HEAD, (h + 1) * GLA_DV_HEAD)
            q = q_ref[rows, kl].astype(F32) * (GLA_DK_HEAD ** -0.5)
            k = k_ref[rows, kl].astype(F32)
            o, new_state = _gla_head_chunk(b_ref[rows, kl], q, k, v_ref[rows, vl], state_ref[h])
            state_ref[h] = new_state
            o = o * lax.rsqrt(jnp.mean(o * o, axis=-1, keepdims=True) + RMS_EPS) * gout
            z = z_ref[rows, vl].astype(F32)
            y_ref[rows, vl] = (o * (z * jax.nn.sigmoid(z))).astype(BF16)
        return carry

    lax.fori_loop(0, tm // C, chunk_body, 0)
    o_ref[...] = x_ref[...] + jnp.dot(y_ref[...], wout_ref[...], preferred_element_type=F32)


def _gla_layer(x, proj, lr, w_gate_up, b_gate, g_out, w_out, *, tm):
    S, D = x.shape
    qk_blk = 2 * D_INNER // GLA_DK
    return pl.pallas_call(
        _gla_kernel,
        grid=(S // tm,),
        in_specs=[
            pl.BlockSpec((tm, GLA_DK), lambda m: (m, qk_blk)),
            pl.BlockSpec((tm, GLA_DK), lambda m: (m, qk_blk + 1)),
            pl.BlockSpec((tm, D_INNER), lambda m: (m, 0)),
            pl.BlockSpec((tm, D_INNER), lambda m: (m, 1)),
            pl.BlockSpec((tm, LANES), lambda m: (m, 0)),
            pl.BlockSpec((tm, D), lambda m: (m, 0)),
            pl.BlockSpec((LANES, GLA_DK), lambda m: (0, 0)),
            pl.BlockSpec((1, GLA_DK), lambda m: (0, 0)),
            pl.BlockSpec((1, GLA_DV_HEAD), lambda m: (0, 0)),
            pl.BlockSpec((D_INNER, D), lambda m: (0, 0)),
        ],
        out_specs=pl.BlockSpec((tm, D), lambda m: (m, 0)),
        out_shape=jax.ShapeDtypeStruct((S, D), F32),
        scratch_shapes=[
            pltpu.VMEM((GLA_HEADS, GLA_DK_HEAD, GLA_DV_HEAD), F32),
            pltpu.VMEM((tm, D_INNER), BF16),
            pltpu.VMEM((tm, GLA_DK), F32),
        ],
        compiler_params=pltpu.CompilerParams(
            dimension_semantics=("arbitrary",), vmem_limit_bytes=VMEM_LIMIT),
        name="gla",
    )(proj, proj, proj, proj, lr, x, w_gate_up, b_gate, g_out, w_out)


def _head_tile(ref, hl):
    t = ref[:, hl] if len(ref.shape) == 2 else ref[:, :, hl]
    return t.reshape(DIL_BLOCK, t.shape[-1])


def _dil_attn_kernel(q_ref, kp_ref, kc_ref, vp_ref, vc_ref, o_ref, lse_ref, lse_acc, *, slopes, dilation):
    blk = DIL_BLOCK
    i = pl.program_id(1)
    row = lax.broadcasted_iota(jnp.int32, (blk, blk), 0)
    col = lax.broadcasted_iota(jnp.int32, (blk, blk), 1)
    dist_prev = jnp.where((col >= row) & (i > 0), ((row + blk - col) * dilation).astype(F32), MASK_DISTANCE)
    dist_cur = jnp.where(col <= row, ((row - col) * dilation).astype(F32), MASK_DISTANCE)
    lse_acc[...] = jnp.zeros_like(lse_acc)
    for h in range(DIL_HEADS):
        hl = slice(h * DIL_HEAD_DIM, (h + 1) * DIL_HEAD_DIM)
        q = _head_tile(q_ref, hl)
        s_prev = lax.dot_general(q, _head_tile(kp_ref, hl), _NT, preferred_element_type=F32) - slopes[h] * dist_prev
        s_cur = lax.dot_general(q, _head_tile(kc_ref, hl), _NT, preferred_element_type=F32) - slopes[h] * dist_cur
        m = jnp.max(jnp.maximum(s_prev, s_cur), axis=-1, keepdims=True)
        p_prev = jnp.exp(s_prev - m)
        p_cur = jnp.exp(s_cur - m)
        l = jnp.sum(p_prev + p_cur, axis=-1, keepdims=True)
        acc = (jnp.dot(p_prev.astype(BF16), _head_tile(vp_ref, hl), preferred_element_type=F32)
               + jnp.dot(p_cur.astype(BF16), _head_tile(vc_ref, hl), preferred_element_type=F32))
        o = (acc / l).astype(o_ref.dtype)
        if len(o_ref.shape) == 2:
            o_ref[:, hl] = o
        else:
            o_ref[:, :, hl] = o.reshape(o_ref.shape[0], o_ref.shape[1], DIL_HEAD_DIM)
        lse_acc[:, h:h + 1] = m + jnp.log(l)
    lse_ref[...] = lse_acc[...].reshape(lse_ref.shape)


def _dil_attn_group(proj, g, *, S):
    _, d = DIL_GROUPS[g]
    nb = S // d // DIL_BLOCK
    n_all = N_GROUPS * DIL_HEADS
    slopes = tuple(2.0 ** (-ALIBI_MAX_EXP * (g * DIL_HEADS + h + 1.0) / n_all) for h in range(DIL_HEADS))
    kcol, vcol, qcol = g, N_GROUPS + g, 2 * N_GROUPS + g
    prev = lambda i: jnp.maximum(i - 1, 0)
    if d == 1:
        pv = proj
        spec = lambda width, row, col: pl.BlockSpec((DIL_BLOCK, width), lambda r, i: (row(i), col))
        shape = lambda width, dt: jax.ShapeDtypeStruct((S, width), dt)
    else:
        run = PERM_ROWS // d
        pieces = DIL_BLOCK // run
        pv = proj.reshape(S // PERM_ROWS, d, run, proj.shape[1])
        spec = lambda width, row, col: pl.BlockSpec((pieces, None, run, width), lambda r, i: (row(i), r, 0, col))
        shape = lambda width, dt: jax.ShapeDtypeStruct((S // PERM_ROWS, d, run, width), dt)
    cur = lambda i: i
    o, lse = pl.pallas_call(
        functools.partial(_dil_attn_kernel, slopes=slopes, dilation=d),
        grid=(d, nb),
        in_specs=[
            spec(D_INNER, cur, qcol),
            spec(D_INNER, prev, kcol), spec(D_INNER, cur, kcol),
            spec(D_INNER, prev, vcol), spec(D_INNER, cur, vcol),
        ],
        out_specs=[spec(D_INNER, cur, 0), spec(LANES, cur, 0)],
        out_shape=[shape(D_INNER, BF16), shape(LANES, F32)],
        scratch_shapes=[pltpu.VMEM((DIL_BLOCK, LANES), F32)],
        compiler_params=pltpu.CompilerParams(
            dimension_semantics=("arbitrary", "arbitrary"), vmem_limit_bytes=VMEM_LIMIT),
        name=f"dil_attn_g{g}",
    )(pv, pv, pv, pv, pv)
    return o.reshape(S, D_INNER), lse.reshape(S, LANES)


def _combine_kernel(o0_ref, o1_ref, o2_ref, l0_ref, l1_ref, l2_ref, z_ref, x_ref, wout_ref, g_ref,
                    out_ref, y_ref, ot_ref):
    tm = x_ref.shape[0]
    o_refs = (o0_ref, o1_ref, o2_ref)
    l_refs = (l0_ref, l1_ref, l2_ref)
    lses = [l0_ref[...]]
    for g in range(1, N_GROUPS):
        unperm = _deinterleave_matrix(DIL_GROUPS[g][1], transpose=True)
        parts = []
        for b in range(tm // PERM_ROWS):
            rows = slice(b * PERM_ROWS, (b + 1) * PERM_ROWS)
            ot_ref[g - 1, rows, :] = jnp.dot(unperm, o_refs[g][rows, :], preferred_element_type=F32)
            parts.append(sum(jnp.dot(unperm, piece, preferred_element_type=F32)
                             for piece in _split3(l_refs[g][rows, :])))
        lses.append(jnp.concatenate(parts, axis=0))
    m = jnp.maximum(jnp.maximum(lses[0], lses[1]), lses[2])
    es = [jnp.exp(l - m) for l in lses]
    den = es[0] + es[1] + es[2]
    ws = [e / den for e in es]
    for h in range(DIL_HEADS):
        hl = slice(h * DIL_HEAD_DIM, (h + 1) * DIL_HEAD_DIM)
        acc = ws[0][:, h:h + 1] * o_refs[0][:, hl].astype(F32)
        for g in range(1, N_GROUPS):
            acc = acc + ws[g][:, h:h + 1] * ot_ref[g - 1, :, hl]
        z = z_ref[:, hl].astype(F32)
        y_ref[:, hl] = (acc * (z * jax.nn.sigmoid(z))).astype(BF16)
    x = x_ref[...] + jnp.dot(y_ref[...], wout_ref[...], preferred_element_type=F32)
    y = x * lax.rsqrt(jnp.mean(x * x, axis=-1, keepdims=True) + RMS_EPS)
    out_ref[...] = y * g_ref[...]


def _combine(os_, lses, proj, x, w_out, g_final, *, tm):
    S, D = x.shape
    z_blk = proj.shape[1] // D_INNER - 1
    row_blk = lambda m: (m, 0)
    return pl.pallas_call(
        _combine_kernel,
        grid=(S // tm,),
        in_specs=[pl.BlockSpec((tm, D_INNER), row_blk)] * 3 + [pl.BlockSpec((tm, LANES), row_blk)] * 3 + [
            pl.BlockSpec((tm, D_INNER), lambda m: (m, z_blk)),
            pl.BlockSpec((tm, D), row_blk),
            pl.BlockSpec((D_INNER, D), lambda m: (0, 0)),
            pl.BlockSpec((1, D), lambda m: (0, 0)),
        ],
        out_specs=pl.BlockSpec((tm, D), row_blk),
        out_shape=jax.ShapeDtypeStruct((S, D), F32),
        scratch_shapes=[pltpu.VMEM((tm, D_INNER), BF16), pltpu.VMEM((N_GROUPS - 1, tm, D_INNER), F32)],
        compiler_params=pltpu.CompilerParams(
            dimension_semantics=("arbitrary",), vmem_limit_bytes=VMEM_LIMIT),
        name="combine",
    )(*os_, *lses, proj, x, w_out, g_final)


def kernel(x, a_norm, a_w_in, a_w_gate_up, a_b_gate, a_g_out, a_w_out, kv_norm, w_kv, b_norm, b_w_in, b_w_out,
           final_norm):
    B, S, D = x.shape
    assert B == 1 and D == D_MODEL and a_norm.shape[0] == 1 and b_norm.shape[0] == 1
    assert S % (DIL_BLOCK * DIL_GROUPS[-1][1]) == 0
    x0 = x.reshape(S, D)

    w_in = a_w_in[0]
    qk, vz, lr0 = w_in[:, :2 * GLA_DK], w_in[:, 2 * GLA_DK:2 * GLA_DK + 2 * D_INNER], 2 * GLA_DK + 2 * D_INNER
    w_a = jnp.concatenate([vz, qk], axis=1).astype(BF16)
    w_lr = jnp.pad(w_in[:, lr0:], ((0, 0), (0, LANES - GLA_GATE_RANK))).astype(BF16)
    tn_a = w_a.shape[1] // 2
    proj_a, lr = _norm_proj(x0, a_norm, [(0, 1)], jnp.zeros((2,), jnp.int32), w_a, w_lr, tm=1024, tn=tn_a)
    w_gu = jnp.pad(a_w_gate_up[0], ((0, LANES - GLA_GATE_RANK), (0, 0))).astype(BF16)
    x1 = _gla_layer(x0, proj_a, lr, w_gu, a_b_gate, a_g_out, a_w_out[0].astype(BF16), tm=256)

    w_b = jnp.concatenate([w_kv, b_w_in[0]], axis=1).astype(BF16)
    q_scale = DIL_HEAD_DIM ** -0.5
    gains = jnp.stack([kv_norm, b_norm[0] * q_scale, b_norm[0]])
    dils = [d for _, d in DIL_GROUPS]
    variants = [(0, d) for d in dils] + [(1, d) for d in dils] + [(2, 1)]
    tn_b = D_INNER // 2
    var_of_segment = list(range(N_GROUPS)) * 2 + [N_GROUPS + g for g in range(N_GROUPS)] + [2 * N_GROUPS]
    var_of_block = jnp.array([v for v in var_of_segment for _ in range(D_INNER // tn_b)], jnp.int32)
    proj_b = _norm_proj(x1, gains, variants, var_of_block, w_b, None, tm=1024, tn=tn_b)

    outs = [_dil_attn_group(proj_b, g, S=S) for g in range(N_GROUPS)]
    out = _combine([o for o, _ in outs], [l for _, l in outs], proj_b, x1, b_w_out[0].astype(BF16),
                   final_norm.reshape(1, D), tm=512)
    return out.reshape(B, S, D)
```

```python
import functools

import jax
import jax.numpy as jnp
from jax import lax
from jax.experimental import pallas as pl
from jax.experimental.pallas import tpu as pltpu

F32 = jnp.float32
BF16 = jnp.bfloat16

RMS_EPS = 1e-6
D_MODEL = 1024
D_INNER = 2048
GLA_HEADS = 4
GLA_DK_HEAD = 128
GLA_DV_HEAD = 512
GLA_DK = GLA_HEADS * GLA_DK_HEAD
GLA_GATE_RANK = 16
GLA_GATE_TAU = 16.0
GLA_CHUNK = 64
GLA_SUB = 16
DIL_GROUPS = ((128, 1), (512, 4), (2048, 16))
N_GROUPS = 3
DIL_HEADS = 16
DIL_HEAD_DIM = 128
DIL_BLOCK = 128
ALIBI_MAX_EXP = 8.0
MASK_DISTANCE = 1e34
PERM_ROWS = 256
LOG2_E = 1.4426950408889634
LN_2 = 0.6931471805599453
SCORE_LOOKAHEAD = 4

LANES = 128
VMEM_LIMIT = 56 * 1024 * 1024

_NT = (((1,), (1,)), ((), ()))


def _deinterleave_matrix(d, transpose=False):
    n = PERM_ROWS // d
    p = lax.broadcasted_iota(jnp.int32, (PERM_ROWS, PERM_ROWS), 1 if transpose else 0)
    j = lax.broadcasted_iota(jnp.int32, (PERM_ROWS, PERM_ROWS), 0 if transpose else 1)
    return (j == (p % n) * d + p // n).astype(BF16)


def _norm_proj_kernel(var_ref, x_ref, g_ref, w_ref, *rest, variants, has_aux):
    if has_aux:
        waux_ref, o_ref, aux_ref, h_ref = rest
    else:
        o_ref, h_ref = rest
    n = pl.program_id(1)
    tm = x_ref.shape[0]

    @pl.when(n == 0)
    def _():
        x = x_ref[...]
        y = x * lax.rsqrt(jnp.mean(x * x, axis=-1, keepdims=True) + RMS_EPS)
        for gi in sorted({g for g, _ in variants}):
            h = (y * g_ref[gi:gi + 1, :]).astype(BF16)
            for vi, (g, d) in enumerate(variants):
                if g != gi:
                    continue
                if d == 1:
                    h_ref[vi] = h
                else:
                    perm = _deinterleave_matrix(d)
                    for b in range(tm // PERM_ROWS):
                        rows = slice(b * PERM_ROWS, (b + 1) * PERM_ROWS)
                        h_ref[vi, rows, :] = jnp.dot(perm, h[rows], preferred_element_type=F32).astype(BF16)
        if has_aux:
            aux_ref[...] = jnp.dot(h_ref[0], waux_ref[...], preferred_element_type=F32)

    h = h_ref[var_ref[n]]
    o_ref[...] = jnp.dot(h, w_ref[...], preferred_element_type=F32).astype(o_ref.dtype)


def _norm_proj(x, gains, variants, var_of_block, w, w_aux, *, tm, tn):
    S, D = x.shape
    N = w.shape[1]
    n_gains = gains.shape[0]
    has_aux = w_aux is not None
    grid = (S // tm, N // tn)
    in_specs = [
        pl.BlockSpec((tm, D), lambda m, n, seg: (m, 0)),
        pl.BlockSpec((n_gains, D), lambda m, n, seg: (0, 0)),
        pl.BlockSpec((D, tn), lambda m, n, seg: (0, n)),
    ]
    out_shape = [jax.ShapeDtypeStruct((S, N), BF16)]
    out_specs = [pl.BlockSpec((tm, tn), lambda m, n, seg: (m, n))]
    args = [x, gains, w]
    if has_aux:
        in_specs.append(pl.BlockSpec((D, LANES), lambda m, n, seg: (0, 0)))
        out_shape.append(jax.ShapeDtypeStruct((S, LANES), F32))
        out_specs.append(pl.BlockSpec((tm, LANES), lambda m, n, seg: (m, 0)))
        args.append(w_aux)
    res = pl.pallas_call(
        functools.partial(_norm_proj_kernel, variants=tuple(variants), has_aux=has_aux),
        grid_spec=pltpu.PrefetchScalarGridSpec(
            num_scalar_prefetch=1, grid=grid, in_specs=in_specs, out_specs=out_specs,
            scratch_shapes=[pltpu.VMEM((len(variants), tm, D), BF16)]),
        out_shape=out_shape,
        compiler_params=pltpu.CompilerParams(
            dimension_semantics=("arbitrary", "arbitrary"), vmem_limit_bytes=VMEM_LIMIT),
        name="norm_proj_aux" if has_aux else "norm_proj",
    )(var_of_block, *args)
    return res if has_aux else res[0]


def _split3(a):
    hi = a.astype(BF16)
    r1 = a - hi.astype(F32)
    mid = r1.astype(BF16)
    lo = (r1 - mid.astype(F32)).astype(BF16)
    return hi, mid, lo


def _gla_head_chunk(b, q, k, v, state):
    C, SUB = GLA_CHUNK, GLA_SUB
    dk = q.shape[1]
    b_last = b[C - 1:C, :]
    o = jnp.dot((q * jnp.exp(b)).astype(BF16), state.astype(BF16), preferred_element_type=F32)

    lane = lax.broadcasted_iota(jnp.int32, (SUB, LANES), 1)
    row = lax.broadcasted_iota(jnp.int32, (SUB, LANES), 0)
    lane_c = lax.broadcasted_iota(jnp.int32, (SUB, C), 1)
    ones =jnp.ones((dk, LANES), BF16)
    a_rows = []
    for i in range(C // SUB):
        r0 = i * SUB
        bq, qq, kk = b[r0:r0 + SUB], q[r0:r0 + SUB], k[r0:r0 + SUB]
        prods = []
        for s in range(SUB):
            dec = jnp.exp(jnp.minimum(bq - bq[s:s + 1, :], 0.0))
            prods.append(qq * dec * kk[s:s + 1, :])
        sums = jnp.dot(jnp.concatenate(prods, axis=0).astype(BF16), ones, preferred_element_type=F32)
        a_i = jnp.zeros((SUB, LANES), F32)
        for s in range(SUB):
            a_i = jnp.where(lane == r0 + s, sums[s * SUB:(s + 1) * SUB, :], a_i)
        a_i = jnp.where(lane <= row + r0, a_i, 0.0)[:, :C]
        if i > 0:
            b_ref = bq[0:1, :]
            qt = (qq * jnp.exp(bq - b_ref)).astype(BF16)
            kt = (k * jnp.exp(jnp.minimum(b_ref - b, 0.0))).astype(BF16)
            off = lax.dot_general(qt, kt, _NT, preferred_element_type=F32)
            a_i = jnp.where(lane_c < r0, off, a_i)
        a_rows.append(a_i)
    a = jnp.concatenate(a_rows, axis=0).astype(BF16)
    o = o + jnp.dot(a, v, preferred_element_type=F32)

    k_dec = k * jnp.exp(b_last - b)
    upd = jnp.dot(k_dec.T.astype(BF16), v, preferred_element_type=F32)
    e_col = jnp.broadcast_to(jnp.exp(b_last), (dk, dk)).T
    e_full = jnp.concatenate([e_col] * (v.shape[1] // dk), axis=1)
    return o, state * e_full + upd


def _gla_kernel(q_ref, k_ref, v_ref, z_ref, lr_ref, x_ref, wgu_ref, bg_ref, gout_ref, wout_ref,
                o_ref, state_ref, y_ref, b_ref):
    tm = q_ref.shape[0]
    C = GLA_CHUNK

    @pl.when(pl.program_id(0) == 0)
    def _():
        state_ref[...] = jnp.zeros_like(state_ref)

    pre = jnp.dot(lr_ref[...].astype(BF16), wgu_ref[...], preferred_element_type=F32) + bg_ref[...]
    log_a = (jnp.minimum(pre, 0.0) - jnp.log1p(jnp.exp(-jnp.abs(pre)))) * (1.0 / GLA_GATE_TAU)
    r = lax.broadcasted_iota(jnp.int32, (tm, tm), 0)
    c = lax.broadcasted_iota(jnp.int32, (tm, tm), 1)
    tri = ((c <= r) & ((c // C) == (r // C))).astype(BF16)
    hi, mid, lo = _split3(log_a)
    b_ref[...] = (jnp.dot(tri, hi, preferred_element_type=F32)
                  + jnp.dot(tri, mid, preferred_element_type=F32)
                  + jnp.dot(tri, lo, preferred_element_type=F32))

    gout = gout_ref[...]

    def chunk_body(ci, carry):
        rows = pl.ds(pl.multiple_of(ci * C, C), C)
        for h in range(GLA_HEADS):
            kl = slice(h * GLA_DK_HEAD, (h + 1) * GLA_DK_HEAD)
            vl = slice(h * GLA_DV_HEAD, (h + 1) * GLA_DV_HEAD)
            q = q_ref[rows, kl].astype(F32) * (GLA_DK_HEAD ** -0.5)
            k = k_ref[rows, kl].astype(F32)
            o, new_state = _gla_head_chunk(b_ref[rows, kl], q, k, v_ref[rows, vl], state_ref[h])
            state_ref[h] = new_state
            o = o * lax.rsqrt(jnp.mean(o * o, axis=-1, keepdims=True) + RMS_EPS) * gout
            z = z_ref[rows, vl].astype(F32)
            y_ref[rows, vl] = (o * (z * jax.nn.sigmoid(z))).astype(BF16)
        return carry

    lax.fori_loop(0, tm // C, chunk_body, 0)
    o_ref[...] = x_ref[...] + jnp.dot(y_ref[...], wout_ref[...], preferred_element_type=F32)


def _gla_layer(x, proj, lr, w_gate_up, b_gate, g_out, w_out, *, tm):
    S, D = x.shape
    qk_blk = 2 * D_INNER // GLA_DK
    return pl.pallas_call(
        _gla_kernel,
        grid=(S // tm,),
        in_specs=[
            pl.BlockSpec((tm, GLA_DK), lambda m: (m, qk_blk)),
            pl.BlockSpec((tm, GLA_DK), lambda m: (m, qk_blk + 1)),
            pl.BlockSpec((tm, D_INNER), lambda m: (m, 0)),
            pl.BlockSpec((tm, D_INNER), lambda m: (m, 1)),
            pl.BlockSpec((tm, LANES), lambda m: (m, 0)),
            pl.BlockSpec((tm, D), lambda m: (m, 0)),
            pl.BlockSpec((LANES, GLA_DK), lambda m: (0, 0)),
            pl.BlockSpec((1, GLA_DK), lambda m: (0, 0)),
            pl.BlockSpec((1, GLA_DV_HEAD), lambda m: (0, 0)),
            pl.BlockSpec((D_INNER, D), lambda m: (0, 0)),
        ],
        out_specs=pl.BlockSpec((tm, D), lambda m: (m, 0)),
        out_shape=jax.ShapeDtypeStruct((S, D), F32),
        scratch_shapes=[
            pltpu.VMEM((GLA_HEADS, GLA_DK_HEAD, GLA_DV_HEAD), F32),
            pltpu.VMEM((tm, D_INNER), BF16),
            pltpu.VMEM((tm, GLA_DK), F32),
        ],
        compiler_params=pltpu.CompilerParams(
            dimension_semantics=("arbitrary",), vmem_limit_bytes=VMEM_LIMIT),
        name="gla",
    )(proj, proj, proj, proj, lr, x, w_gate_up, b_gate, g_out, w_out)


def _head_tile(ref, hl):
    t = ref[:, hl] if len(ref.shape) == 2 else ref[:, :, hl]
    return t.reshape(DIL_BLOCK, t.shape[-1])


def _dil_attn_kernel(q_ref, kc_ref, vc_ref, o_ref, lse_ref, kp_ref, vp_ref, bias_ref, lse_acc, *, slopes, dilation):
    blk = DIL_BLOCK
    i = pl.program_id(1)

    @pl.when((pl.program_id(0) == 0) & (i == 0))
    def _():
        row = lax.broadcasted_iota(jnp.int32, (blk, blk), 0)
        col = lax.broadcasted_iota(jnp.int32, (blk, blk), 1)
        dist_prev = jnp.where(col >= row, ((row + blk - col) * dilation).astype(F32), MASK_DISTANCE)
        dist_cur = jnp.where(col <= row, ((row - col) * dilation).astype(F32), MASK_DISTANCE)
        for h in range(DIL_HEADS):
            bias_ref[h, :, :blk] = -(slopes[h] * LOG2_E) * dist_prev
            bias_ref[h, :, blk:] = -(slopes[h] * LOG2_E) * dist_cur

    def all_heads(use_prev):
        def scores(h):
            hl = slice(h * DIL_HEAD_DIM, (h + 1) * DIL_HEAD_DIM)
            q = _head_tile(q_ref, hl)
            s_cur = lax.dot_general(q, _head_tile(kc_ref, hl), _NT, preferred_element_type=F32) + bias_ref[h, :, blk:]
            if not use_prev:
                return None, s_cur
            s_prev = lax.dot_general(q, _head_tile(kp_ref, hl), _NT, preferred_element_type=F32) + bias_ref[h, :, :blk]
            return s_prev, s_cur

        ahead = [scores(h) for h in range(SCORE_LOOKAHEAD)]
        for h in range(DIL_HEADS):
            hl = slice(h * DIL_HEAD_DIM, (h + 1) * DIL_HEAD_DIM)
            s_prev, s_cur = ahead.pop(0)
            if h + SCORE_LOOKAHEAD < DIL_HEADS:
                ahead.append(scores(h + SCORE_LOOKAHEAD))
            if use_prev:
                m = jnp.max(jnp.maximum(s_prev, s_cur), axis=-1, keepdims=True)
                p_prev = jnp.exp2(s_prev - m)
                p_cur = jnp.exp2(s_cur - m)
                l = jnp.sum(p_prev + p_cur, axis=-1, keepdims=True)
                acc = (jnp.dot(p_prev.astype(BF16), _head_tile(vp_ref, hl), preferred_element_type=F32)
                       + jnp.dot(p_cur.astype(BF16), _head_tile(vc_ref, hl), preferred_element_type=F32))
            else:
                m = jnp.max(s_cur, axis=-1, keepdims=True)
                p_cur = jnp.exp2(s_cur - m)
                l = jnp.sum(p_cur, axis=-1, keepdims=True)
                acc = jnp.dot(p_cur.astype(BF16), _head_tile(vc_ref, hl), preferred_element_type=F32)
            o = (acc / l).astype(o_ref.dtype)
            if len(o_ref.shape) == 2:
                o_ref[:, hl] = o
            else:
                o_ref[:, :, hl] = o.reshape(o_ref.shape[0], o_ref.shape[1], DIL_HEAD_DIM)
            lse_acc[:, h:h + 1] = m * LN_2 + jnp.log(l)

    lse_acc[...] = jnp.zeros_like(lse_acc)

    @pl.when(i == 0)
    def _():
        all_heads(False)

    @pl.when(i > 0)
    def _():
        all_heads(True)

    lse_ref[...] = lse_acc[...].reshape(lse_ref.shape)
    kp_ref[...] = kc_ref[...]
    vp_ref[...] = vc_ref[...]


def _dil_attn_group(proj, g, *, S):
    _, d = DIL_GROUPS[g]
    nb = S // d // DIL_BLOCK
    n_all = N_GROUPS * DIL_HEADS
    slopes = tuple(2.0 ** (-ALIBI_MAX_EXP * (g * DIL_HEADS + h + 1.0) / n_all) for h in range(DIL_HEADS))
    kcol, vcol, qcol = g, N_GROUPS + g, 2 * N_GROUPS + g
    if d == 1:
        pv = proj
        block = lambda width: (DIL_BLOCK, width)
        spec = lambda width, col: pl.BlockSpec(block(width), lambda r, i: (i, col))
        shape = lambda width, dt: jax.ShapeDtypeStruct((S, width), dt)
    else:
        run = PERM_ROWS // d
        pieces = DIL_BLOCK // run
        pv = proj.reshape(S // PERM_ROWS, d, run, proj.shape[1])
        block = lambda width: (pieces, run, width)
        spec = lambda width, col: pl.BlockSpec((pieces, None, run, width), lambda r, i: (i, r, 0, col))
        shape = lambda width, dt: jax.ShapeDtypeStruct((S // PERM_ROWS, d, run, width), dt)
    o, lse = pl.pallas_call(
        functools.partial(_dil_attn_kernel, slopes=slopes, dilation=d),
        grid=(d, nb),
        in_specs=[spec(D_INNER, qcol), spec(D_INNER, kcol), spec(D_INNER, vcol)],
        out_specs=[spec(D_INNER, 0), spec(LANES, 0)],
        out_shape=[shape(D_INNER, BF16), shape(LANES, F32)],
        scratch_shapes=[
            pltpu.VMEM(block(D_INNER), BF16),
            pltpu.VMEM(block(D_INNER), BF16),
            pltpu.VMEM((DIL_HEADS, DIL_BLOCK, 2 * DIL_BLOCK), F32),
            pltpu.VMEM((DIL_BLOCK, LANES), F32),
        ],
        compiler_params=pltpu.CompilerParams(
            dimension_semantics=("arbitrary", "arbitrary"), vmem_limit_bytes=VMEM_LIMIT),
        name=f"dil_attn_g{g}",
    )(pv, pv, pv)
    return o.reshape(S, D_INNER), lse.reshape(S, LANES)


def _combine_kernel(o0_ref, o1_ref, o2_ref, l0_ref, l1_ref, l2_ref, z_ref, x_ref, wout_ref, g_ref,
                    out_ref, y_ref, ot_ref):
    tm = x_ref.shape[0]
    o_refs = (o0_ref, o1_ref, o2_ref)
    l_refs = (l0_ref, l1_ref, l2_ref)
    lses = [l0_ref[...]]
    for g in range(1, N_GROUPS):
        unperm = _deinterleave_matrix(DIL_GROUPS[g][1], transpose=True)
        parts = []
        for b in range(tm // PERM_ROWS):
            rows = slice(b * PERM_ROWS, (b + 1) * PERM_ROWS)
            ot_ref[g - 1, rows, :] = jnp.dot(unperm, o_refs[g][rows, :], preferred_element_type=F32)
            parts.append(sum(jnp.dot(unperm, piece, preferred_element_type=F32)
                             for piece in _split3(l_refs[g][rows, :])))
        lses.append(jnp.concatenate(parts, axis=0))
    m = jnp.maximum(jnp.maximum(lses[0], lses[1]), lses[2])
    es = [jnp.exp(l - m) for l in lses]
    den = es[0] + es[1] + es[2]
    ws = [e / den for e in es]
    for h in range(DIL_HEADS):
        hl = slice(h * DIL_HEAD_DIM, (h + 1) * DIL_HEAD_DIM)
        acc = ws[0][:, h:h + 1] * o_refs[0][:, hl].astype(F32)
        for g in range(1, N_GROUPS):
            acc = acc + ws[g][:, h:h + 1] * ot_ref[g - 1, :, hl]
        z = z_ref[:, hl].astype(F32)
        y_ref[:, hl] = (acc * (z * jax.nn.sigmoid(z))).astype(BF16)
    x = x_ref[...] + jnp.dot(y_ref[...], wout_ref[...], preferred_element_type=F32)
    y = x * lax.rsqrt(jnp.mean(x * x, axis=-1, keepdims=True) + RMS_EPS)
    out_ref[...] = y * g_ref[...]


def _combine(os_, lses, proj, x, w_out, g_final, *, tm):
    S, D = x.shape
    z_blk = proj.shape[1] // D_INNER - 1
    row_blk = lambda m: (m, 0)
    return pl.pallas_call(
        _combine_kernel,
        grid=(S // tm,),
        in_specs=[pl.BlockSpec((tm, D_INNER), row_blk)] * 3 + [pl.BlockSpec((tm, LANES), row_blk)] * 3 + [
            pl.BlockSpec((tm, D_INNER), lambda m: (m, z_blk)),
            pl.BlockSpec((tm, D), row_blk),
            pl.BlockSpec((D_INNER, D), lambda m: (0, 0)),
            pl.BlockSpec((1, D), lambda m: (0, 0)),
        ],
        out_specs=pl.BlockSpec((tm, D), row_blk),
        out_shape=jax.ShapeDtypeStruct((S, D), F32),
        scratch_shapes=[pltpu.VMEM((tm, D_INNER), BF16), pltpu.VMEM((N_GROUPS - 1, tm, D_INNER), F32)],
        compiler_params=pltpu.CompilerParams(
            dimension_semantics=("arbitrary",), vmem_limit_bytes=VMEM_LIMIT),
        name="combine",
    )(*os_, *lses, proj, x, w_out, g_final)


def kernel(x, a_norm, a_w_in, a_w_gate_up, a_b_gate, a_g_out, a_w_out, kv_norm, w_kv, b_norm, b_w_in, b_w_out,
           final_norm):
    B, S, D = x.shape
    assert B == 1 and D == D_MODEL and a_norm.shape[0] == 1 and b_norm.shape[0] == 1
    assert S % (DIL_BLOCK * DIL_GROUPS[-1][1]) == 0
    x0 = x.reshape(S, D)

    w_in = a_w_in[0]
    qk, vz, lr0 = w_in[:, :2 * GLA_DK], w_in[:, 2 * GLA_DK:2 * GLA_DK + 2 * D_INNER], 2 * GLA_DK + 2 * D_INNER
    w_a = jnp.concatenate([vz, qk], axis=1).astype(BF16)
    w_lr = jnp.pad(w_in[:, lr0:], ((0, 0), (0, LANES - GLA_GATE_RANK))).astype(BF16)
    tn_a = w_a.shape[1] // 2
    proj_a, lr = _norm_proj(x0, a_norm, [(0, 1)], jnp.zeros((2,), jnp.int32), w_a, w_lr, tm=1024, tn=tn_a)
    w_gu = jnp.pad(a_w_gate_up[0], ((0, LANES - GLA_GATE_RANK), (0, 0))).astype(BF16)
    x1 = _gla_layer(x0, proj_a, lr, w_gu, a_b_gate, a_g_out, a_w_out[0].astype(BF16), tm=256)

    w_b = jnp.concatenate([w_kv, b_w_in[0]], axis=1).astype(BF16)
    q_scale = DIL_HEAD_DIM ** -0.5 * LOG2_E
    gains = jnp.stack([kv_norm, b_norm[0] * q_scale, b_norm[0]])
    dils = [d for _, d in DIL_GROUPS]
    variants = [(0, d) for d in dils] + [(1, d) for d in dils] + [(2, 1)]
    tn_b = D_INNER // 2
    var_of_segment = list(range(N_GROUPS)) * 2 + [N_GROUPS + g for g in range(N_GROUPS)] + [2 * N_GROUPS]
    var_of_block = jnp.array([v for v in var_of_segment for _ in range(D_INNER // tn_b)], jnp.int32)
    proj_b = _norm_proj(x1, gains, variants, var_of_block, w_b, None, tm=1024, tn=tn_b)

    outs = [_dil_attn_group(proj_b, g, S=S) for g in range(N_GROUPS)]
    out = _combine([o for o, _ in outs], [l for _, l in outs], proj_b, x1, b_w_out[0].astype(BF16),
                   final_norm.reshape(1, D), tm=512)
    return out.reshape(B, S, D)
```

```python
import functools

import jax
import jax.numpy as jnp
from jax import lax
from jax.experimental import pallas as pl
from jax.experimental.pallas import tpu as pltpu

F32 = jnp.float32
BF16 = jnp.bfloat16

RMS_EPS = 1e-6
D_MODEL = 1024
D_INNER = 2048
GLA_HEADS = 4
GLA_DK_HEAD = 128
GLA_DV_HEAD = 512
GLA_DK = GLA_HEADS * GLA_DK_HEAD
GLA_GATE_RANK = 16
GLA_GATE_TAU = 16.0
GLA_CHUNK = 64
GLA_SUB = 16
DIL_GROUPS = ((128, 1), (512, 4), (2048, 16))
N_GROUPS = 3
DIL_HEADS = 16
DIL_HEAD_DIM = 128
DIL_BLOCK = 128
ALIBI_MAX_EXP = 8.0
MASK_DISTANCE = 1e34
PERM_ROWS = 256
LOG2_E = 1.4426950408889634
SCORE_LOOKAHEAD = 4

LANES = 128
VMEM_LIMIT = 56 * 1024 * 1024

_NT = (((1,), (1,)), ((), ()))


def _deinterleave_matrix(d, transpose=False):
    n = PERM_ROWS // d
    p = lax.broadcasted_iota(jnp.int32, (PERM_ROWS, PERM_ROWS), 1 if transpose else 0)
    j = lax.broadcasted_iota(jnp.int32, (PERM_ROWS, PERM_ROWS), 0 if transpose else 1)
    return (j == (p % n) * d + p // n).astype(BF16)


def _norm_proj_kernel(var_ref, x_ref, g_ref, w_ref, *rest, variants, has_aux):
    if has_aux:
        waux_ref, o_ref, aux_ref, h_ref = rest
    else:
        o_ref, h_ref = rest
    n = pl.program_id(1)
    tm = x_ref.shape[0]

    @pl.when(n == 0)
    def _():
        x = x_ref[...]
        y = x * lax.rsqrt(jnp.mean(x * x, axis=-1, keepdims=True) + RMS_EPS)
        for gi in sorted({g for g, _ in variants}):
            h = (y * g_ref[gi:gi + 1, :]).astype(BF16)
            for vi, (g, d) in enumerate(variants):
                if g != gi:
                    continue
                if d == 1:
                    h_ref[vi] = h
                else:
                    perm = _deinterleave_matrix(d)
                    for b in range(tm // PERM_ROWS):
                        rows = slice(b * PERM_ROWS, (b + 1) * PERM_ROWS)
                        h_ref[vi, rows, :] = jnp.dot(perm, h[rows], preferred_element_type=F32).astype(BF16)
        if has_aux:
            aux_ref[...] = jnp.dot(h_ref[0], waux_ref[...], preferred_element_type=F32)

    h = h_ref[var_ref[n]]
    o_ref[...] = jnp.dot(h, w_ref[...], preferred_element_type=F32).astype(o_ref.dtype)


def _norm_proj(x, gains, variants, var_of_block, w, w_aux, *, tm, tn):
    S, D = x.shape
    N = w.shape[1]
    n_gains = gains.shape[0]
    has_aux = w_aux is not None
    grid = (S // tm, N // tn)
    in_specs = [
        pl.BlockSpec((tm, D), lambda m, n, seg: (m, 0)),
        pl.BlockSpec((n_gains, D), lambda m, n, seg: (0, 0)),
        pl.BlockSpec((D, tn), lambda m, n, seg: (0, n)),
    ]
    out_shape = [jax.ShapeDtypeStruct((S, N), BF16)]
    out_specs = [pl.BlockSpec((tm, tn), lambda m, n, seg: (m, n))]
    args = [x, gains, w]
    if has_aux:
        in_specs.append(pl.BlockSpec((D, LANES), lambda m, n, seg: (0, 0)))
        out_shape.append(jax.ShapeDtypeStruct((S, LANES), F32))
        out_specs.append(pl.BlockSpec((tm, LANES), lambda m, n, seg: (m, 0)))
        args.append(w_aux)
    res = pl.pallas_call(
        functools.partial(_norm_proj_kernel, variants=tuple(variants), has_aux=has_aux),
        grid_spec=pltpu.PrefetchScalarGridSpec(
            num_scalar_prefetch=1, grid=grid, in_specs=in_specs, out_specs=out_specs,
            scratch_shapes=[pltpu.VMEM((len(variants), tm, D), BF16)]),
        out_shape=out_shape,
        compiler_params=pltpu.CompilerParams(
            dimension_semantics=("arbitrary", "arbitrary"), vmem_limit_bytes=VMEM_LIMIT),
        name="norm_proj_aux" if has_aux else "norm_proj",
    )(var_of_block, *args)
    return res if has_aux else res[0]


def _split3(a):
    hi = a.astype(BF16)
    r1 = a - hi.astype(F32)
    mid = r1.astype(BF16)
    lo = (r1 - mid.astype(F32)).astype(BF16)
    return hi, mid, lo


def _gla_head_chunk(b, q, k, v, state):
    C, SUB = GLA_CHUNK, GLA_SUB
    dk = q.shape[1]
    b_last = b[C - 1:C, :]
    o = jnp.dot((q * jnp.exp(b)).astype(BF16), state.astype(BF16), preferred_element_type=F32)

    lane = lax.broadcasted_iota(jnp.int32, (SUB, LANES), 1)
    row = lax.broadcasted_iota(jnp.int32, (SUB, LANES), 0)
    lane_c = lax.broadcasted_iota(jnp.int32, (SUB, C), 1)
    ones =jnp.ones((dk, LANES), BF16)
    a_rows = []
    for i in range(C // SUB):
        r0 = i * SUB
        bq, qq, kk = b[r0:r0 + SUB], q[r0:r0 + SUB], k[r0:r0 + SUB]
        prods = []
        for s in range(SUB):
            dec = jnp.exp(jnp.minimum(bq - bq[s:s + 1, :], 0.0))
            prods.append(qq * dec * kk[s:s + 1, :])
        sums = jnp.dot(jnp.concatenate(prods, axis=0).astype(BF16), ones, preferred_element_type=F32)
        a_i = jnp.zeros((SUB, LANES), F32)
        for s in range(SUB):
            a_i = jnp.where(lane == r0 + s, sums[s * SUB:(s + 1) * SUB, :], a_i)
        a_i = jnp.where(lane <= row + r0, a_i, 0.0)[:, :C]
        if i > 0:
            b_ref = bq[0:1, :]
            qt = (qq * jnp.exp(bq - b_ref)).astype(BF16)
            kt = (k * jnp.exp(jnp.minimum(b_ref - b, 0.0))).astype(BF16)
            off = lax.dot_general(qt, kt, _NT, preferred_element_type=F32)
            a_i = jnp.where(lane_c < r0, off, a_i)
        a_rows.append(a_i)
    a = jnp.concatenate(a_rows, axis=0).astype(BF16)
    o = o + jnp.dot(a, v, preferred_element_type=F32)

    k_dec = k * jnp.exp(b_last - b)
    upd = jnp.dot(k_dec.T.astype(BF16), v, preferred_element_type=F32)
    e_col = jnp.broadcast_to(jnp.exp(b_last), (dk, dk)).T
    e_full = jnp.concatenate([e_col] * (v.shape[1] // dk), axis=1)
    return o, state * e_full + upd


def _gla_kernel(q_ref, k_ref, v_ref, z_ref, lr_ref, x_ref, wgu_ref, bg_ref, gout_ref, wout_ref,
                o_ref, state_ref, y_ref, b_ref):
    tm = q_ref.shape[0]
    C = GLA_CHUNK

    @pl.when(pl.program_id(0) == 0)
    def _():
        state_ref[...] = jnp.zeros_like(state_ref)

    pre = jnp.dot(lr_ref[...].astype(BF16), wgu_ref[...], preferred_element_type=F32) + bg_ref[...]
    log_a = (jnp.minimum(pre, 0.0) - jnp.log1p(jnp.exp(-jnp.abs(pre)))) * (1.0 / GLA_GATE_TAU)
    r = lax.broadcasted_iota(jnp.int32, (tm, tm), 0)
    c = lax.broadcasted_iota(jnp.int32, (tm, tm), 1)
    tri = ((c <= r) & ((c // C) == (r // C))).astype(BF16)
    hi, mid, lo = _split3(log_a)
    b_ref[...] = (jnp.dot(tri, hi, preferred_element_type=F32)
                  + jnp.dot(tri, mid, preferred_element_type=F32)
                  + jnp.dot(tri, lo, preferred_element_type=F32))

    gout = gout_ref[...]

    def chunk_body(ci, carry):
        rows = pl.ds(pl.multiple_of(ci * C, C), C)
        for h in range(GLA_HEADS):
            kl = slice(h * GLA_DK_HEAD, (h + 1) * GLA_DK_HEAD)
            vl = slice(h * GLA_DV_HEAD, (h + 1) * GLA_DV_HEAD)
            q = q_ref[rows, kl].astype(F32) * (GLA_DK_HEAD ** -0.5)
            k = k_ref[rows, kl].astype(F32)
            o, new_state = _gla_head_chunk(b_ref[rows, kl], q, k, v_ref[rows, vl], state_ref[h])
            state_ref[h] = new_state
            o = o * lax.rsqrt(jnp.mean(o * o, axis=-1, keepdims=True) + RMS_EPS) * gout
            z = z_ref[rows, vl].astype(F32)
            y_ref[rows, vl] = (o * (z * jax.nn.sigmoid(z))).astype(BF16)
        return carry

    lax.fori_loop(0, tm // C, chunk_body, 0)
    o_ref[...] = x_ref[...] + jnp.dot(y_ref[...], wout_ref[...], preferred_element_type=F32)


def _gla_layer(x, proj, lr, w_gate_up, b_gate, g_out, w_out, *, tm):
    S, D = x.shape
    qk_blk = 2 * D_INNER // GLA_DK
    return pl.pallas_call(
        _gla_kernel,
        grid=(S // tm,),
        in_specs=[
            pl.BlockSpec((tm, GLA_DK), lambda m: (m, qk_blk)),
            pl.BlockSpec((tm, GLA_DK), lambda m: (m, qk_blk + 1)),
            pl.BlockSpec((tm, D_INNER), lambda m: (m, 0)),
            pl.BlockSpec((tm, D_INNER), lambda m: (m, 1)),
            pl.BlockSpec((tm, LANES), lambda m: (m, 0)),
            pl.BlockSpec((tm, D), lambda m: (m, 0)),
            pl.BlockSpec((LANES, GLA_DK), lambda m: (0, 0)),
            pl.BlockSpec((1, GLA_DK), lambda m: (0, 0)),
            pl.BlockSpec((1, GLA_DV_HEAD), lambda m: (0, 0)),
            pl.BlockSpec((D_INNER, D), lambda m: (0, 0)),
        ],
        out_specs=pl.BlockSpec((tm, D), lambda m: (m, 0)),
        out_shape=jax.ShapeDtypeStruct((S, D), F32),
        scratch_shapes=[
            pltpu.VMEM((GLA_HEADS, GLA_DK_HEAD, GLA_DV_HEAD), F32),
            pltpu.VMEM((tm, D_INNER), BF16),
            pltpu.VMEM((tm, GLA_DK), F32),
        ],
        compiler_params=pltpu.CompilerParams(
            dimension_semantics=("arbitrary",), vmem_limit_bytes=VMEM_LIMIT),
        name="gla",
    )(proj, proj, proj, proj, lr, x, w_gate_up, b_gate, g_out, w_out)


def _head_tile(ref, hl):
    t = ref[:, hl] if len(ref.shape) == 2 else ref[:, :, hl]
    return t.reshape(DIL_BLOCK, t.shape[-1])


def _dil_attn_kernel(q_ref, kc_ref, vc_ref, o_ref, m_ref, l_ref, kp_ref, vp_ref, bias_ref, m_acc, l_acc, *,
                     slopes, dilation):
    blk = DIL_BLOCK
    i = pl.program_id(1)

    @pl.when((pl.program_id(0) == 0) & (i == 0))
    def _():
        row = lax.broadcasted_iota(jnp.int32, (blk, blk), 0)
        col = lax.broadcasted_iota(jnp.int32, (blk, blk), 1)
        dist_prev = jnp.where(col >= row, ((row + blk - col) * dilation).astype(F32), MASK_DISTANCE)
        dist_cur = jnp.where(col <= row, ((row - col) * dilation).astype(F32), MASK_DISTANCE)
        for h in range(DIL_HEADS):
            bias_ref[h, :, :blk] = -(slopes[h] * LOG2_E) * dist_prev
            bias_ref[h, :, blk:] = -(slopes[h] * LOG2_E) * dist_cur

    def all_heads(use_prev):
        def scores(h):
            hl = slice(h * DIL_HEAD_DIM, (h + 1) * DIL_HEAD_DIM)
            q = _head_tile(q_ref, hl)
            s_cur = lax.dot_general(q, _head_tile(kc_ref, hl), _NT, preferred_element_type=F32) + bias_ref[h, :, blk:]
            if not use_prev:
                return None, s_cur
            s_prev = lax.dot_general(q, _head_tile(kp_ref, hl), _NT, preferred_element_type=F32) + bias_ref[h, :, :blk]
            return s_prev, s_cur

        ahead = [scores(h) for h in range(SCORE_LOOKAHEAD)]
        for h in range(DIL_HEADS):
            hl = slice(h * DIL_HEAD_DIM, (h + 1) * DIL_HEAD_DIM)
            s_prev, s_cur = ahead.pop(0)
            if h + SCORE_LOOKAHEAD < DIL_HEADS:
                ahead.append(scores(h + SCORE_LOOKAHEAD))
            if use_prev:
                m = jnp.max(jnp.maximum(s_prev, s_cur), axis=-1, keepdims=True)
                p_prev = jnp.exp2(s_prev - m)
                p_cur = jnp.exp2(s_cur - m)
                l = jnp.sum(p_prev + p_cur, axis=-1, keepdims=True)
                acc = (jnp.dot(p_prev.astype(BF16), _head_tile(vp_ref, hl), preferred_element_type=F32)
                       + jnp.dot(p_cur.astype(BF16), _head_tile(vc_ref, hl), preferred_element_type=F32))
            else:
                m = jnp.max(s_cur, axis=-1, keepdims=True)
                p_cur = jnp.exp2(s_cur - m)
                l = jnp.sum(p_cur, axis=-1, keepdims=True)
                acc = jnp.dot(p_cur.astype(BF16), _head_tile(vc_ref, hl), preferred_element_type=F32)
            o = acc.astype(o_ref.dtype)
            if len(o_ref.shape) == 2:
                o_ref[:, hl] = o
            else:
                o_ref[:, :, hl] = o.reshape(o_ref.shape[0], o_ref.shape[1], DIL_HEAD_DIM)
            m_acc[:, h:h + 1] = m
            l_acc[:, h:h + 1] = l

    m_acc[...] = jnp.zeros_like(m_acc)
    l_acc[...] = jnp.zeros_like(l_acc)

    @pl.when(i == 0)
    def _():
        all_heads(False)

    @pl.when(i > 0)
    def _():
        all_heads(True)

    m_ref[...] = m_acc[...].reshape(m_ref.shape)
    l_ref[...] = l_acc[...].reshape(l_ref.shape)
    kp_ref[...] = kc_ref[...]
    vp_ref[...] = vc_ref[...]


def _dil_attn_group(proj, g, *, S):
    _, d = DIL_GROUPS[g]
    nb = S // d // DIL_BLOCK
    n_all = N_GROUPS * DIL_HEADS
    slopes = tuple(2.0 ** (-ALIBI_MAX_EXP * (g * DIL_HEADS + h + 1.0) / n_all) for h in range(DIL_HEADS))
    kcol, vcol, qcol = g, N_GROUPS + g, 2 * N_GROUPS + g
    if d == 1:
        pv = proj
        block = lambda width: (DIL_BLOCK, width)
        spec = lambda width, col: pl.BlockSpec(block(width), lambda r, i: (i, col))
        shape = lambda width, dt: jax.ShapeDtypeStruct((S, width), dt)
    else:
        run = PERM_ROWS // d
        pieces = DIL_BLOCK // run
        pv = proj.reshape(S // PERM_ROWS, d, run, proj.shape[1])
        block = lambda width: (pieces, run, width)
        spec = lambda width, col: pl.BlockSpec((pieces, None, run, width), lambda r, i: (i, r, 0, col))
        shape = lambda width, dt: jax.ShapeDtypeStruct((S // PERM_ROWS, d, run, width), dt)
    o, m, l = pl.pallas_call(
        functools.partial(_dil_attn_kernel, slopes=slopes, dilation=d),
        grid=(d, nb),
        in_specs=[spec(D_INNER, qcol), spec(D_INNER, kcol), spec(D_INNER, vcol)],
        out_specs=[spec(D_INNER, 0), spec(LANES, 0), spec(LANES, 0)],
        out_shape=[shape(D_INNER, BF16), shape(LANES, F32), shape(LANES, F32)],
        scratch_shapes=[
            pltpu.VMEM(block(D_INNER), BF16),
            pltpu.VMEM(block(D_INNER), BF16),
            pltpu.VMEM((DIL_HEADS, DIL_BLOCK, 2 * DIL_BLOCK), F32),
            pltpu.VMEM((DIL_BLOCK, LANES), F32),
            pltpu.VMEM((DIL_BLOCK, LANES), F32),
        ],
        compiler_params=pltpu.CompilerParams(
            dimension_semantics=("arbitrary", "arbitrary"), vmem_limit_bytes=VMEM_LIMIT),
        name=f"dil_attn_g{g}",
    )(pv, pv, pv)
    return o.reshape(S, D_INNER), m.reshape(S, LANES), l.reshape(S, LANES)


def _combine_kernel(o0_ref, o1_ref, o2_ref, m0_ref, m1_ref, m2_ref, l0_ref, l1_ref, l2_ref, z_ref, x_ref,
                    wout_ref, g_ref, out_ref, y_ref, ot_ref):
    tm = x_ref.shape[0]
    o_refs = (o0_ref, o1_ref, o2_ref)
    m_refs = (m0_ref, m1_ref, m2_ref)
    l_refs = (l0_ref, l1_ref, l2_ref)
    lane = lax.broadcasted_iota(jnp.int32, (PERM_ROWS, LANES), 1)
    unperms = [None] + [_deinterleave_matrix(d, transpose=True) for _, d in DIL_GROUPS[1:]]

    def to_token_order(b):
        rows = slice(b * PERM_ROWS, (b + 1) * PERM_ROWS)
        ms, ls = [m_refs[0][rows, :]], [l_refs[0][rows, :]]
        for g in range(1, N_GROUPS):
            ot_ref[g - 1, rows, :] = jnp.dot(unperms[g], o_refs[g][rows, :], preferred_element_type=F32)
            for stats, ref in ((ms, m_refs[g]), (ls, l_refs[g])):
                stats.append(sum(jnp.dot(unperms[g], piece, preferred_element_type=F32)
                                 for piece in _split3(ref[rows, :])))
        return ms, ls

    def mix_and_project(b, ms, ls):
        rows = slice(b * PERM_ROWS, (b + 1) * PERM_ROWS)
        m = jnp.maximum(jnp.maximum(ms[0], ms[1]), ms[2])
        es = [jnp.exp2(mg - m) for mg in ms]
        den = es[0] * ls[0] + es[1] * ls[1] + es[2] * ls[2]
        ws = [jnp.where(lane < DIL_HEADS, e / den, 0.0) for e in es]
        for h in range(DIL_HEADS):
            hl = slice(h * DIL_HEAD_DIM, (h + 1) * DIL_HEAD_DIM)
            acc = ws[0][:, h:h + 1] * o_refs[0][rows, hl].astype(F32)
            for g in range(1, N_GROUPS):
                acc = acc + ws[g][:, h:h + 1] * ot_ref[g - 1, rows, hl]
            z = z_ref[rows, hl].astype(F32)
            y_ref[rows, hl] = (acc * (z * jax.nn.sigmoid(z))).astype(BF16)
        x = x_ref[rows, :] + jnp.dot(y_ref[rows, :], wout_ref[...], preferred_element_type=F32)
        y = x * lax.rsqrt(jnp.mean(x * x, axis=-1, keepdims=True) + RMS_EPS)
        out_ref[rows, :] = y * g_ref[...]

    n_blocks = tm // PERM_ROWS
    stats = to_token_order(0)
    for b in range(n_blocks):
        nxt = to_token_order(b + 1) if b + 1 < n_blocks else None
        mix_and_project(b, *stats)
        stats = nxt


def _combine(os_, ms, ls, proj, x, w_out, g_final, *, tm):
    S, D = x.shape
    z_blk = proj.shape[1] // D_INNER - 1
    row_blk = lambda m: (m, 0)
    return pl.pallas_call(
        _combine_kernel,
        grid=(S // tm,),
        in_specs=[pl.BlockSpec((tm, D_INNER), row_blk)] * 3 + [pl.BlockSpec((tm, LANES), row_blk)] * 6 + [
            pl.BlockSpec((tm, D_INNER), lambda m: (m, z_blk)),
            pl.BlockSpec((tm, D), row_blk),
            pl.BlockSpec((D_INNER, D), lambda m: (0, 0)),
            pl.BlockSpec((1, D), lambda m: (0, 0)),
        ],
        out_specs=pl.BlockSpec((tm, D), row_blk),
        out_shape=jax.ShapeDtypeStruct((S, D), F32),
        scratch_shapes=[pltpu.VMEM((tm, D_INNER), BF16), pltpu.VMEM((N_GROUPS - 1, tm, D_INNER), F32)],
        compiler_params=pltpu.CompilerParams(
            dimension_semantics=("arbitrary",), vmem_limit_bytes=VMEM_LIMIT),
        name="combine",
    )(*os_, *ms, *ls, proj, x, w_out, g_final)


def kernel(x, a_norm, a_w_in, a_w_gate_up, a_b_gate, a_g_out, a_w_out, kv_norm, w_kv, b_norm, b_w_in, b_w_out,
           final_norm):
    B, S, D = x.shape
    assert B == 1 and D == D_MODEL and a_norm.shape[0] == 1 and b_norm.shape[0] == 1
    assert S % (DIL_BLOCK * DIL_GROUPS[-1][1]) == 0
    x0 = x.reshape(S, D)

    w_in = a_w_in[0]
    qk, vz, lr0 = w_in[:, :2 * GLA_DK], w_in[:, 2 * GLA_DK:2 * GLA_DK + 2 * D_INNER], 2 * GLA_DK + 2 * D_INNER
    w_a = jnp.concatenate([vz, qk], axis=1).astype(BF16)
    w_lr = jnp.pad(w_in[:, lr0:], ((0, 0), (0, LANES - GLA_GATE_RANK))).astype(BF16)
    tn_a = w_a.shape[1] // 2
    proj_a, lr = _norm_proj(x0, a_norm, [(0, 1)], jnp.zeros((2,), jnp.int32), w_a, w_lr, tm=1024, tn=tn_a)
    w_gu = jnp.pad(a_w_gate_up[0], ((0, LANES - GLA_GATE_RANK), (0, 0))).astype(BF16)
    x1 = _gla_layer(x0, proj_a, lr, w_gu, a_b_gate, a_g_out, a_w_out[0].astype(BF16), tm=256)

    w_b = jnp.concatenate([w_kv, b_w_in[0]], axis=1).astype(BF16)
    q_scale = DIL_HEAD_DIM ** -0.5 * LOG2_E
    gains = jnp.stack([kv_norm, b_norm[0] * q_scale, b_norm[0]])
    dils = [d for _, d in DIL_GROUPS]
    variants = [(0, d) for d in dils] + [(1, d) for d in dils] + [(2, 1)]
    tn_b = D_INNER // 2
    var_of_segment = list(range(N_GROUPS)) * 2 + [N_GROUPS + g for g in range(N_GROUPS)] + [2 * N_GROUPS]
    var_of_block = jnp.array([v for v in var_of_segment for _ in range(D_INNER // tn_b)], jnp.int32)
    proj_b = _norm_proj(x1, gains, variants, var_of_block, w_b, None, tm=1024, tn=tn_b)

    outs = [_dil_attn_group(proj_b, g, S=S) for g in range(N_GROUPS)]
    out = _combine([o for o, _, _ in outs], [m for _, m, _ in outs], [l for _, _, l in outs], proj_b, x1,
                   b_w_out[0].astype(BF16), final_norm.reshape(1, D), tm=512)
    return out.reshape(B, S, D)
```

```python
import functools

import jax
import jax.numpy as jnp
from jax import lax
from jax.experimental import pallas as pl
from jax.experimental.pallas import tpu as pltpu

F32 = jnp.float32
BF16 = jnp.bfloat16

RMS_EPS = 1e-6
D_MODEL = 1024
D_INNER = 2048
GLA_HEADS = 4
GLA_DK_HEAD = 128
GLA_DV_HEAD = 512
GLA_DK = GLA_HEADS * GLA_DK_HEAD
GLA_GATE_RANK = 16
GLA_GATE_TAU = 16.0
GLA_CHUNK = 64
GLA_SUB = 16
DIL_GROUPS = ((128, 1), (512, 4), (2048, 16))
N_GROUPS = 3
DIL_HEADS = 16
DIL_HEAD_DIM = 128
DIL_BLOCK = 128
ALIBI_MAX_EXP = 8.0
MASK_DISTANCE = 1e34
PERM_ROWS = 256
LOG2_E = 1.4426950408889634
PROJ_COLS = 256
PROJ_LOOKAHEAD = 1
SCORE_LOOKAHEAD = 4

LANES = 128
VMEM_LIMIT = 56 * 1024 * 1024

_NT = (((1,), (1,)), ((), ()))


def _deinterleave_matrix(d, transpose=False):
    n = PERM_ROWS // d
    p = lax.broadcasted_iota(jnp.int32, (PERM_ROWS, PERM_ROWS), 1 if transpose else 0)
    j = lax.broadcasted_iota(jnp.int32, (PERM_ROWS, PERM_ROWS), 0 if transpose else 1)
    return (j == (p % n) * d + p // n).astype(BF16)


def _norm_proj_kernel(var_ref, x_ref, g_ref, w_ref, *rest, mm_variants, emit_variants, has_aux):
    rest = list(rest)
    waux_ref = rest.pop(0) if has_aux else None
    o_ref = rest.pop(0)
    aux_ref = rest.pop(0) if has_aux else None
    hv_ref = rest.pop(0) if emit_variants else None
    (h_ref,) = rest
    n = pl.program_id(1)
    tm = x_ref.shape[0]

    @pl.when(n == 0)
    def _():
        x = x_ref[...]
        y = x * lax.rsqrt(jnp.mean(x * x, axis=-1, keepdims=True) + RMS_EPS)
        targets = ([(h_ref, i, v) for i, v in enumerate(mm_variants)]
                   + [(hv_ref, i, v) for i, v in enumerate(emit_variants)])
        for gi in sorted({g for _, _, (g, _) in targets}):
            h = (y * g_ref[gi:gi + 1, :]).astype(BF16)
            for ref, i, (g, d) in targets:
                if g != gi:
                    continue
                if d == 1:
                    ref[i] = h
                else:
                    perm = _deinterleave_matrix(d)
                    for b in range(tm // PERM_ROWS):
                        rows = slice(b * PERM_ROWS, (b + 1) * PERM_ROWS)
                        ref[i, rows, :] = jnp.dot(perm, h[rows], preferred_element_type=F32).astype(BF16)
        if has_aux:
            aux_ref[...] = jnp.dot(h_ref[0], waux_ref[...], preferred_element_type=F32)

    h = h_ref[var_ref[n]]
    o_ref[...] = jnp.dot(h, w_ref[...], preferred_element_type=F32).astype(o_ref.dtype)


def _norm_proj(x, gains, mm_variants, var_of_block, w, w_aux, emit_variants=(), *, tm, tn):
    S, D = x.shape
    N = w.shape[1]
    n_gains = gains.shape[0]
    has_aux = w_aux is not None
    grid = (S // tm, N // tn)
    in_specs = [
        pl.BlockSpec((tm, D), lambda m, n, var: (m, 0)),
        pl.BlockSpec((n_gains, D), lambda m, n, var: (0, 0)),
        pl.BlockSpec((D, tn), lambda m, n, var: (0, n)),
    ]
    out_shape = [jax.ShapeDtypeStruct((S, N), BF16)]
    out_specs = [pl.BlockSpec((tm, tn), lambda m, n, var: (m, n))]
    args = [x, gains, w]
    if has_aux:
        in_specs.append(pl.BlockSpec((D, LANES), lambda m, n, var: (0, 0)))
        out_shape.append(jax.ShapeDtypeStruct((S, LANES), F32))
        out_specs.append(pl.BlockSpec((tm, LANES), lambda m, n, var: (m, 0)))
        args.append(w_aux)
    if emit_variants:
        out_shape.append(jax.ShapeDtypeStruct((len(emit_variants), S, D), BF16))
        out_specs.append(pl.BlockSpec((len(emit_variants), tm, D), lambda m, n, var: (0, m, 0)))
    return pl.pallas_call(
        functools.partial(_norm_proj_kernel, mm_variants=tuple(mm_variants), emit_variants=tuple(emit_variants),
                          has_aux=has_aux),
        grid_spec=pltpu.PrefetchScalarGridSpec(
            num_scalar_prefetch=1, grid=grid, in_specs=in_specs, out_specs=out_specs,
            scratch_shapes=[pltpu.VMEM((len(mm_variants), tm, D), BF16)]),
        out_shape=out_shape,
        compiler_params=pltpu.CompilerParams(
            dimension_semantics=("arbitrary", "arbitrary"), vmem_limit_bytes=VMEM_LIMIT),
        name="norm_proj_aux" if has_aux else "norm_proj",
    )(var_of_block, *args)


def _split3(a):
    hi = a.astype(BF16)
    r1 = a - hi.astype(F32)
    mid = r1.astype(BF16)
    lo = (r1 - mid.astype(F32)).astype(BF16)
    return hi, mid, lo


def _gla_head_chunk(b, q, k, v, state):
    C, SUB = GLA_CHUNK, GLA_SUB
    dk = q.shape[1]
    b_last = b[C - 1:C, :]
    o = jnp.dot((q * jnp.exp(b)).astype(BF16), state.astype(BF16), preferred_element_type=F32)

    lane = lax.broadcasted_iota(jnp.int32, (SUB, LANES), 1)
    row = lax.broadcasted_iota(jnp.int32, (SUB, LANES), 0)
    lane_c = lax.broadcasted_iota(jnp.int32, (SUB, C), 1)
    ones = jnp.ones((dk, LANES), BF16)
    a_rows = []
    for i in range(C // SUB):
        r0 = i * SUB
        bq, qq, kk = b[r0:r0 + SUB], q[r0:r0 + SUB], k[r0:r0 + SUB]
        prods = []
        for s in range(SUB):
            dec = jnp.exp(jnp.minimum(bq - bq[s:s + 1, :], 0.0))
            prods.append(qq * dec * kk[s:s + 1, :])
        sums = jnp.dot(jnp.concatenate(prods, axis=0).astype(BF16), ones, preferred_element_type=F32)
        a_i = jnp.zeros((SUB, LANES), F32)
        for s in range(SUB):
            a_i = jnp.where(lane == r0 + s, sums[s * SUB:(s + 1) * SUB, :], a_i)
        a_i = jnp.where(lane <= row + r0, a_i, 0.0)[:, :C]
        if i > 0:
            b_ref = bq[0:1, :]
            qt = (qq * jnp.exp(bq - b_ref)).astype(BF16)
            kt = (k * jnp.exp(jnp.minimum(b_ref - b, 0.0))).astype(BF16)
            off = lax.dot_general(qt, kt, _NT, preferred_element_type=F32)
            a_i = jnp.where(lane_c < r0, off, a_i)
        a_rows.append(a_i)
    a = jnp.concatenate(a_rows, axis=0).astype(BF16)
    o = o + jnp.dot(a, v, preferred_element_type=F32)

    k_dec = k * jnp.exp(b_last - b)
    upd = jnp.dot(k_dec.T.astype(BF16), v, preferred_element_type=F32)
    e_col = jnp.broadcast_to(jnp.exp(b_last), (dk, dk)).T
    e_full = jnp.concatenate([e_col] * (v.shape[1] // dk), axis=1)
    return o, state * e_full + upd


def _gla_kernel(q_ref, k_ref, v_ref, z_ref, lr_ref, x_ref, wgu_ref, bg_ref, gout_ref, wout_ref,
                o_ref, state_ref, y_ref, b_ref):
    tm = q_ref.shape[0]
    C = GLA_CHUNK

    @pl.when(pl.program_id(0) == 0)
    def _():
        state_ref[...] = jnp.zeros_like(state_ref)

    pre = jnp.dot(lr_ref[...].astype(BF16), wgu_ref[...], preferred_element_type=F32) + bg_ref[...]
    log_a = (jnp.minimum(pre, 0.0) - jnp.log1p(jnp.exp(-jnp.abs(pre)))) * (1.0 / GLA_GATE_TAU)
    r = lax.broadcasted_iota(jnp.int32, (tm, tm), 0)
    c = lax.broadcasted_iota(jnp.int32, (tm, tm), 1)
    tri = ((c <= r) & ((c // C) == (r // C))).astype(BF16)
    hi, mid, lo = _split3(log_a)
    b_ref[...] = (jnp.dot(tri, hi, preferred_element_type=F32)
                  + jnp.dot(tri, mid, preferred_element_type=F32)
                  + jnp.dot(tri, lo, preferred_element_type=F32))

    gout = gout_ref[...]

    def chunk_body(ci, carry):
        rows = pl.ds(pl.multiple_of(ci * C, C), C)
        for h in range(GLA_HEADS):
            kl = slice(h * GLA_DK_HEAD, (h + 1) * GLA_DK_HEAD)
            vl = slice(h * GLA_DV_HEAD, (h + 1) * GLA_DV_HEAD)
            q = q_ref[rows, kl].astype(F32) * (GLA_DK_HEAD ** -0.5)
            k = k_ref[rows, kl].astype(F32)
            o, new_state = _gla_head_chunk(b_ref[rows, kl], q, k, v_ref[rows, vl], state_ref[h])
            state_ref[h] = new_state
            o = o * lax.rsqrt(jnp.mean(o * o, axis=-1, keepdims=True) + RMS_EPS) * gout
            z = z_ref[rows, vl].astype(F32)
            y_ref[rows, vl] = (o * (z * jax.nn.sigmoid(z))).astype(BF16)
        return carry

    lax.fori_loop(0, tm // C, chunk_body, 0)
    o_ref[...] = x_ref[...] + jnp.dot(y_ref[...], wout_ref[...], preferred_element_type=F32)


def _gla_layer(x, proj, lr, w_gate_up, b_gate, g_out, w_out, *, tm):
    S, D = x.shape
    qk_blk = 2 * D_INNER // GLA_DK
    return pl.pallas_call(
        _gla_kernel,
        grid=(S // tm,),
        in_specs=[
            pl.BlockSpec((tm, GLA_DK), lambda m: (m, qk_blk)),
            pl.BlockSpec((tm, GLA_DK), lambda m: (m, qk_blk + 1)),
            pl.BlockSpec((tm, D_INNER), lambda m: (m, 0)),
            pl.BlockSpec((tm, D_INNER), lambda m: (m, 1)),
            pl.BlockSpec((tm, LANES), lambda m: (m, 0)),
            pl.BlockSpec((tm, D), lambda m: (m, 0)),
            pl.BlockSpec((LANES, GLA_DK), lambda m: (0, 0)),
            pl.BlockSpec((1, GLA_DK), lambda m: (0, 0)),
            pl.BlockSpec((1, GLA_DV_HEAD), lambda m: (0, 0)),
            pl.BlockSpec((D_INNER, D), lambda m: (0, 0)),
        ],
        out_specs=pl.BlockSpec((tm, D), lambda m: (m, 0)),
        out_shape=jax.ShapeDtypeStruct((S, D), F32),
        scratch_shapes=[
            pltpu.VMEM((GLA_HEADS, GLA_DK_HEAD, GLA_DV_HEAD), F32),
            pltpu.VMEM((tm, D_INNER), BF16),
            pltpu.VMEM((tm, GLA_DK), F32),
        ],
        compiler_params=pltpu.CompilerParams(
            dimension_semantics=("arbitrary",), vmem_limit_bytes=VMEM_LIMIT),
        name="gla",
    )(proj, proj, proj, proj, lr, x, w_gate_up, b_gate, g_out, w_out)


def _block_rows(ref):
    return ref[...].reshape(DIL_BLOCK, ref.shape[-1])


def _dil_attn_kernel(hq_ref, hkv_ref, wq_ref, wk_ref, wv_ref, o_ref, m_ref, l_ref,
                     q_s, k_s, v_s, bias_ref, m_acc, l_acc, *, slopes, dilation):
    blk = DIL_BLOCK
    i = pl.program_id(1)
    cur = slice(blk, 2 * blk)

    @pl.when((pl.program_id(0) == 0) & (i == 0))
    def _():
        row = lax.broadcasted_iota(jnp.int32, (blk, blk), 0)
        col = lax.broadcasted_iota(jnp.int32, (blk, blk), 1)
        dist_prev = jnp.where(col >= row, ((row + blk - col) * dilation).astype(F32), MASK_DISTANCE)
        dist_cur = jnp.where(col <= row, ((row - col) * dilation).astype(F32), MASK_DISTANCE)
        for h in range(DIL_HEADS):
            bias_ref[h, :, :blk] = -(slopes[h] * LOG2_E) * dist_prev
            bias_ref[h, :, blk:] = -(slopes[h] * LOG2_E) * dist_cur

    def all_heads(use_prev):
        hq = _block_rows(hq_ref)
        hkv = _block_rows(hkv_ref)
        keys = slice(0, 2 * blk) if use_prev else cur
        heads_per_chunk = PROJ_COLS // DIL_HEAD_DIM
        n_chunks = D_INNER // PROJ_COLS
        projected = [0]

        def project_through(chunk):
            while projected[0] <= min(chunk, n_chunks - 1):
                cols = slice(projected[0] * PROJ_COLS, (projected[0] + 1) * PROJ_COLS)
                q_s[:, cols] = jnp.dot(hq, wq_ref[:, cols], preferred_element_type=F32).astype(BF16)
                k_s[cur, cols] = jnp.dot(hkv, wk_ref[:, cols], preferred_element_type=F32).astype(BF16)
                v_s[cur, cols] = jnp.dot(hkv, wv_ref[:, cols], preferred_element_type=F32).astype(BF16)
                projected[0] += 1

        def scores(h):
            project_through(h // heads_per_chunk + PROJ_LOOKAHEAD)
            hl = slice(h * DIL_HEAD_DIM, (h + 1) * DIL_HEAD_DIM)
            return lax.dot_general(q_s[:, hl], k_s[keys, hl], _NT, preferred_element_type=F32) + bias_ref[h, :, keys]

        ahead = [scores(h) for h in range(SCORE_LOOKAHEAD)]
        for h in range(DIL_HEADS):
            hl = slice(h * DIL_HEAD_DIM, (h + 1) * DIL_HEAD_DIM)
            s = ahead.pop(0)
            if h + SCORE_LOOKAHEAD < DIL_HEADS:
                ahead.append(scores(h + SCORE_LOOKAHEAD))
            m = jnp.max(s, axis=-1, keepdims=True)
            p = jnp.exp2(s - m)
            l = jnp.sum(p, axis=-1, keepdims=True)
            o = jnp.dot(p.astype(BF16), v_s[keys, hl], preferred_element_type=F32).astype(o_ref.dtype)
            if len(o_ref.shape) == 2:
                o_ref[:, hl] = o
            else:
                o_ref[:, :, hl] = o.reshape(o_ref.shape[0], o_ref.shape[1], DIL_HEAD_DIM)
            m_acc[:, h:h + 1] = m
            l_acc[:, h:h + 1] = l

    m_acc[...] = jnp.zeros_like(m_acc)
    l_acc[...] = jnp.zeros_like(l_acc)

    @pl.when(i == 0)
    def _():
        all_heads(False)

    @pl.when(i > 0)
    def _():
        all_heads(True)

    m_ref[...] = m_acc[...].reshape(m_ref.shape)
    l_ref[...] = l_acc[...].reshape(l_ref.shape)
    k_s[:blk, :] = k_s[cur, :]
    v_s[:blk, :] = v_s[cur, :]


def _dil_attn_group(hv, w, g, *, S):
    _, d = DIL_GROUPS[g]
    D = hv.shape[-1]
    nb = S // d // DIL_BLOCK
    n_all = N_GROUPS * DIL_HEADS
    slopes = tuple(2.0 ** (-ALIBI_MAX_EXP * (g * DIL_HEADS + h + 1.0) / n_all) for h in range(DIL_HEADS))
    kcol, vcol, qcol = g, N_GROUPS + g, 2 * N_GROUPS + g
    if d == 1:
        h_spec = lambda v: pl.BlockSpec((None, DIL_BLOCK, D), lambda r, i: (v, i, 0))
        spec = lambda width: pl.BlockSpec((DIL_BLOCK, width), lambda r, i: (i, 0))
        shape = lambda width, dt: jax.ShapeDtypeStruct((S, width), dt)
    else:
        run = PERM_ROWS // d
        pieces = DIL_BLOCK // run
        hv = hv.reshape(hv.shape[0], S // PERM_ROWS, d, run, D)
        h_spec = lambda v: pl.BlockSpec((None, pieces, None, run, D), lambda r, i: (v, i, r, 0, 0))
        spec = lambda width: pl.BlockSpec((pieces, None, run, width), lambda r, i: (i, r, 0, 0))
        shape = lambda width, dt: jax.ShapeDtypeStruct((S // PERM_ROWS, d, run, width), dt)
    w_spec = lambda col: pl.BlockSpec((D, D_INNER), lambda r, i: (0, col))
    o, m, l = pl.pallas_call(
        functools.partial(_dil_attn_kernel, slopes=slopes, dilation=d),
        grid=(d, nb),
        in_specs=[h_spec(N_GROUPS + g), h_spec(g), w_spec(qcol), w_spec(kcol), w_spec(vcol)],
        out_specs=[spec(D_INNER), spec(LANES), spec(LANES)],
        out_shape=[shape(D_INNER, BF16), shape(LANES, F32), shape(LANES, F32)],
        scratch_shapes=[
            pltpu.VMEM((DIL_BLOCK, D_INNER), BF16),
            pltpu.VMEM((2 * DIL_BLOCK, D_INNER), BF16),
            pltpu.VMEM((2 * DIL_BLOCK, D_INNER), BF16),
            pltpu.VMEM((DIL_HEADS, DIL_BLOCK, 2 * DIL_BLOCK), F32),
            pltpu.VMEM((DIL_BLOCK, LANES), F32),
            pltpu.VMEM((DIL_BLOCK, LANES), F32),
        ],
        compiler_params=pltpu.CompilerParams(
            dimension_semantics=("arbitrary", "arbitrary"), vmem_limit_bytes=VMEM_LIMIT),
        name=f"dil_attn_g{g}",
    )(hv, hv, w, w, w)
    return o.reshape(S, D_INNER), m.reshape(S, LANES), l.reshape(S, LANES)


def _combine_kernel(o0_ref, o1_ref, o2_ref, m0_ref, m1_ref, m2_ref, l0_ref, l1_ref, l2_ref, z_ref, x_ref,
                    wout_ref, g_ref, out_ref, y_ref, ot_ref):
    tm = x_ref.shape[0]
    o_refs = (o0_ref, o1_ref, o2_ref)
    m_refs = (m0_ref, m1_ref, m2_ref)
    l_refs = (l0_ref, l1_ref, l2_ref)
    lane = lax.broadcasted_iota(jnp.int32, (PERM_ROWS, LANES), 1)
    unperms = [None] + [_deinterleave_matrix(d, transpose=True) for _, d in DIL_GROUPS[1:]]

    def to_token_order(b):
        rows = slice(b * PERM_ROWS, (b + 1) * PERM_ROWS)
        ms, ls = [m_refs[0][rows, :]], [l_refs[0][rows, :]]
        for g in range(1, N_GROUPS):
            ot_ref[g - 1, rows, :] = jnp.dot(unperms[g], o_refs[g][rows, :], preferred_element_type=F32)
            for stats, ref in ((ms, m_refs[g]), (ls, l_refs[g])):
                stats.append(sum(jnp.dot(unperms[g], piece, preferred_element_type=F32)
                                 for piece in _split3(ref[rows, :])))
        return ms, ls

    def mix_and_project(b, ms, ls):
        rows = slice(b * PERM_ROWS, (b + 1) * PERM_ROWS)
        m = jnp.maximum(jnp.maximum(ms[0], ms[1]), ms[2])
        es = [jnp.exp2(mg - m) for mg in ms]
        den = es[0] * ls[0] + es[1] * ls[1] + es[2] * ls[2]
        ws = [jnp.where(lane < DIL_HEADS, e / den, 0.0) for e in es]
        for h in range(DIL_HEADS):
            hl = slice(h * DIL_HEAD_DIM, (h + 1) * DIL_HEAD_DIM)
            acc = ws[0][:, h:h + 1] * o_refs[0][rows, hl].astype(F32)
            for g in range(1, N_GROUPS):
                acc = acc + ws[g][:, h:h + 1] * ot_ref[g - 1, rows, hl]
            z = z_ref[rows, hl].astype(F32)
            y_ref[rows, hl] = (acc * (z * jax.nn.sigmoid(z))).astype(BF16)
        x = x_ref[rows, :] + jnp.dot(y_ref[rows, :], wout_ref[...], preferred_element_type=F32)
        y = x * lax.rsqrt(jnp.mean(x * x, axis=-1, keepdims=True) + RMS_EPS)
        out_ref[rows, :] = y * g_ref[...]

    n_blocks = tm // PERM_ROWS
    stats = to_token_order(0)
    for b in range(n_blocks):
        nxt = to_token_order(b + 1) if b + 1 < n_blocks else None
        mix_and_project(b, *stats)
        stats = nxt


def _combine(os_, ms, ls, z, x, w_out, g_final, *, tm):
    S, D = x.shape
    row_blk = lambda m: (m, 0)
    return pl.pallas_call(
        _combine_kernel,
        grid=(S // tm,),
        in_specs=[pl.BlockSpec((tm, D_INNER), row_blk)] * 3 + [pl.BlockSpec((tm, LANES), row_blk)] * 6 + [
            pl.BlockSpec((tm, D_INNER), row_blk),
            pl.BlockSpec((tm, D), row_blk),
            pl.BlockSpec((D_INNER, D), lambda m: (0, 0)),
            pl.BlockSpec((1, D), lambda m: (0, 0)),
        ],
        out_specs=pl.BlockSpec((tm, D), row_blk),
        out_shape=jax.ShapeDtypeStruct((S, D), F32),
        scratch_shapes=[pltpu.VMEM((tm, D_INNER), BF16), pltpu.VMEM((N_GROUPS - 1, tm, D_INNER), F32)],
        compiler_params=pltpu.CompilerParams(
            dimension_semantics=("arbitrary",), vmem_limit_bytes=VMEM_LIMIT),
        name="combine",
    )(*os_, *ms, *ls, z, x, w_out, g_final)


def kernel(x, a_norm, a_w_in, a_w_gate_up, a_b_gate, a_g_out, a_w_out, kv_norm, w_kv, b_norm, b_w_in, b_w_out,
           final_norm):
    B, S, D = x.shape
    assert B == 1 and D == D_MODEL and a_norm.shape[0] == 1 and b_norm.shape[0] == 1
    assert S % (DIL_BLOCK * DIL_GROUPS[-1][1]) == 0
    x0 = x.reshape(S, D)

    w_in = a_w_in[0]
    qk, vz, lr0 = w_in[:, :2 * GLA_DK], w_in[:, 2 * GLA_DK:2 * GLA_DK + 2 * D_INNER], 2 * GLA_DK + 2 * D_INNER
    w_a = jnp.concatenate([vz, qk], axis=1).astype(BF16)
    w_lr = jnp.pad(w_in[:, lr0:], ((0, 0), (0, LANES - GLA_GATE_RANK))).astype(BF16)
    tn_a = w_a.shape[1] // 2
    proj_a, lr = _norm_proj(x0, a_norm, [(0, 1)], jnp.zeros((2,), jnp.int32), w_a, w_lr, tm=1024, tn=tn_a)
    w_gu = jnp.pad(a_w_gate_up[0], ((0, LANES - GLA_GATE_RANK), (0, 0))).astype(BF16)
    x1 = _gla_layer(x0, proj_a, lr, w_gu, a_b_gate, a_g_out, a_w_out[0].astype(BF16), tm=256)

    w_b = jnp.concatenate([w_kv, b_w_in[0]], axis=1).astype(BF16)
    q_scale = DIL_HEAD_DIM ** -0.5 * LOG2_E
    gains = jnp.stack([kv_norm, b_norm[0] * q_scale, b_norm[0]])
    dils = [d for _, d in DIL_GROUPS]
    emit = [(0, d) for d in dils] + [(1, d) for d in dils]
    w_z = w_b[:, 3 * N_GROUPS * D_INNER:]
    z, hv = _norm_proj(x1, gains, [(2, 1)], jnp.zeros((2,), jnp.int32), w_z, None, emit, tm=1024, tn=D_INNER // 2)

    outs = [_dil_attn_group(hv, w_b, g, S=S) for g in range(N_GROUPS)]
    out = _combine([o for o, _, _ in outs], [m for _, m, _ in outs], [l for _, _, l in outs], z, x1,
                   b_w_out[0].astype(BF16), final_norm.reshape(1, D), tm=512)
    return out.reshape(B, S, D)
```

```python
import functools

import jax
import jax.numpy as jnp
from jax import lax
from jax.experimental import pallas as pl
from jax.experimental.pallas import tpu as pltpu

F32 = jnp.float32
BF16 = jnp.bfloat16

RMS_EPS = 1e-6
D_MODEL = 1024
D_INNER = 2048
GLA_HEADS = 4
GLA_DK_HEAD = 128
GLA_DV_HEAD = 512
GLA_DK = GLA_HEADS * GLA_DK_HEAD
GLA_GATE_RANK = 16
GLA_GATE_TAU = 16.0
GLA_CHUNK = 64
GLA_SUB = 16
DIL_GROUPS = ((128, 1), (512, 4), (2048, 16))
N_GROUPS = 3
DIL_HEADS = 16
DIL_HEAD_DIM = 128
DIL_BLOCK = 128
ALIBI_MAX_EXP = 8.0
MASK_DISTANCE = 1e34
PERM_ROWS = 256
LOG2_E = 1.4426950408889634
PROJ_COLS = 256
PROJ_LOOKAHEAD = 1
SCORE_LOOKAHEAD = 4
ATTN_BLOCKS_PER_STEP = 4

LANES = 128
VMEM_LIMIT = 56 * 1024 * 1024

_NT = (((1,), (1,)), ((), ()))


def _deinterleave_matrix(d, transpose=False):
    n = PERM_ROWS // d
    p = lax.broadcasted_iota(jnp.int32, (PERM_ROWS, PERM_ROWS), 1 if transpose else 0)
    j = lax.broadcasted_iota(jnp.int32, (PERM_ROWS, PERM_ROWS), 0 if transpose else 1)
    return (j == (p % n) * d + p // n).astype(BF16)


def _norm_proj_kernel(var_ref, x_ref, g_ref, w_ref, *rest, mm_variants, emit_variants, has_aux):
    rest = list(rest)
    waux_ref = rest.pop(0) if has_aux else None
    o_ref = rest.pop(0)
    aux_ref = rest.pop(0) if has_aux else None
    hv_ref = rest.pop(0) if emit_variants else None
    (h_ref,) = rest
    n = pl.program_id(1)
    tm = x_ref.shape[0]

    @pl.when(n == 0)
    def _():
        x = x_ref[...]
        y = x * lax.rsqrt(jnp.mean(x * x, axis=-1, keepdims=True) + RMS_EPS)
        targets = ([(h_ref, i, v) for i, v in enumerate(mm_variants)]
                   + [(hv_ref, i, v) for i, v in enumerate(emit_variants)])
        for gi in sorted({g for _, _, (g, _) in targets}):
            h = (y * g_ref[gi:gi + 1, :]).astype(BF16)
            for ref, i, (g, d) in targets:
                if g != gi:
                    continue
                if d == 1:
                    ref[i] = h
                else:
                    perm = _deinterleave_matrix(d)
                    for b in range(tm // PERM_ROWS):
                        rows = slice(b * PERM_ROWS, (b + 1) * PERM_ROWS)
                        ref[i, rows, :] = jnp.dot(perm, h[rows], preferred_element_type=F32).astype(BF16)
        if has_aux:
            aux_ref[...] = jnp.dot(h_ref[0], waux_ref[...], preferred_element_type=F32)

    h = h_ref[var_ref[n]]
    o_ref[...] = jnp.dot(h, w_ref[...], preferred_element_type=F32).astype(o_ref.dtype)


def _norm_proj(x, gains, mm_variants, var_of_block, w, w_aux, emit_variants=(), *, tm, tn):
    S, D = x.shape
    N = w.shape[1]
    n_gains = gains.shape[0]
    has_aux = w_aux is not None
    grid = (S // tm, N // tn)
    in_specs = [
        pl.BlockSpec((tm, D), lambda m, n, var: (m, 0)),
        pl.BlockSpec((n_gains, D), lambda m, n, var: (0, 0)),
        pl.BlockSpec((D, tn), lambda m, n, var: (0, n)),
    ]
    out_shape = [jax.ShapeDtypeStruct((S, N), BF16)]
    out_specs = [pl.BlockSpec((tm, tn), lambda m, n, var: (m, n))]
    args = [x, gains, w]
    if has_aux:
        in_specs.append(pl.BlockSpec((D, LANES), lambda m, n, var: (0, 0)))
        out_shape.append(jax.ShapeDtypeStruct((S, LANES), F32))
        out_specs.append(pl.BlockSpec((tm, LANES), lambda m, n, var: (m, 0)))
        args.append(w_aux)
    if emit_variants:
        out_shape.append(jax.ShapeDtypeStruct((len(emit_variants), S, D), BF16))
        out_specs.append(pl.BlockSpec((len(emit_variants), tm, D), lambda m, n, var: (0, m, 0)))
    return pl.pallas_call(
        functools.partial(_norm_proj_kernel, mm_variants=tuple(mm_variants), emit_variants=tuple(emit_variants),
                          has_aux=has_aux),
        grid_spec=pltpu.PrefetchScalarGridSpec(
            num_scalar_prefetch=1, grid=grid, in_specs=in_specs, out_specs=out_specs,
            scratch_shapes=[pltpu.VMEM((len(mm_variants), tm, D), BF16)]),
        out_shape=out_shape,
        compiler_params=pltpu.CompilerParams(
            dimension_semantics=("arbitrary", "arbitrary"), vmem_limit_bytes=VMEM_LIMIT),
        name="norm_proj_aux" if has_aux else "norm_proj",
    )(var_of_block, *args)


def _split3(a):
    hi = a.astype(BF16)
    r1 = a - hi.astype(F32)
    mid = r1.astype(BF16)
    lo = (r1 - mid.astype(F32)).astype(BF16)
    return hi, mid, lo


def _gla_head_chunk(b, q, k, v, state):
    C, SUB = GLA_CHUNK, GLA_SUB
    dk = q.shape[1]
    b_last = b[C - 1:C, :]
    o = jnp.dot((q * jnp.exp(b)).astype(BF16), state.astype(BF16), preferred_element_type=F32)

    lane = lax.broadcasted_iota(jnp.int32, (SUB, LANES), 1)
    row = lax.broadcasted_iota(jnp.int32, (SUB, LANES), 0)
    lane_c = lax.broadcasted_iota(jnp.int32, (SUB, C), 1)
    ones = jnp.ones((dk, LANES), BF16)
    a_rows = []
    for i in range(C // SUB):
        r0 = i * SUB
        bq, qq, kk = b[r0:r0 + SUB], q[r0:r0 + SUB], k[r0:r0 + SUB]
        prods = []
        for s in range(SUB):
            dec = jnp.exp(jnp.minimum(bq - bq[s:s + 1, :], 0.0))
            prods.append(qq * dec * kk[s:s + 1, :])
        sums = jnp.dot(jnp.concatenate(prods, axis=0).astype(BF16), ones, preferred_element_type=F32)
        a_i = jnp.zeros((SUB, LANES), F32)
        for s in range(SUB):
            a_i = jnp.where(lane == r0 + s, sums[s * SUB:(s + 1) * SUB, :], a_i)
        a_i = jnp.where(lane <= row + r0, a_i, 0.0)[:, :C]
        if i > 0:
            b_ref = bq[0:1, :]
            qt = (qq * jnp.exp(bq - b_ref)).astype(BF16)
            kt = (k * jnp.exp(jnp.minimum(b_ref - b, 0.0))).astype(BF16)
            off = lax.dot_general(qt, kt, _NT, preferred_element_type=F32)
            a_i = jnp.where(lane_c < r0, off, a_i)
        a_rows.append(a_i)
    a = jnp.concatenate(a_rows, axis=0).astype(BF16)
    o = o + jnp.dot(a, v, preferred_element_type=F32)

    k_dec = k * jnp.exp(b_last - b)
    upd = jnp.dot(k_dec.T.astype(BF16), v, preferred_element_type=F32)
    e_col = jnp.broadcast_to(jnp.exp(b_last), (dk, dk)).T
    e_full = jnp.concatenate([e_col] * (v.shape[1] // dk), axis=1)
    return o, state * e_full + upd


def _gla_kernel(q_ref, k_ref, v_ref, z_ref, lr_ref, x_ref, wgu_ref, bg_ref, gout_ref, wout_ref,
                o_ref, state_ref, y_ref, b_ref):
    tm = q_ref.shape[0]
    C = GLA_CHUNK

    @pl.when(pl.program_id(0) == 0)
    def _():
        state_ref[...] = jnp.zeros_like(state_ref)

    pre = jnp.dot(lr_ref[...].astype(BF16), wgu_ref[...], preferred_element_type=F32) + bg_ref[...]
    log_a = (jnp.minimum(pre, 0.0) - jnp.log1p(jnp.exp(-jnp.abs(pre)))) * (1.0 / GLA_GATE_TAU)
    r = lax.broadcasted_iota(jnp.int32, (tm, tm), 0)
    c = lax.broadcasted_iota(jnp.int32, (tm, tm), 1)
    tri = ((c <= r) & ((c // C) == (r // C))).astype(BF16)
    hi, mid, lo = _split3(log_a)
    b_ref[...] = (jnp.dot(tri, hi, preferred_element_type=F32)
                  + jnp.dot(tri, mid, preferred_element_type=F32)
                  + jnp.dot(tri, lo, preferred_element_type=F32))

    gout = gout_ref[...]

    def chunk_body(ci, carry):
        rows = pl.ds(pl.multiple_of(ci * C, C), C)
        for h in range(GLA_HEADS):
            kl = slice(h * GLA_DK_HEAD, (h + 1) * GLA_DK_HEAD)
            vl = slice(h * GLA_DV_HEAD, (h + 1) * GLA_DV_HEAD)
            q = q_ref[rows, kl].astype(F32) * (GLA_DK_HEAD ** -0.5)
            k = k_ref[rows, kl].astype(F32)
            o, new_state = _gla_head_chunk(b_ref[rows, kl], q, k, v_ref[rows, vl], state_ref[h])
            state_ref[h] = new_state
            o = o * lax.rsqrt(jnp.mean(o * o, axis=-1, keepdims=True) + RMS_EPS) * gout
            z = z_ref[rows, vl].astype(F32)
            y_ref[rows, vl] = (o * (z * jax.nn.sigmoid(z))).astype(BF16)
        return carry

    lax.fori_loop(0, tm // C, chunk_body, 0)
    o_ref[...] = x_ref[...] + jnp.dot(y_ref[...], wout_ref[...], preferred_element_type=F32)


def _gla_layer(x, proj, lr, w_gate_up, b_gate, g_out, w_out, *, tm):
    S, D = x.shape
    qk_blk = 2 * D_INNER // GLA_DK
    return pl.pallas_call(
        _gla_kernel,
        grid=(S // tm,),
        in_specs=[
            pl.BlockSpec((tm, GLA_DK), lambda m: (m, qk_blk)),
            pl.BlockSpec((tm, GLA_DK), lambda m: (m, qk_blk + 1)),
            pl.BlockSpec((tm, D_INNER), lambda m: (m, 0)),
            pl.BlockSpec((tm, D_INNER), lambda m: (m, 1)),
            pl.BlockSpec((tm, LANES), lambda m: (m, 0)),
            pl.BlockSpec((tm, D), lambda m: (m, 0)),
            pl.BlockSpec((LANES, GLA_DK), lambda m: (0, 0)),
            pl.BlockSpec((1, GLA_DK), lambda m: (0, 0)),
            pl.BlockSpec((1, GLA_DV_HEAD), lambda m: (0, 0)),
            pl.BlockSpec((D_INNER, D), lambda m: (0, 0)),
        ],
        out_specs=pl.BlockSpec((tm, D), lambda m: (m, 0)),
        out_shape=jax.ShapeDtypeStruct((S, D), F32),
        scratch_shapes=[
            pltpu.VMEM((GLA_HEADS, GLA_DK_HEAD, GLA_DV_HEAD), F32),
            pltpu.VMEM((tm, D_INNER), BF16),
            pltpu.VMEM((tm, GLA_DK), F32),
        ],
        compiler_params=pltpu.CompilerParams(
            dimension_semantics=("arbitrary",), vmem_limit_bytes=VMEM_LIMIT),
        name="gla",
    )(proj, proj, proj, proj, lr, x, w_gate_up, b_gate, g_out, w_out)


def _block_rows(ref):
    return ref[...].reshape(-1, ref.shape[-1])


def _dil_attn_kernel(hq_ref, hkv_ref, wq_ref, wk_ref, wv_ref, o_ref, m_ref, l_ref,
                     q_s, k_s, v_s, bias_ref, m_acc, l_acc, *, slopes, dilation, qb):
    blk = DIL_BLOCK
    i = pl.program_id(1)
    new_rows = slice(blk, (qb + 1) * blk)

    @pl.when((pl.program_id(0) == 0) & (i == 0))
    def _():
        row = lax.broadcasted_iota(jnp.int32, (blk, blk), 0)
        col = lax.broadcasted_iota(jnp.int32, (blk, blk), 1)
        dist_prev = jnp.where(col >= row, ((row + blk - col) * dilation).astype(F32), MASK_DISTANCE)
        dist_cur = jnp.where(col <= row, ((row - col) * dilation).astype(F32), MASK_DISTANCE)
        for h in range(DIL_HEADS):
            bias_ref[h, :, :blk] = -(slopes[h] * LOG2_E) * dist_prev
            bias_ref[h, :, blk:] = -(slopes[h] * LOG2_E) * dist_cur

    def all_heads(first_step):
        hq = _block_rows(hq_ref)
        hkv = _block_rows(hkv_ref)
        heads_per_chunk = PROJ_COLS // DIL_HEAD_DIM
        n_chunks = D_INNER // PROJ_COLS
        projected = [0]

        def project_through(chunk):
            while projected[0] <= min(chunk, n_chunks - 1):
                cols = slice(projected[0] * PROJ_COLS, (projected[0] + 1) * PROJ_COLS)
                q_s[:, cols] = jnp.dot(hq, wq_ref[:, cols], preferred_element_type=F32).astype(BF16)
                k_s[new_rows, cols] = jnp.dot(hkv, wk_ref[:, cols], preferred_element_type=F32).astype(BF16)
                v_s[new_rows, cols] = jnp.dot(hkv, wv_ref[:, cols], preferred_element_type=F32).astype(BF16)
                projected[0] += 1

        def key_rows(j):
            return slice((j + 1) * blk, (j + 2) * blk) if first_step and j == 0 else slice(j * blk, (j + 2) * blk)

        def scores(unit):
            h, j = divmod(unit, qb)
            project_through(h // heads_per_chunk + PROJ_LOOKAHEAD)
            hl = slice(h * DIL_HEAD_DIM, (h + 1) * DIL_HEAD_DIM)
            keys = key_rows(j)
            bias = bias_ref[h, :, 2 * blk - (keys.stop - keys.start):]
            q = q_s[j * blk:(j + 1) * blk, hl]
            return lax.dot_general(q, k_s[keys, hl], _NT, preferred_element_type=F32) + bias

        n_units = DIL_HEADS * qb
        ahead = [scores(u) for u in range(SCORE_LOOKAHEAD)]
        for unit in range(n_units):
            h, j = divmod(unit, qb)
            hl = slice(h * DIL_HEAD_DIM, (h + 1) * DIL_HEAD_DIM)
            s = ahead.pop(0)
            if unit + SCORE_LOOKAHEAD < n_units:
                ahead.append(scores(unit + SCORE_LOOKAHEAD))
            m = jnp.max(s, axis=-1, keepdims=True)
            p = jnp.exp2(s - m)
            l = jnp.sum(p, axis=-1, keepdims=True)
            o = jnp.dot(p.astype(BF16), v_s[key_rows(j), hl], preferred_element_type=F32).astype(o_ref.dtype)
            if len(o_ref.shape) == 2:
                o_ref[j * blk:(j + 1) * blk, hl] = o
            else:
                pieces = o_ref.shape[0] // qb
                o_ref[j * pieces:(j + 1) * pieces, :, hl] = o.reshape(pieces, o_ref.shape[1], DIL_HEAD_DIM)
            m_acc[j * blk:(j + 1) * blk, h:h + 1] = m
            l_acc[j * blk:(j + 1) * blk, h:h + 1] = l

    m_acc[...] = jnp.zeros_like(m_acc)
    l_acc[...] = jnp.zeros_like(l_acc)

    @pl.when(i == 0)
    def _():
        all_heads(True)

    @pl.when(i > 0)
    def _():
        all_heads(False)

    m_ref[...] = m_acc[...].reshape(m_ref.shape)
    l_ref[...] = l_acc[...].reshape(l_ref.shape)
    k_s[:blk, :] = k_s[qb * blk:, :]
    v_s[:blk, :] = v_s[qb * blk:, :]


def _dil_attn_group(hv, w, g, *, S, qb):
    _, d = DIL_GROUPS[g]
    D = hv.shape[-1]
    rows = qb * DIL_BLOCK
    n_steps = S // d // rows
    n_all = N_GROUPS * DIL_HEADS
    slopes = tuple(2.0 ** (-ALIBI_MAX_EXP * (g * DIL_HEADS + h + 1.0) / n_all) for h in range(DIL_HEADS))
    kcol, vcol, qcol = g, N_GROUPS + g, 2 * N_GROUPS + g
    if d == 1:
        h_spec = lambda v: pl.BlockSpec((None, rows, D), lambda r, i: (v, i, 0))
        spec = lambda width: pl.BlockSpec((rows, width), lambda r, i: (i, 0))
        shape = lambda width, dt: jax.ShapeDtypeStruct((S, width), dt)
    else:
        run = PERM_ROWS // d
        pieces = rows // run
        hv = hv.reshape(hv.shape[0], S // PERM_ROWS, d, run, D)
        h_spec = lambda v: pl.BlockSpec((None, pieces, None, run, D), lambda r, i: (v, i, r, 0, 0))
        spec = lambda width: pl.BlockSpec((pieces, None, run, width), lambda r, i: (i, r, 0, 0))
        shape = lambda width, dt: jax.ShapeDtypeStruct((S // PERM_ROWS, d, run, width), dt)
    w_spec = lambda col: pl.BlockSpec((D, D_INNER), lambda r, i: (0, col))
    o, m, l = pl.pallas_call(
        functools.partial(_dil_attn_kernel, slopes=slopes, dilation=d, qb=qb),
        grid=(d, n_steps),
        in_specs=[h_spec(N_GROUPS + g), h_spec(g), w_spec(qcol), w_spec(kcol), w_spec(vcol)],
        out_specs=[spec(D_INNER), spec(LANES), spec(LANES)],
        out_shape=[shape(D_INNER, BF16), shape(LANES, F32), shape(LANES, F32)],
        scratch_shapes=[
            pltpu.VMEM((rows, D_INNER), BF16),
            pltpu.VMEM((rows + DIL_BLOCK, D_INNER), BF16),
            pltpu.VMEM((rows + DIL_BLOCK, D_INNER), BF16),
            pltpu.VMEM((DIL_HEADS, DIL_BLOCK, 2 * DIL_BLOCK), F32),
            pltpu.VMEM((rows, LANES), F32),
            pltpu.VMEM((rows, LANES), F32),
        ],
        compiler_params=pltpu.CompilerParams(
            dimension_semantics=("arbitrary", "arbitrary"), vmem_limit_bytes=VMEM_LIMIT),
        name=f"dil_attn_g{g}",
    )(hv, hv, w, w, w)
    return o.reshape(S, D_INNER), m.reshape(S, LANES), l.reshape(S, LANES)


def _combine_kernel(o0_ref, o1_ref, o2_ref, m0_ref, m1_ref, m2_ref, l0_ref, l1_ref, l2_ref, z_ref, x_ref,
                    wout_ref, g_ref, out_ref, y_ref, ot_ref):
    tm = x_ref.shape[0]
    o_refs = (o0_ref, o1_ref, o2_ref)
    m_refs = (m0_ref, m1_ref, m2_ref)
    l_refs = (l0_ref, l1_ref, l2_ref)
    lane = lax.broadcasted_iota(jnp.int32, (PERM_ROWS, LANES), 1)
    unperms = [None] + [_deinterleave_matrix(d, transpose=True) for _, d in DIL_GROUPS[1:]]

    def to_token_order(b):
        rows = slice(b * PERM_ROWS, (b + 1) * PERM_ROWS)
        ms, ls = [m_refs[0][rows, :]], [l_refs[0][rows, :]]
        for g in range(1, N_GROUPS):
            ot_ref[g - 1, rows, :] = jnp.dot(unperms[g], o_refs[g][rows, :], preferred_element_type=F32)
            for stats, ref in ((ms, m_refs[g]), (ls, l_refs[g])):
                stats.append(sum(jnp.dot(unperms[g], piece, preferred_element_type=F32)
                                 for piece in _split3(ref[rows, :])))
        return ms, ls

    def mix_and_project(b, ms, ls):
        rows = slice(b * PERM_ROWS, (b + 1) * PERM_ROWS)
        m = jnp.maximum(jnp.maximum(ms[0], ms[1]), ms[2])
        es = [jnp.exp2(mg - m) for mg in ms]
        den = es[0] * ls[0] + es[1] * ls[1] + es[2] * ls[2]
        ws = [jnp.where(lane < DIL_HEADS, e / den, 0.0) for e in es]
        for h in range(DIL_HEADS):
            hl = slice(h * DIL_HEAD_DIM, (h + 1) * DIL_HEAD_DIM)
            acc = ws[0][:, h:h + 1] * o_refs[0][rows, hl].astype(F32)
            for g in range(1, N_GROUPS):
                acc = acc + ws[g][:, h:h + 1] * ot_ref[g - 1, rows, hl]
            z = z_ref[rows, hl].astype(F32)
            y_ref[rows, hl] = (acc * (z * jax.nn.sigmoid(z))).astype(BF16)
        x = x_ref[rows, :] + jnp.dot(y_ref[rows, :], wout_ref[...], preferred_element_type=F32)
        y = x * lax.rsqrt(jnp.mean(x * x, axis=-1, keepdims=True) + RMS_EPS)
        out_ref[rows, :] = y * g_ref[...]

    n_blocks = tm // PERM_ROWS
    stats = to_token_order(0)
    for b in range(n_blocks):
        nxt = to_token_order(b + 1) if b + 1 < n_blocks else None
        mix_and_project(b, *stats)
        stats = nxt


def _combine(os_, ms, ls, z, x, w_out, g_final, *, tm):
    S, D = x.shape
    row_blk = lambda m: (m, 0)
    return pl.pallas_call(
        _combine_kernel,
        grid=(S // tm,),
        in_specs=[pl.BlockSpec((tm, D_INNER), row_blk)] * 3 + [pl.BlockSpec((tm, LANES), row_blk)] * 6 + [
            pl.BlockSpec((tm, D_INNER), row_blk),
            pl.BlockSpec((tm, D), row_blk),
            pl.BlockSpec((D_INNER, D), lambda m: (0, 0)),
            pl.BlockSpec((1, D), lambda m: (0, 0)),
        ],
        out_specs=pl.BlockSpec((tm, D), row_blk),
        out_shape=jax.ShapeDtypeStruct((S, D), F32),
        scratch_shapes=[pltpu.VMEM((tm, D_INNER), BF16), pltpu.VMEM((N_GROUPS - 1, tm, D_INNER), F32)],
        compiler_params=pltpu.CompilerParams(
            dimension_semantics=("arbitrary",), vmem_limit_bytes=VMEM_LIMIT),
        name="combine",
    )(*os_, *ms, *ls, z, x, w_out, g_final)


def kernel(x, a_norm, a_w_in, a_w_gate_up, a_b_gate, a_g_out, a_w_out, kv_norm, w_kv, b_norm, b_w_in, b_w_out,
           final_norm):
    B, S, D = x.shape
    assert B == 1 and D == D_MODEL and a_norm.shape[0] == 1 and b_norm.shape[0] == 1
    assert S % (DIL_BLOCK * DIL_GROUPS[-1][1]) == 0
    x0 = x.reshape(S, D)

    w_in = a_w_in[0]
    qk, vz, lr0 = w_in[:, :2 * GLA_DK], w_in[:, 2 * GLA_DK:2 * GLA_DK + 2 * D_INNER], 2 * GLA_DK + 2 * D_INNER
    w_a = jnp.concatenate([vz, qk], axis=1).astype(BF16)
    w_lr = jnp.pad(w_in[:, lr0:], ((0, 0), (0, LANES - GLA_GATE_RANK))).astype(BF16)
    tn_a = w_a.shape[1] // 2
    proj_a, lr = _norm_proj(x0, a_norm, [(0, 1)], jnp.zeros((2,), jnp.int32), w_a, w_lr, tm=1024, tn=tn_a)
    w_gu = jnp.pad(a_w_gate_up[0], ((0, LANES - GLA_GATE_RANK), (0, 0))).astype(BF16)
    x1 = _gla_layer(x0, proj_a, lr, w_gu, a_b_gate, a_g_out, a_w_out[0].astype(BF16), tm=256)

    w_b = jnp.concatenate([w_kv, b_w_in[0]], axis=1).astype(BF16)
    q_scale = DIL_HEAD_DIM ** -0.5 * LOG2_E
    gains = jnp.stack([kv_norm, b_norm[0] * q_scale, b_norm[0]])
    dils = [d for _, d in DIL_GROUPS]
    emit = [(0, d) for d in dils] + [(1, d) for d in dils]
    w_z = w_b[:, 3 * N_GROUPS * D_INNER:]
    z, hv = _norm_proj(x1, gains, [(2, 1)], jnp.zeros((2,), jnp.int32), w_z, None, emit, tm=1024, tn=D_INNER // 2)

    outs = [_dil_attn_group(hv, w_b, g, S=S, qb=ATTN_BLOCKS_PER_STEP) for g in range(N_GROUPS)]
    out = _combine([o for o, _, _ in outs], [m for _, m, _ in outs], [l for _, _, l in outs], z, x1,
                   b_w_out[0].astype(BF16), final_norm.reshape(1, D), tm=512)
    return out.reshape(B, S, D)
```

```python
import functools

import jax
import jax.numpy as jnp
from jax import lax
from jax.experimental import pallas as pl
from jax.experimental.pallas import tpu as pltpu

F32 = jnp.float32
BF16 = jnp.bfloat16

RMS_EPS = 1e-6
D_MODEL = 1024
D_INNER = 2048
GLA_HEADS = 4
GLA_DK_HEAD = 128
GLA_DV_HEAD = 512
GLA_DK = GLA_HEADS * GLA_DK_HEAD
GLA_GATE_RANK = 16
GLA_GATE_TAU = 16.0
GLA_CHUNK = 64
GLA_SUB = 16
GLA_CUMSUM_ROWS = 256
DIL_GROUPS = ((128, 1), (512, 4), (2048, 16))
N_GROUPS = 3
DIL_HEADS = 16
DIL_HEAD_DIM = 128
DIL_BLOCK = 128
ALIBI_MAX_EXP = 8.0
MASK_DISTANCE = 1e34
PERM_ROWS = 256
LOG2_E = 1.4426950408889634
PROJ_COLS = 256
PROJ_LOOKAHEAD = 1
SCORE_LOOKAHEAD = 4
ATTN_BLOCKS_PER_STEP = 4

LANES = 128
VMEM_LIMIT = 56 * 1024 * 1024

_NT = (((1,), (1,)), ((), ()))


def _deinterleave_matrix(d, transpose=False):
    n = PERM_ROWS // d
    p = lax.broadcasted_iota(jnp.int32, (PERM_ROWS, PERM_ROWS), 1 if transpose else 0)
    j = lax.broadcasted_iota(jnp.int32, (PERM_ROWS, PERM_ROWS), 0 if transpose else 1)
    return (j == (p % n) * d + p // n).astype(BF16)


def _norm_proj_kernel(var_ref, x_ref, g_ref, w_ref, *rest, mm_variants, emit_variants, has_aux):
    rest = list(rest)
    waux_ref = rest.pop(0) if has_aux else None
    o_ref = rest.pop(0)
    aux_ref = rest.pop(0) if has_aux else None
    hv_ref = rest.pop(0) if emit_variants else None
    (h_ref,) = rest
    n = pl.program_id(1)
    tm = x_ref.shape[0]

    @pl.when(n == 0)
    def _():
        x = x_ref[...]
        y = x * lax.rsqrt(jnp.mean(x * x, axis=-1, keepdims=True) + RMS_EPS)
        targets = ([(h_ref, i, v) for i, v in enumerate(mm_variants)]
                   + [(hv_ref, i, v) for i, v in enumerate(emit_variants)])
        for gi in sorted({g for _, _, (g, _) in targets}):
            h = (y * g_ref[gi:gi + 1, :]).astype(BF16)
            for ref, i, (g, d) in targets:
                if g != gi:
                    continue
                if d == 1:
                    ref[i] = h
                else:
                    perm = _deinterleave_matrix(d)
                    for b in range(tm // PERM_ROWS):
                        rows = slice(b * PERM_ROWS, (b + 1) * PERM_ROWS)
                        ref[i, rows, :] = jnp.dot(perm, h[rows], preferred_element_type=F32).astype(BF16)
        if has_aux:
            aux_ref[...] = jnp.dot(h_ref[0], waux_ref[...], preferred_element_type=F32)

    h = h_ref[var_ref[n]]
    o_ref[...] = jnp.dot(h, w_ref[...], preferred_element_type=F32).astype(o_ref.dtype)


def _norm_proj(x, gains, mm_variants, var_of_block, w, w_aux, emit_variants=(), *, tm, tn):
    S, D = x.shape
    N = w.shape[1]
    n_gains = gains.shape[0]
    has_aux = w_aux is not None
    grid = (S // tm, N // tn)
    in_specs = [
        pl.BlockSpec((tm, D), lambda m, n, var: (m, 0)),
        pl.BlockSpec((n_gains, D), lambda m, n, var: (0, 0)),
        pl.BlockSpec((D, tn), lambda m, n, var: (0, n)),
    ]
    out_shape = [jax.ShapeDtypeStruct((S, N), BF16)]
    out_specs = [pl.BlockSpec((tm, tn), lambda m, n, var: (m, n))]
    args = [x, gains, w]
    if has_aux:
        in_specs.append(pl.BlockSpec((D, LANES), lambda m, n, var: (0, 0)))
        out_shape.append(jax.ShapeDtypeStruct((S, LANES), F32))
        out_specs.append(pl.BlockSpec((tm, LANES), lambda m, n, var: (m, 0)))
        args.append(w_aux)
    if emit_variants:
        out_shape.append(jax.ShapeDtypeStruct((len(emit_variants), S, D), BF16))
        out_specs.append(pl.BlockSpec((len(emit_variants), tm, D), lambda m, n, var: (0, m, 0)))
    return pl.pallas_call(
        functools.partial(_norm_proj_kernel, mm_variants=tuple(mm_variants), emit_variants=tuple(emit_variants),
                          has_aux=has_aux),
        grid_spec=pltpu.PrefetchScalarGridSpec(
            num_scalar_prefetch=1, grid=grid, in_specs=in_specs, out_specs=out_specs,
            scratch_shapes=[pltpu.VMEM((len(mm_variants), tm, D), BF16)]),
        out_shape=out_shape,
        compiler_params=pltpu.CompilerParams(
            dimension_semantics=("arbitrary", "arbitrary"), vmem_limit_bytes=VMEM_LIMIT),
        name="norm_proj_aux" if has_aux else "norm_proj",
    )(var_of_block, *args)


def _split3(a):
    hi = a.astype(BF16)
    r1 = a - hi.astype(F32)
    mid = r1.astype(BF16)
    lo = (r1 - mid.astype(F32)).astype(BF16)
    return hi, mid, lo


def _gla_scores(b_ref, q_ref, k_ref, row0, v, state):
    C, SUB, HALF = GLA_CHUNK, GLA_SUB, GLA_SUB // 2
    rows = pl.ds(row0, C)
    b, q, k = b_ref[rows, :], q_ref[rows, :], k_ref[rows, :]
    dk = q.shape[1]
    b_last = b_ref[pl.ds(row0 + C - 1, 1), :]
    o_inter = jnp.dot((q * jnp.exp2(b)).astype(BF16), state.astype(BF16), preferred_element_type=F32)
    k_dec = k * jnp.exp2(b_last - b)
    upd = jnp.dot(k_dec.T.astype(BF16), v, preferred_element_type=F32)
    e_col = jnp.broadcast_to(jnp.exp2(b_last), (dk, dk)).T
    new_state = state * jnp.concatenate([e_col] * (v.shape[1] // dk), axis=1) + upd

    ones = jnp.ones((dk, LANES), BF16)
    diag_sums, offs = [], []
    for i in range(C // SUB):
        r0 = i * SUB
        halves = [(b[r0 + u * HALF:r0 + (u + 1) * HALF], q[r0 + u * HALF:r0 + (u + 1) * HALF]) for u in range(2)]
        prods = []
        for s in range(SUB):
            bs = b_ref[pl.ds(row0 + r0 + s, 1), :]
            ks = k_ref[pl.ds(row0 + r0 + s, 1), :]
            for u in range(s // HALF, 2):
                prods.append(halves[u][1] * jnp.exp2(halves[u][0] - bs) * ks)
        diag_sums.append(jnp.dot(jnp.concatenate(prods, axis=0).astype(BF16), ones, preferred_element_type=F32))
        if i > 0:
            b0 = b_ref[pl.ds(row0 + r0, 1), :]
            qt = (q[r0:r0 + SUB] * jnp.exp2(b[r0:r0 + SUB] - b0)).astype(BF16)
            kt = jnp.concatenate([(k[:r0] * jnp.exp2(b0 - b[:r0])).astype(BF16), jnp.zeros((C - r0, dk), BF16)], axis=0)
            offs.append(lax.dot_general(qt, kt, _NT, preferred_element_type=F32))
    return o_inter, new_state, diag_sums, offs


def _gla_output(o_inter, diag_sums, offs, v):
    C, SUB, HALF = GLA_CHUNK, GLA_SUB, GLA_SUB // 2
    lane = lax.broadcasted_iota(jnp.int32, (HALF, LANES), 1)
    row = lax.broadcasted_iota(jnp.int32, (HALF, LANES), 0)
    a_rows = []
    for i in range(C // SUB):
        r0 = i * SUB
        acc = [jnp.zeros((HALF, LANES), F32) for _ in range(2)]
        piece = 0
        for s in range(SUB):
            for u in range(s // HALF, 2):
                acc[u] = jnp.where(lane == r0 + s, diag_sums[i][piece * HALF:(piece + 1) * HALF, :], acc[u])
                piece += 1
        a_i = jnp.concatenate([jnp.where(lane <= row + r0 + u * HALF, acc[u], 0.0) for u in range(2)], axis=0)[:, :C]
        a_rows.append(a_i + offs[i - 1] if i > 0 else a_i)
    a = jnp.concatenate(a_rows, axis=0).astype(BF16)
    return o_inter + jnp.dot(a, v, preferred_element_type=F32)


def _gla_kernel(q_ref, k_ref, v_ref, z_ref, lr_ref, x_ref, wgu_ref, bg_ref, gout_ref, wout_ref,
                o_ref, state_ref, y_ref, b_ref, q32_ref, k32_ref):
    tm = q_ref.shape[0]
    C = GLA_CHUNK

    @pl.when(pl.program_id(0) == 0)
    def _():
        state_ref[...] = jnp.zeros_like(state_ref)

    pre = jnp.dot(lr_ref[...].astype(BF16), wgu_ref[...], preferred_element_type=F32) + bg_ref[...]
    log2_a = (jnp.minimum(pre, 0.0) - jnp.log1p(jnp.exp(-jnp.abs(pre)))) * (LOG2_E / GLA_GATE_TAU)
    tb = min(tm, GLA_CUMSUM_ROWS)
    r = lax.broadcasted_iota(jnp.int32, (tb, tb), 0)
    c = lax.broadcasted_iota(jnp.int32, (tb, tb), 1)
    tri = ((c <= r) & ((c // C) == (r // C))).astype(BF16)
    b_all = jnp.concatenate(
        [sum(jnp.dot(tri, piece, preferred_element_type=F32) for piece in _split3(log2_a[t * tb:(t + 1) * tb]))
         for t in range(tm // tb)], axis=0)
    for h in range(GLA_HEADS):
        kl = slice(h * GLA_DK_HEAD, (h + 1) * GLA_DK_HEAD)
        b_ref[h] = b_all[:, kl]
        q32_ref[h] = q_ref[:, kl].astype(F32) * (GLA_DK_HEAD ** -0.5)
        k32_ref[h] = k_ref[:, kl].astype(F32)

    gout = gout_ref[...]

    def chunk_body(ci, carry):
        row0 = pl.multiple_of(ci * C, C)
        rows = pl.ds(row0, C)
        vls = [slice(h * GLA_DV_HEAD, (h + 1) * GLA_DV_HEAD) for h in range(GLA_HEADS)]
        parts = []
        for h in range(GLA_HEADS):
            o_inter, new_state, diag_sums, offs = _gla_scores(b_ref.at[h], q32_ref.at[h], k32_ref.at[h], row0,
                                                              v_ref[rows, vls[h]], state_ref[h])
            state_ref[h] = new_state
            parts.append((o_inter, diag_sums, offs))
        for h in range(GLA_HEADS):
            o = _gla_output(*parts[h], v_ref[rows, vls[h]])
            o = o * lax.rsqrt(jnp.mean(o * o, axis=-1, keepdims=True) + RMS_EPS) * gout
            z = z_ref[rows, vls[h]].astype(F32)
            y_ref[rows, vls[h]] = (o * (z * jax.nn.sigmoid(z))).astype(BF16)
        return carry

    lax.fori_loop(0, tm // C, chunk_body, 0)
    o_ref[...] = x_ref[...] + jnp.dot(y_ref[...], wout_ref[...], preferred_element_type=F32)


def _gla_layer(x, proj, lr, w_gate_up, b_gate, g_out, w_out, *, tm):
    S, D = x.shape
    qk_blk = 2 * D_INNER // GLA_DK
    return pl.pallas_call(
        _gla_kernel,
        grid=(S // tm,),
        in_specs=[
            pl.BlockSpec((tm, GLA_DK), lambda m: (m, qk_blk)),
            pl.BlockSpec((tm, GLA_DK), lambda m: (m, qk_blk + 1)),
            pl.BlockSpec((tm, D_INNER), lambda m: (m, 0)),
            pl.BlockSpec((tm, D_INNER), lambda m: (m, 1)),
            pl.BlockSpec((tm, LANES), lambda m: (m, 0)),
            pl.BlockSpec((tm, D), lambda m: (m, 0)),
            pl.BlockSpec((LANES, GLA_DK), lambda m: (0, 0)),
            pl.BlockSpec((1, GLA_DK), lambda m: (0, 0)),
            pl.BlockSpec((1, GLA_DV_HEAD), lambda m: (0, 0)),
            pl.BlockSpec((D_INNER, D), lambda m: (0, 0)),
        ],
        out_specs=pl.BlockSpec((tm, D), lambda m: (m, 0)),
        out_shape=jax.ShapeDtypeStruct((S, D), F32),
        scratch_shapes=[
            pltpu.VMEM((GLA_HEADS, GLA_DK_HEAD, GLA_DV_HEAD), F32),
            pltpu.VMEM((tm, D_INNER), BF16),
            pltpu.VMEM((GLA_HEADS, tm, GLA_DK_HEAD), F32),
            pltpu.VMEM((GLA_HEADS, tm, GLA_DK_HEAD), F32),
            pltpu.VMEM((GLA_HEADS, tm, GLA_DK_HEAD), F32),
        ],
        compiler_params=pltpu.CompilerParams(
            dimension_semantics=("arbitrary",), vmem_limit_bytes=VMEM_LIMIT),
        name="gla",
    )(proj, proj, proj, proj, lr, x, w_gate_up, b_gate, g_out, w_out)


def _block_rows(ref):
    return ref[...].reshape(-1, ref.shape[-1])


def _dil_attn_kernel(hq_ref, hkv_ref, wq_ref, wk_ref, wv_ref, o_ref, m_ref, l_ref,
                     q_s, k_s, v_s, bias_ref, m_acc, l_acc, *, slopes, dilation, qb):
    blk = DIL_BLOCK
    i = pl.program_id(1)
    new_rows = slice(blk, (qb + 1) * blk)

    @pl.when((pl.program_id(0) == 0) & (i == 0))
    def _():
        row = lax.broadcasted_iota(jnp.int32, (blk, blk), 0)
        col = lax.broadcasted_iota(jnp.int32, (blk, blk), 1)
        dist_prev = jnp.where(col >= row, ((row + blk - col) * dilation).astype(F32), MASK_DISTANCE)
        dist_cur = jnp.where(col <= row, ((row - col) * dilation).astype(F32), MASK_DISTANCE)
        for h in range(DIL_HEADS):
            bias_ref[h, :, :blk] = -(slopes[h] * LOG2_E) * dist_prev
            bias_ref[h, :, blk:] = -(slopes[h] * LOG2_E) * dist_cur

    def all_heads(first_step):
        hq = _block_rows(hq_ref)
        hkv = _block_rows(hkv_ref)
        heads_per_chunk = PROJ_COLS // DIL_HEAD_DIM
        n_chunks = D_INNER // PROJ_COLS
        projected = [0]

        def project_through(chunk):
            while projected[0] <= min(chunk, n_chunks - 1):
                cols = slice(projected[0] * PROJ_COLS, (projected[0] + 1) * PROJ_COLS)
                q_s[:, cols] = jnp.dot(hq, wq_ref[:, cols], preferred_element_type=F32).astype(BF16)
                k_s[new_rows, cols] = jnp.dot(hkv, wk_ref[:, cols], preferred_element_type=F32).astype(BF16)
                v_s[new_rows, cols] = jnp.dot(hkv, wv_ref[:, cols], preferred_element_type=F32).astype(BF16)
                projected[0] += 1

        def key_rows(j):
            return slice((j + 1) * blk, (j + 2) * blk) if first_step and j == 0 else slice(j * blk, (j + 2) * blk)

        def scores(unit):
            h, j = divmod(unit, qb)
            project_through(h // heads_per_chunk + PROJ_LOOKAHEAD)
            hl = slice(h * DIL_HEAD_DIM, (h + 1) * DIL_HEAD_DIM)
            keys = key_rows(j)
            bias = bias_ref[h, :, 2 * blk - (keys.stop - keys.start):]
            q = q_s[j * blk:(j + 1) * blk, hl]
            return lax.dot_general(q, k_s[keys, hl], _NT, preferred_element_type=F32) + bias

        n_units = DIL_HEADS * qb
        ahead = [scores(u) for u in range(SCORE_LOOKAHEAD)]
        for unit in range(n_units):
            h, j = divmod(unit, qb)
            hl = slice(h * DIL_HEAD_DIM, (h + 1) * DIL_HEAD_DIM)
            s = ahead.pop(0)
            if unit + SCORE_LOOKAHEAD < n_units:
                ahead.append(scores(unit + SCORE_LOOKAHEAD))
            m = jnp.max(s, axis=-1, keepdims=True)
            p = jnp.exp2(s - m)
            l = jnp.sum(p, axis=-1, keepdims=True)
            o = jnp.dot(p.astype(BF16), v_s[key_rows(j), hl], preferred_element_type=F32).astype(o_ref.dtype)
            if len(o_ref.shape) == 2:
                o_ref[j * blk:(j + 1) * blk, hl] = o
            else:
                pieces = o_ref.shape[0] // qb
                o_ref[j * pieces:(j + 1) * pieces, :, hl] = o.reshape(pieces, o_ref.shape[1], DIL_HEAD_DIM)
            m_acc[j * blk:(j + 1) * blk, h:h + 1] = m
            l_acc[j * blk:(j + 1) * blk, h:h + 1] = l

    m_acc[...] = jnp.zeros_like(m_acc)
    l_acc[...] = jnp.zeros_like(l_acc)

    @pl.when(i == 0)
    def _():
        all_heads(True)

    @pl.when(i > 0)
    def _():
        all_heads(False)

    m_ref[...] = m_acc[...].reshape(m_ref.shape)
    l_ref[...] = l_acc[...].reshape(l_ref.shape)
    k_s[:blk, :] = k_s[qb * blk:, :]
    v_s[:blk, :] = v_s[qb * blk:, :]


def _dil_attn_group(hv, w, g, *, S, qb):
    _, d = DIL_GROUPS[g]
    D = hv.shape[-1]
    rows = qb * DIL_BLOCK
    n_steps = S // d // rows
    n_all = N_GROUPS * DIL_HEADS
    slopes = tuple(2.0 ** (-ALIBI_MAX_EXP * (g * DIL_HEADS + h + 1.0) / n_all) for h in range(DIL_HEADS))
    kcol, vcol, qcol = g, N_GROUPS + g, 2 * N_GROUPS + g
    if d == 1:
        h_spec = lambda v: pl.BlockSpec((None, rows, D), lambda r, i: (v, i, 0))
        spec = lambda width: pl.BlockSpec((rows, width), lambda r, i: (i, 0))
        shape = lambda width, dt: jax.ShapeDtypeStruct((S, width), dt)
    else:
        run = PERM_ROWS // d
        pieces = rows // run
        hv = hv.reshape(hv.shape[0], S // PERM_ROWS, d, run, D)
        h_spec = lambda v: pl.BlockSpec((None, pieces, None, run, D), lambda r, i: (v, i, r, 0, 0))
        spec = lambda width: pl.BlockSpec((pieces, None, run, width), lambda r, i: (i, r, 0, 0))
        shape = lambda width, dt: jax.ShapeDtypeStruct((S // PERM_ROWS, d, run, width), dt)
    w_spec = lambda col: pl.BlockSpec((D, D_INNER), lambda r, i: (0, col))
    o, m, l = pl.pallas_call(
        functools.partial(_dil_attn_kernel, slopes=slopes, dilation=d, qb=qb),
        grid=(d, n_steps),
        in_specs=[h_spec(N_GROUPS + g), h_spec(g), w_spec(qcol), w_spec(kcol), w_spec(vcol)],
        out_specs=[spec(D_INNER), spec(LANES), spec(LANES)],
        out_shape=[shape(D_INNER, BF16), shape(LANES, F32), shape(LANES, F32)],
        scratch_shapes=[
            pltpu.VMEM((rows, D_INNER), BF16),
            pltpu.VMEM((rows + DIL_BLOCK, D_INNER), BF16),
            pltpu.VMEM((rows + DIL_BLOCK, D_INNER), BF16),
            pltpu.VMEM((DIL_HEADS, DIL_BLOCK, 2 * DIL_BLOCK), F32),
            pltpu.VMEM((rows, LANES), F32),
            pltpu.VMEM((rows, LANES), F32),
        ],
        compiler_params=pltpu.CompilerParams(
            dimension_semantics=("arbitrary", "arbitrary"), vmem_limit_bytes=VMEM_LIMIT),
        name=f"dil_attn_g{g}",
    )(hv, hv, w, w, w)
    return o.reshape(S, D_INNER), m.reshape(S, LANES), l.reshape(S, LANES)


def _combine_kernel(o0_ref, o1_ref, o2_ref, m0_ref, m1_ref, m2_ref, l0_ref, l1_ref, l2_ref, z_ref, x_ref,
                    wout_ref, g_ref, out_ref, y_ref, ot_ref):
    tm = x_ref.shape[0]
    o_refs = (o0_ref, o1_ref, o2_ref)
    m_refs = (m0_ref, m1_ref, m2_ref)
    l_refs = (l0_ref, l1_ref, l2_ref)
    lane = lax.broadcasted_iota(jnp.int32, (PERM_ROWS, LANES), 1)
    unperms = [None] + [_deinterleave_matrix(d, transpose=True) for _, d in DIL_GROUPS[1:]]

    def to_token_order(b):
        rows = slice(b * PERM_ROWS, (b + 1) * PERM_ROWS)
        ms, ls = [m_refs[0][rows, :]], [l_refs[0][rows, :]]
        for g in range(1, N_GROUPS):
            ot_ref[g - 1, rows, :] = jnp.dot(unperms[g], o_refs[g][rows, :], preferred_element_type=F32)
            for stats, ref in ((ms, m_refs[g]), (ls, l_refs[g])):
                stats.append(sum(jnp.dot(unperms[g], piece, preferred_element_type=F32)
                                 for piece in _split3(ref[rows, :])))
        return ms, ls

    def mix_and_project(b, ms, ls):
        rows = slice(b * PERM_ROWS, (b + 1) * PERM_ROWS)
        m = jnp.maximum(jnp.maximum(ms[0], ms[1]), ms[2])
        es = [jnp.exp2(mg - m) for mg in ms]
        den = es[0] * ls[0] + es[1] * ls[1] + es[2] * ls[2]
        ws = [jnp.where(lane < DIL_HEADS, e / den, 0.0) for e in es]
        for h in range(DIL_HEADS):
            hl = slice(h * DIL_HEAD_DIM, (h + 1) * DIL_HEAD_DIM)
            acc = ws[0][:, h:h + 1] * o_refs[0][rows, hl].astype(F32)
            for g in range(1, N_GROUPS):
                acc = acc + ws[g][:, h:h + 1] * ot_ref[g - 1, rows, hl]
            z = z_ref[rows, hl].astype(F32)
            y_ref[rows, hl] = (acc * (z * jax.nn.sigmoid(z))).astype(BF16)
        x = x_ref[rows, :] + jnp.dot(y_ref[rows, :], wout_ref[...], preferred_element_type=F32)
        y = x * lax.rsqrt(jnp.mean(x * x, axis=-1, keepdims=True) + RMS_EPS)
        out_ref[rows, :] = y * g_ref[...]

    n_blocks = tm // PERM_ROWS
    stats = to_token_order(0)
    for b in range(n_blocks):
        nxt = to_token_order(b + 1) if b + 1 < n_blocks else None
        mix_and_project(b, *stats)
        stats = nxt


def _combine(os_, ms, ls, z, x, w_out, g_final, *, tm):
    S, D = x.shape
    row_blk = lambda m: (m, 0)
    return pl.pallas_call(
        _combine_kernel,
        grid=(S // tm,),
        in_specs=[pl.BlockSpec((tm, D_INNER), row_blk)] * 3 + [pl.BlockSpec((tm, LANES), row_blk)] * 6 + [
            pl.BlockSpec((tm, D_INNER), row_blk),
            pl.BlockSpec((tm, D), row_blk),
            pl.BlockSpec((D_INNER, D), lambda m: (0, 0)),
            pl.BlockSpec((1, D), lambda m: (0, 0)),
        ],
        out_specs=pl.BlockSpec((tm, D), row_blk),
        out_shape=jax.ShapeDtypeStruct((S, D), F32),
        scratch_shapes=[pltpu.VMEM((tm, D_INNER), BF16), pltpu.VMEM((N_GROUPS - 1, tm, D_INNER), F32)],
        compiler_params=pltpu.CompilerParams(
            dimension_semantics=("arbitrary",), vmem_limit_bytes=VMEM_LIMIT),
        name="combine",
    )(*os_, *ms, *ls, z, x, w_out, g_final)


def kernel(x, a_norm, a_w_in, a_w_gate_up, a_b_gate, a_g_out, a_w_out, kv_norm, w_kv, b_norm, b_w_in, b_w_out,
           final_norm):
    B, S, D = x.shape
    assert B == 1 and D == D_MODEL and a_norm.shape[0] == 1 and b_norm.shape[0] == 1
    assert S % (ATTN_BLOCKS_PER_STEP * DIL_BLOCK * max(d for _, d in DIL_GROUPS)) == 0
    x0 = x.reshape(S, D)

    w_in = a_w_in[0]
    qk, vz, lr0 = w_in[:, :2 * GLA_DK], w_in[:, 2 * GLA_DK:2 * GLA_DK + 2 * D_INNER], 2 * GLA_DK + 2 * D_INNER
    w_a = jnp.concatenate([vz, qk], axis=1).astype(BF16)
    w_lr = jnp.pad(w_in[:, lr0:], ((0, 0), (0, LANES - GLA_GATE_RANK))).astype(BF16)
    tn_a = w_a.shape[1] // 2
    proj_a, lr = _norm_proj(x0, a_norm, [(0, 1)], jnp.zeros((2,), jnp.int32), w_a, w_lr, tm=1024, tn=tn_a)
    w_gu = jnp.pad(a_w_gate_up[0], ((0, LANES - GLA_GATE_RANK), (0, 0))).astype(BF16)
    x1 = _gla_layer(x0, proj_a, lr, w_gu, a_b_gate, a_g_out, a_w_out[0].astype(BF16), tm=512)

    w_b = jnp.concatenate([w_kv, b_w_in[0]], axis=1).astype(BF16)
    q_scale = DIL_HEAD_DIM ** -0.5 * LOG2_E
    gains = jnp.stack([kv_norm, b_norm[0] * q_scale, b_norm[0]])
    dils = [d for _, d in DIL_GROUPS]
    emit = [(0, d) for d in dils] + [(1, d) for d in dils]
    w_z = w_b[:, 3 * N_GROUPS * D_INNER:]
    z, hv = _norm_proj(x1, gains, [(2, 1)], jnp.zeros((2,), jnp.int32), w_z, None, emit, tm=1024, tn=D_INNER // 2)

    outs = [_dil_attn_group(hv, w_b, g, S=S, qb=ATTN_BLOCKS_PER_STEP) for g in range(N_GROUPS)]
    out = _combine([o for o, _, _ in outs], [m for _, m, _ in outs], [l for _, _, l in outs], z, x1,
                   b_w_out[0].astype(BF16), final_norm.reshape(1, D), tm=512)
    return out.reshape(B, S, D)
```

```python
import functools

import jax
import jax.numpy as jnp
from jax import lax
from jax.experimental import pallas as pl
from jax.experimental.pallas import tpu as pltpu

F32 = jnp.float32
BF16 = jnp.bfloat16

RMS_EPS = 1e-6
D_MODEL = 1024
D_INNER = 2048
GLA_HEADS = 4
GLA_DK_HEAD = 128
GLA_DV_HEAD = 512
GLA_DK = GLA_HEADS * GLA_DK_HEAD
GLA_GATE_RANK = 16
GLA_GATE_TAU = 16.0
GLA_CHUNK = 64
GLA_SUB = 16
GLA_CUMSUM_ROWS = 256
DIL_GROUPS = ((128, 1), (512, 4), (2048, 16))
N_GROUPS = 3
DIL_HEADS = 16
DIL_HEAD_DIM = 128
DIL_BLOCK = 128
ALIBI_MAX_EXP = 8.0
MASK_DISTANCE = 1e34
PERM_ROWS = 256
LOG2_E = 1.4426950408889634
PROJ_COLS = 256
PROJ_LOOKAHEAD = 1
SCORE_LOOKAHEAD = 4
ATTN_BLOCKS_PER_STEP = 4

LANES = 128
VMEM_LIMIT = 56 * 1024 * 1024

_NT = (((1,), (1,)), ((), ()))


def _deinterleave_matrix(d, transpose=False):
    n = PERM_ROWS // d
    p = lax.broadcasted_iota(jnp.int32, (PERM_ROWS, PERM_ROWS), 1 if transpose else 0)
    j = lax.broadcasted_iota(jnp.int32, (PERM_ROWS, PERM_ROWS), 0 if transpose else 1)
    return (j == (p % n) * d + p // n).astype(BF16)


def _norm_proj_kernel(var_ref, x_ref, g_ref, w_ref, *rest, mm_variants, emit_variants, has_aux):
    rest = list(rest)
    waux_ref = rest.pop(0) if has_aux else None
    o_ref = rest.pop(0)
    aux_ref = rest.pop(0) if has_aux else None
    hv_ref = rest.pop(0) if emit_variants else None
    (h_ref,) = rest
    n = pl.program_id(1)
    tm = x_ref.shape[0]

    @pl.when(n == 0)
    def _():
        x = x_ref[...]
        y = x * lax.rsqrt(jnp.mean(x * x, axis=-1, keepdims=True) + RMS_EPS)
        targets = ([(h_ref, i, v) for i, v in enumerate(mm_variants)]
                   + [(hv_ref, i, v) for i, v in enumerate(emit_variants)])
        for gi in sorted({g for _, _, (g, _) in targets}):
            h = (y * g_ref[gi:gi + 1, :]).astype(BF16)
            for ref, i, (g, d) in targets:
                if g != gi:
                    continue
                if d == 1:
                    ref[i] = h
                else:
                    perm = _deinterleave_matrix(d)
                    for b in range(tm // PERM_ROWS):
                        rows = slice(b * PERM_ROWS, (b + 1) * PERM_ROWS)
                        ref[i, rows, :] = jnp.dot(perm, h[rows], preferred_element_type=F32).astype(BF16)
        if has_aux:
            aux_ref[...] = jnp.dot(h_ref[0], waux_ref[...], preferred_element_type=F32)

    h = h_ref[var_ref[n]]
    o_ref[...] = jnp.dot(h, w_ref[...], preferred_element_type=F32).astype(o_ref.dtype)


def _norm_proj(x, gains, mm_variants, var_of_block, w, w_aux, emit_variants=(), *, n_out, w_col0=0, tm, tn):
    S, D = x.shape
    N = n_out
    n_gains = gains.shape[0]
    has_aux = w_aux is not None
    grid = (S // tm, N // tn)
    w_blk0 = w_col0 // tn
    in_specs = [
        pl.BlockSpec((tm, D), lambda m, n, var: (m, 0)),
        pl.BlockSpec((n_gains, D), lambda m, n, var: (0, 0)),
        pl.BlockSpec((D, tn), lambda m, n, var: (0, w_blk0 + n)),
    ]
    out_shape = [jax.ShapeDtypeStruct((S, N), BF16)]
    out_specs = [pl.BlockSpec((tm, tn), lambda m, n, var: (m, n))]
    args = [x, gains, w]
    if has_aux:
        in_specs.append(pl.BlockSpec((D, LANES), lambda m, n, var: (0, 0)))
        out_shape.append(jax.ShapeDtypeStruct((S, LANES), F32))
        out_specs.append(pl.BlockSpec((tm, LANES), lambda m, n, var: (m, 0)))
        args.append(w_aux)
    if emit_variants:
        out_shape.append(jax.ShapeDtypeStruct((len(emit_variants), S, D), BF16))
        out_specs.append(pl.BlockSpec((len(emit_variants), tm, D), lambda m, n, var: (0, m, 0)))
    return pl.pallas_call(
        functools.partial(_norm_proj_kernel, mm_variants=tuple(mm_variants), emit_variants=tuple(emit_variants),
                          has_aux=has_aux),
        grid_spec=pltpu.PrefetchScalarGridSpec(
            num_scalar_prefetch=1, grid=grid, in_specs=in_specs, out_specs=out_specs,
            scratch_shapes=[pltpu.VMEM((len(mm_variants), tm, D), BF16)]),
        out_shape=out_shape,
        compiler_params=pltpu.CompilerParams(
            dimension_semantics=("arbitrary", "arbitrary"), vmem_limit_bytes=VMEM_LIMIT),
        name="norm_proj_aux" if has_aux else "norm_proj",
    )(var_of_block, *args)


def _split3(a):
    hi = a.astype(BF16)
    r1 = a - hi.astype(F32)
    mid = r1.astype(BF16)
    lo = (r1 - mid.astype(F32)).astype(BF16)
    return hi, mid, lo


def _gla_scores(b_ref, q_ref, k_ref, row0, v, state):
    C, SUB, HALF = GLA_CHUNK, GLA_SUB, GLA_SUB // 2
    rows = pl.ds(row0, C)
    b, q, k = b_ref[rows, :], q_ref[rows, :], k_ref[rows, :]
    dk = q.shape[1]
    b_last = b_ref[pl.ds(row0 + C - 1, 1), :]
    o_inter = jnp.dot((q * jnp.exp2(b)).astype(BF16), state.astype(BF16), preferred_element_type=F32)
    k_dec = k * jnp.exp2(b_last - b)
    upd = jnp.dot(k_dec.T.astype(BF16), v, preferred_element_type=F32)
    e_col = jnp.broadcast_to(jnp.exp2(b_last), (dk, dk)).T
    new_state = state * jnp.concatenate([e_col] * (v.shape[1] // dk), axis=1) + upd

    ones = jnp.ones((dk, LANES), BF16)
    diag_sums, offs = [], []
    for i in range(C // SUB):
        r0 = i * SUB
        halves = [(b[r0 + u * HALF:r0 + (u + 1) * HALF], q[r0 + u * HALF:r0 + (u + 1) * HALF]) for u in range(2)]
        prods = []
        for s in range(SUB):
            bs = b_ref[pl.ds(row0 + r0 + s, 1), :]
            ks = k_ref[pl.ds(row0 + r0 + s, 1), :]
            for u in range(s // HALF, 2):
                prods.append(halves[u][1] * jnp.exp2(halves[u][0] - bs) * ks)
        diag_sums.append(jnp.dot(jnp.concatenate(prods, axis=0).astype(BF16), ones, preferred_element_type=F32))
        if i > 0:
            b0 = b_ref[pl.ds(row0 + r0, 1), :]
            qt = (q[r0:r0 + SUB] * jnp.exp2(b[r0:r0 + SUB] - b0)).astype(BF16)
            kt = jnp.concatenate([(k[:r0] * jnp.exp2(b0 - b[:r0])).astype(BF16), jnp.zeros((C - r0, dk), BF16)], axis=0)
            offs.append(lax.dot_general(qt, kt, _NT, preferred_element_type=F32))
    return o_inter, new_state, diag_sums, offs


def _gla_output(o_inter, diag_sums, offs, v):
    C, SUB, HALF = GLA_CHUNK, GLA_SUB, GLA_SUB // 2
    lane = lax.broadcasted_iota(jnp.int32, (HALF, LANES), 1)
    row = lax.broadcasted_iota(jnp.int32, (HALF, LANES), 0)
    a_rows = []
    for i in range(C // SUB):
        r0 = i * SUB
        acc = [jnp.zeros((HALF, LANES), F32) for _ in range(2)]
        piece = 0
        for s in range(SUB):
            for u in range(s // HALF, 2):
                acc[u] = jnp.where(lane == r0 + s, diag_sums[i][piece * HALF:(piece + 1) * HALF, :], acc[u])
                piece += 1
        a_i = jnp.concatenate([jnp.where(lane <= row + r0 + u * HALF, acc[u], 0.0) for u in range(2)], axis=0)[:, :C]
        a_rows.append(a_i + offs[i - 1] if i > 0 else a_i)
    a = jnp.concatenate(a_rows, axis=0).astype(BF16)
    return o_inter + jnp.dot(a, v, preferred_element_type=F32)


def _gla_kernel(q_ref, k_ref, v0_ref, v1_ref, z0_ref, z1_ref, lr_ref, x_ref, wgu_ref, bg_ref, gout_ref, wout_ref,
                o_ref, state_ref, y_ref, b_ref, q32_ref, k32_ref):
    tm = q_ref.shape[0]
    C = GLA_CHUNK
    heads_per_ref = v0_ref.shape[1] // GLA_DV_HEAD

    def head_cols(refs, h):
        u = h % heads_per_ref
        return refs[h // heads_per_ref], slice(u * GLA_DV_HEAD, (u + 1) * GLA_DV_HEAD)

    @pl.when(pl.program_id(0) == 0)
    def _():
        state_ref[...] = jnp.zeros_like(state_ref)

    pre = jnp.dot(lr_ref[...].astype(BF16), wgu_ref[...], preferred_element_type=F32) + bg_ref[...]
    log2_a = (jnp.minimum(pre, 0.0) - jnp.log1p(jnp.exp(-jnp.abs(pre)))) * (LOG2_E / GLA_GATE_TAU)
    tb = min(tm, GLA_CUMSUM_ROWS)
    r = lax.broadcasted_iota(jnp.int32, (tb, tb), 0)
    c = lax.broadcasted_iota(jnp.int32, (tb, tb), 1)
    tri = ((c <= r) & ((c // C) == (r // C))).astype(BF16)
    b_all = jnp.concatenate(
        [sum(jnp.dot(tri, piece, preferred_element_type=F32) for piece in _split3(log2_a[t * tb:(t + 1) * tb]))
         for t in range(tm // tb)], axis=0)
    for h in range(GLA_HEADS):
        kl = slice(h * GLA_DK_HEAD, (h + 1) * GLA_DK_HEAD)
        b_ref[h] = b_all[:, kl]
        q32_ref[h] = q_ref[:, kl].astype(F32) * (GLA_DK_HEAD ** -0.5)
        k32_ref[h] = k_ref[:, kl].astype(F32)

    gout = gout_ref[...]

    def chunk_body(ci, carry):
        row0 = pl.multiple_of(ci * C, C)
        rows = pl.ds(row0, C)
        vls = [slice(h * GLA_DV_HEAD, (h + 1) * GLA_DV_HEAD) for h in range(GLA_HEADS)]
        parts = []
        for h in range(GLA_HEADS):
            v_ref, vl = head_cols((v0_ref, v1_ref), h)
            o_inter, new_state, diag_sums, offs = _gla_scores(b_ref.at[h], q32_ref.at[h], k32_ref.at[h], row0,
                                                              v_ref[rows, vl], state_ref[h])
            state_ref[h] = new_state
            parts.append((o_inter, diag_sums, offs))
        for h in range(GLA_HEADS):
            v_ref, vl = head_cols((v0_ref, v1_ref), h)
            z_ref, zl = head_cols((z0_ref, z1_ref), h)
            o = _gla_output(*parts[h], v_ref[rows, vl])
            o = o * lax.rsqrt(jnp.mean(o * o, axis=-1, keepdims=True) + RMS_EPS) * gout
            z = z_ref[rows, zl].astype(F32)
            y_ref[rows, vls[h]] = (o * (z * jax.nn.sigmoid(z))).astype(BF16)
        return carry

    lax.fori_loop(0, tm // C, chunk_body, 0)
    o_ref[...] = x_ref[...] + jnp.dot(y_ref[...], wout_ref[...], preferred_element_type=F32)


def _gla_layer(x, proj, lr, w_gate_up, b_gate, g_out, w_out, *, tm):
    S, D = x.shape
    half = D_INNER // 2
    v_blk, z_blk = 2 * GLA_DK // half, (2 * GLA_DK + D_INNER) // half
    return pl.pallas_call(
        _gla_kernel,
        grid=(S // tm,),
        in_specs=[
            pl.BlockSpec((tm, GLA_DK), lambda m: (m, 0)),
            pl.BlockSpec((tm, GLA_DK), lambda m: (m, 1)),
            pl.BlockSpec((tm, half), lambda m: (m, v_blk)),
            pl.BlockSpec((tm, half), lambda m: (m, v_blk + 1)),
            pl.BlockSpec((tm, half), lambda m: (m, z_blk)),
            pl.BlockSpec((tm, half), lambda m: (m, z_blk + 1)),
            pl.BlockSpec((tm, LANES), lambda m: (m, 0)),
            pl.BlockSpec((tm, D), lambda m: (m, 0)),
            pl.BlockSpec((LANES, GLA_DK), lambda m: (0, 0)),
            pl.BlockSpec((1, GLA_DK), lambda m: (0, 0)),
            pl.BlockSpec((1, GLA_DV_HEAD), lambda m: (0, 0)),
            pl.BlockSpec((D_INNER, D), lambda m: (0, 0)),
        ],
        out_specs=pl.BlockSpec((tm, D), lambda m: (m, 0)),
        out_shape=jax.ShapeDtypeStruct((S, D), F32),
        scratch_shapes=[
            pltpu.VMEM((GLA_HEADS, GLA_DK_HEAD, GLA_DV_HEAD), F32),
            pltpu.VMEM((tm, D_INNER), BF16),
            pltpu.VMEM((GLA_HEADS, tm, GLA_DK_HEAD), F32),
            pltpu.VMEM((GLA_HEADS, tm, GLA_DK_HEAD), F32),
            pltpu.VMEM((GLA_HEADS, tm, GLA_DK_HEAD), F32),
        ],
        compiler_params=pltpu.CompilerParams(
            dimension_semantics=("arbitrary",), vmem_limit_bytes=VMEM_LIMIT),
        name="gla",
    )(proj, proj, proj, proj, proj, proj, lr, x, w_gate_up, b_gate, g_out, w_out)


def _block_rows(ref):
    return ref[...].reshape(-1, ref.shape[-1])


def _dil_attn_kernel(hq_ref, hkv_ref, wq_ref, wk_ref, wv_ref, o_ref, m_ref, l_ref,
                     q_s, k_s, v_s, bias_ref, m_acc, l_acc, *, slopes, dilation, qb):
    blk = DIL_BLOCK
    i = pl.program_id(1)
    new_rows = slice(blk, (qb + 1) * blk)

    @pl.when((pl.program_id(0) == 0) & (i == 0))
    def _():
        row = lax.broadcasted_iota(jnp.int32, (blk, blk), 0)
        col = lax.broadcasted_iota(jnp.int32, (blk, blk), 1)
        dist_prev = jnp.where(col >= row, ((row + blk - col) * dilation).astype(F32), MASK_DISTANCE)
        dist_cur = jnp.where(col <= row, ((row - col) * dilation).astype(F32), MASK_DISTANCE)
        for h in range(DIL_HEADS):
            bias_ref[h, :, :blk] = -(slopes[h] * LOG2_E) * dist_prev
            bias_ref[h, :, blk:] = -(slopes[h] * LOG2_E) * dist_cur

    def all_heads(first_step):
        hq = _block_rows(hq_ref)
        hkv = _block_rows(hkv_ref)
        heads_per_chunk = PROJ_COLS // DIL_HEAD_DIM
        n_chunks = D_INNER // PROJ_COLS
        projected = [0]

        def project_through(chunk):
            while projected[0] <= min(chunk, n_chunks - 1):
                cols = slice(projected[0] * PROJ_COLS, (projected[0] + 1) * PROJ_COLS)
                q_s[:, cols] = jnp.dot(hq, wq_ref[:, cols], preferred_element_type=F32).astype(BF16)
                k_s[new_rows, cols] = jnp.dot(hkv, wk_ref[:, cols], preferred_element_type=F32).astype(BF16)
                v_s[new_rows, cols] = jnp.dot(hkv, wv_ref[:, cols], preferred_element_type=F32).astype(BF16)
                projected[0] += 1

        def key_rows(j):
            return slice((j + 1) * blk, (j + 2) * blk) if first_step and j == 0 else slice(j * blk, (j + 2) * blk)

        def scores(unit):
            h, j = divmod(unit, qb)
            project_through(h // heads_per_chunk + PROJ_LOOKAHEAD)
            hl = slice(h * DIL_HEAD_DIM, (h + 1) * DIL_HEAD_DIM)
            keys = key_rows(j)
            bias = bias_ref[h, :, 2 * blk - (keys.stop - keys.start):]
            q = q_s[j * blk:(j + 1) * blk, hl]
            return lax.dot_general(q, k_s[keys, hl], _NT, preferred_element_type=F32) + bias

        n_units = DIL_HEADS * qb
        ahead = [scores(u) for u in range(SCORE_LOOKAHEAD)]
        for unit in range(n_units):
            h, j = divmod(unit, qb)
            hl = slice(h * DIL_HEAD_DIM, (h + 1) * DIL_HEAD_DIM)
            s = ahead.pop(0)
            if unit + SCORE_LOOKAHEAD < n_units:
                ahead.append(scores(unit + SCORE_LOOKAHEAD))
            m = jnp.max(s, axis=-1, keepdims=True)
            p = jnp.exp2(s - m)
            l = jnp.sum(p, axis=-1, keepdims=True)
            o = jnp.dot(p.astype(BF16), v_s[key_rows(j), hl], preferred_element_type=F32).astype(o_ref.dtype)
            if len(o_ref.shape) == 2:
                o_ref[j * blk:(j + 1) * blk, hl] = o
            else:
                pieces = o_ref.shape[0] // qb
                o_ref[j * pieces:(j + 1) * pieces, :, hl] = o.reshape(pieces, o_ref.shape[1], DIL_HEAD_DIM)
            m_acc[j * blk:(j + 1) * blk, h:h + 1] = m
            l_acc[j * blk:(j + 1) * blk, h:h + 1] = l

    m_acc[...] = jnp.zeros_like(m_acc)
    l_acc[...] = jnp.zeros_like(l_acc)

    @pl.when(i == 0)
    def _():
        all_heads(True)

    @pl.when(i > 0)
    def _():
        all_heads(False)

    m_ref[...] = m_acc[...].reshape(m_ref.shape)
    l_ref[...] = l_acc[...].reshape(l_ref.shape)
    k_s[:blk, :] = k_s[qb * blk:, :]
    v_s[:blk, :] = v_s[qb * blk:, :]


def _dil_attn_group(hv, w_kv, w_q, g, *, S, qb):
    _, d = DIL_GROUPS[g]
    D = hv.shape[-1]
    rows = qb * DIL_BLOCK
    n_steps = S // d // rows
    n_all = N_GROUPS * DIL_HEADS
    slopes = tuple(2.0 ** (-ALIBI_MAX_EXP * (g * DIL_HEADS + h + 1.0) / n_all) for h in range(DIL_HEADS))
    kcol, vcol, qcol = g, N_GROUPS + g, g
    if d == 1:
        h_spec = lambda v: pl.BlockSpec((None, rows, D), lambda r, i: (v, i, 0))
        spec = lambda width: pl.BlockSpec((rows, width), lambda r, i: (i, 0))
        shape = lambda width, dt: jax.ShapeDtypeStruct((S, width), dt)
    else:
        run = PERM_ROWS // d
        pieces = rows // run
        hv = hv.reshape(hv.shape[0], S // PERM_ROWS, d, run, D)
        h_spec = lambda v: pl.BlockSpec((None, pieces, None, run, D), lambda r, i: (v, i, r, 0, 0))
        spec = lambda width: pl.BlockSpec((pieces, None, run, width), lambda r, i: (i, r, 0, 0))
        shape = lambda width, dt: jax.ShapeDtypeStruct((S // PERM_ROWS, d, run, width), dt)
    w_spec = lambda col: pl.BlockSpec((D, D_INNER), lambda r, i: (0, col))
    o, m, l = pl.pallas_call(
        functools.partial(_dil_attn_kernel, slopes=slopes, dilation=d, qb=qb),
        grid=(d, n_steps),
        in_specs=[h_spec(N_GROUPS + g), h_spec(g), w_spec(qcol), w_spec(kcol), w_spec(vcol)],
        out_specs=[spec(D_INNER), spec(LANES), spec(LANES)],
        out_shape=[shape(D_INNER, BF16), shape(LANES, F32), shape(LANES, F32)],
        scratch_shapes=[
            pltpu.VMEM((rows, D_INNER), BF16),
            pltpu.VMEM((rows + DIL_BLOCK, D_INNER), BF16),
            pltpu.VMEM((rows + DIL_BLOCK, D_INNER), BF16),
            pltpu.VMEM((DIL_HEADS, DIL_BLOCK, 2 * DIL_BLOCK), F32),
            pltpu.VMEM((rows, LANES), F32),
            pltpu.VMEM((rows, LANES), F32),
        ],
        compiler_params=pltpu.CompilerParams(
            dimension_semantics=("arbitrary", "arbitrary"), vmem_limit_bytes=VMEM_LIMIT),
        name=f"dil_attn_g{g}",
    )(hv, hv, w_q, w_kv, w_kv)
    return o.reshape(S, D_INNER), m.reshape(S, LANES), l.reshape(S, LANES)


def _combine_kernel(o0_ref, o1_ref, o2_ref, m0_ref, m1_ref, m2_ref, l0_ref, l1_ref, l2_ref, z_ref, x_ref,
                    wout_ref, g_ref, out_ref, y_ref, ot_ref):
    tm = x_ref.shape[0]
    o_refs = (o0_ref, o1_ref, o2_ref)
    m_refs = (m0_ref, m1_ref, m2_ref)
    l_refs = (l0_ref, l1_ref, l2_ref)
    lane = lax.broadcasted_iota(jnp.int32, (PERM_ROWS, LANES), 1)
    unperms = [None] + [_deinterleave_matrix(d, transpose=True) for _, d in DIL_GROUPS[1:]]

    def to_token_order(b):
        rows = slice(b * PERM_ROWS, (b + 1) * PERM_ROWS)
        ms, ls = [m_refs[0][rows, :]], [l_refs[0][rows, :]]
        for g in range(1, N_GROUPS):
            ot_ref[g - 1, rows, :] = jnp.dot(unperms[g], o_refs[g][rows, :], preferred_element_type=F32)
            for stats, ref in ((ms, m_refs[g]), (ls, l_refs[g])):
                stats.append(sum(jnp.dot(unperms[g], piece, preferred_element_type=F32)
                                 for piece in _split3(ref[rows, :])))
        return ms, ls

    def mix_and_project(b, ms, ls):
        rows = slice(b * PERM_ROWS, (b + 1) * PERM_ROWS)
        m = jnp.maximum(jnp.maximum(ms[0], ms[1]), ms[2])
        es = [jnp.exp2(mg - m) for mg in ms]
        den = es[0] * ls[0] + es[1] * ls[1] + es[2] * ls[2]
        ws = [jnp.where(lane < DIL_HEADS, e / den, 0.0) for e in es]
        for h in range(DIL_HEADS):
            hl = slice(h * DIL_HEAD_DIM, (h + 1) * DIL_HEAD_DIM)
            acc = ws[0][:, h:h + 1] * o_refs[0][rows, hl].astype(F32)
            for g in range(1, N_GROUPS):
                acc = acc + ws[g][:, h:h + 1] * ot_ref[g - 1, rows, hl]
            z = z_ref[rows, hl].astype(F32)
            y_ref[rows, hl] = (acc * (z * jax.nn.sigmoid(z))).astype(BF16)
        x = x_ref[rows, :] + jnp.dot(y_ref[rows, :], wout_ref[...], preferred_element_type=F32)
        y = x * lax.rsqrt(jnp.mean(x * x, axis=-1, keepdims=True) + RMS_EPS)
        out_ref[rows, :] = y * g_ref[...]

    n_blocks = tm // PERM_ROWS
    stats = to_token_order(0)
    for b in range(n_blocks):
        nxt = to_token_order(b + 1) if b + 1 < n_blocks else None
        mix_and_project(b, *stats)
        stats = nxt


def _combine(os_, ms, ls, z, x, w_out, g_final, *, tm):
    S, D = x.shape
    row_blk = lambda m: (m, 0)
    return pl.pallas_call(
        _combine_kernel,
        grid=(S // tm,),
        in_specs=[pl.BlockSpec((tm, D_INNER), row_blk)] * 3 + [pl.BlockSpec((tm, LANES), row_blk)] * 6 + [
            pl.BlockSpec((tm, D_INNER), row_blk),
            pl.BlockSpec((tm, D), row_blk),
            pl.BlockSpec((D_INNER, D), lambda m: (0, 0)),
            pl.BlockSpec((1, D), lambda m: (0, 0)),
        ],
        out_specs=pl.BlockSpec((tm, D), row_blk),
        out_shape=jax.ShapeDtypeStruct((S, D), F32),
        scratch_shapes=[pltpu.VMEM((tm, D_INNER), BF16), pltpu.VMEM((N_GROUPS - 1, tm, D_INNER), F32)],
        compiler_params=pltpu.CompilerParams(
            dimension_semantics=("arbitrary",), vmem_limit_bytes=VMEM_LIMIT),
        name="combine",
    )(*os_, *ms, *ls, z, x, w_out, g_final)


def kernel(x, a_norm, a_w_in, a_w_gate_up, a_b_gate, a_g_out, a_w_out, kv_norm, w_kv, b_norm, b_w_in, b_w_out,
           final_norm):
    B, S, D = x.shape
    assert B == 1 and D == D_MODEL and a_norm.shape[0] == 1 and b_norm.shape[0] == 1
    assert S % (ATTN_BLOCKS_PER_STEP * DIL_BLOCK * max(d for _, d in DIL_GROUPS)) == 0
    x0 = x.reshape(S, D)

    w_a = a_w_in[0].astype(BF16)
    n_a = 2 * GLA_DK + 2 * D_INNER
    w_lr = jnp.pad(w_a[:, n_a:], ((0, 0), (0, LANES - GLA_GATE_RANK)))
    proj_a, lr = _norm_proj(x0, a_norm, [(0, 1)], jnp.zeros((2,), jnp.int32), w_a, w_lr, n_out=n_a, tm=1024,
                            tn=n_a // 2)
    w_gu = jnp.pad(a_w_gate_up[0], ((0, LANES - GLA_GATE_RANK), (0, 0))).astype(BF16)
    x1 = _gla_layer(x0, proj_a, lr, w_gu, a_b_gate, a_g_out, a_w_out[0].astype(BF16), tm=512)

    w_kv_b, w_qz = w_kv.astype(BF16), b_w_in[0].astype(BF16)
    q_scale = DIL_HEAD_DIM ** -0.5 * LOG2_E
    gains = jnp.stack([kv_norm, b_norm[0] * q_scale, b_norm[0]])
    dils = [d for _, d in DIL_GROUPS]
    emit = [(0, d) for d in dils] + [(1, d) for d in dils]
    z, hv = _norm_proj(x1, gains, [(2, 1)], jnp.zeros((2,), jnp.int32), w_qz, None, emit, n_out=D_INNER,
                       w_col0=N_GROUPS * D_INNER, tm=1024, tn=D_INNER // 2)

    outs = [_dil_attn_group(hv, w_kv_b, w_qz, g, S=S, qb=ATTN_BLOCKS_PER_STEP) for g in range(N_GROUPS)]
    out = _combine([o for o, _, _ in outs], [m for _, m, _ in outs], [l for _, _, l in outs], z, x1,
                   b_w_out[0].astype(BF16), final_norm.reshape(1, D), tm=512)
    return out.reshape(B, S, D)
```

```python
import functools

import jax
import jax.numpy as jnp
from jax import lax
from jax.experimental import pallas as pl
from jax.experimental.pallas import tpu as pltpu

F32 = jnp.float32
BF16 = jnp.bfloat16

RMS_EPS = 1e-6
D_MODEL = 1024
D_INNER = 2048
GLA_HEADS = 4
GLA_DK_HEAD = 128
GLA_DV_HEAD = 512
GLA_DK = GLA_HEADS * GLA_DK_HEAD
GLA_GATE_RANK = 16
GLA_GATE_TAU = 16.0
GLA_CHUNK = 64
GLA_SUB = 16
GLA_CUMSUM_ROWS = 256
GLA_OUT_ROWS = 256
DIL_GROUPS = ((128, 1), (512, 4), (2048, 16))
N_GROUPS = 3
DIL_HEADS = 16
DIL_HEAD_DIM = 128
DIL_BLOCK = 128
ALIBI_MAX_EXP = 8.0
MASK_DISTANCE = 1e34
PERM_ROWS = 256
LOG2_E = 1.4426950408889634
PROJ_COLS = 256
PROJ_LOOKAHEAD = 1
SCORE_LOOKAHEAD = 4
ATTN_BLOCKS_PER_STEP = 4

LANES = 128
VMEM_LIMIT = 56 * 1024 * 1024

_NT = (((1,), (1,)), ((), ()))


def _deinterleave_matrix(d, transpose=False):
    n = PERM_ROWS // d
    p = lax.broadcasted_iota(jnp.int32, (PERM_ROWS, PERM_ROWS), 1 if transpose else 0)
    j = lax.broadcasted_iota(jnp.int32, (PERM_ROWS, PERM_ROWS), 0 if transpose else 1)
    return (j == (p % n) * d + p // n).astype(BF16)


def _norm_proj_kernel(var_ref, x_ref, g_ref, w_ref, *rest, mm_variants, emit_variants, has_aux):
    rest = list(rest)
    waux_ref = rest.pop(0) if has_aux else None
    o_ref = rest.pop(0)
    aux_ref = rest.pop(0) if has_aux else None
    hv_ref = rest.pop(0) if emit_variants else None
    (h_ref,) = rest
    n = pl.program_id(1)
    tm = x_ref.shape[0]

    @pl.when(n == 0)
    def _():
        x = x_ref[...]
        y = x * lax.rsqrt(jnp.mean(x * x, axis=-1, keepdims=True) + RMS_EPS)
        targets = ([(h_ref, i, v) for i, v in enumerate(mm_variants)]
                   + [(hv_ref, i, v) for i, v in enumerate(emit_variants)])
        for gi in sorted({g for _, _, (g, _) in targets}):
            h = (y * g_ref[gi:gi + 1, :]).astype(BF16)
            for ref, i, (g, d) in targets:
                if g != gi:
                    continue
                if d == 1:
                    ref[i] = h
                else:
                    perm = _deinterleave_matrix(d)
                    for b in range(tm // PERM_ROWS):
                        rows = slice(b * PERM_ROWS, (b + 1) * PERM_ROWS)
                        ref[i, rows, :] = jnp.dot(perm, h[rows], preferred_element_type=F32).astype(BF16)
        if has_aux:
            aux_ref[...] = jnp.dot(h_ref[0], waux_ref[...], preferred_element_type=F32)

    h = h_ref[var_ref[n]]
    o_ref[...] = jnp.dot(h, w_ref[...], preferred_element_type=F32).astype(o_ref.dtype)


def _norm_proj(x, gains, mm_variants, var_of_block, w, w_aux, emit_variants=(), *, n_out, w_col0=0, tm, tn):
    S, D = x.shape
    N = n_out
    n_gains = gains.shape[0]
    has_aux = w_aux is not None
    grid = (S // tm, N // tn)
    w_blk0 = w_col0 // tn
    in_specs = [
        pl.BlockSpec((tm, D), lambda m, n, var: (m, 0)),
        pl.BlockSpec((n_gains, D), lambda m, n, var: (0, 0)),
        pl.BlockSpec((D, tn), lambda m, n, var: (0, w_blk0 + n)),
    ]
    out_shape = [jax.ShapeDtypeStruct((S, N), BF16)]
    out_specs = [pl.BlockSpec((tm, tn), lambda m, n, var: (m, n))]
    args = [x, gains, w]
    if has_aux:
        in_specs.append(pl.BlockSpec((D, LANES), lambda m, n, var: (0, 0)))
        out_shape.append(jax.ShapeDtypeStruct((S, LANES), F32))
        out_specs.append(pl.BlockSpec((tm, LANES), lambda m, n, var: (m, 0)))
        args.append(w_aux)
    if emit_variants:
        out_shape.append(jax.ShapeDtypeStruct((len(emit_variants), S, D), BF16))
        out_specs.append(pl.BlockSpec((len(emit_variants), tm, D), lambda m, n, var: (0, m, 0)))
    return pl.pallas_call(
        functools.partial(_norm_proj_kernel, mm_variants=tuple(mm_variants), emit_variants=tuple(emit_variants),
                          has_aux=has_aux),
        grid_spec=pltpu.PrefetchScalarGridSpec(
            num_scalar_prefetch=1, grid=grid, in_specs=in_specs, out_specs=out_specs,
            scratch_shapes=[pltpu.VMEM((len(mm_variants), tm, D), BF16)]),
        out_shape=out_shape,
        compiler_params=pltpu.CompilerParams(
            dimension_semantics=("arbitrary", "arbitrary"), vmem_limit_bytes=VMEM_LIMIT),
        name="norm_proj_aux" if has_aux else "norm_proj",
    )(var_of_block, *args)


def _split3(a):
    hi = a.astype(BF16)
    r1 = a - hi.astype(F32)
    mid = r1.astype(BF16)
    lo = (r1 - mid.astype(F32)).astype(BF16)
    return hi, mid, lo


def _gla_scores(b_ref, q_ref, k_ref, row0, v, state):
    C, SUB, HALF = GLA_CHUNK, GLA_SUB, GLA_SUB // 2
    rows = pl.ds(row0, C)
    b, q, k = b_ref[rows, :], q_ref[rows, :], k_ref[rows, :]
    dk = q.shape[1]
    b_last = b_ref[pl.ds(row0 + C - 1, 1), :]
    o_inter = jnp.dot((q * jnp.exp2(b)).astype(BF16), state.astype(BF16), preferred_element_type=F32)
    k_dec = k * jnp.exp2(b_last - b)
    upd = jnp.dot(k_dec.T.astype(BF16), v, preferred_element_type=F32)
    e_col = jnp.broadcast_to(jnp.exp2(b_last), (dk, dk)).T
    new_state = state * jnp.concatenate([e_col] * (v.shape[1] // dk), axis=1) + upd

    ones = jnp.ones((dk, LANES), BF16)
    diag_sums, offs = [], []
    for i in range(C // SUB):
        r0 = i * SUB
        halves = [(b[r0 + u * HALF:r0 + (u + 1) * HALF], q[r0 + u * HALF:r0 + (u + 1) * HALF]) for u in range(2)]
        prods = []
        for s in range(SUB):
            bs = b_ref[pl.ds(row0 + r0 + s, 1), :]
            ks = k_ref[pl.ds(row0 + r0 + s, 1), :]
            for u in range(s // HALF, 2):
                prods.append(halves[u][1] * jnp.exp2(halves[u][0] - bs) * ks)
        diag_sums.append(jnp.dot(jnp.concatenate(prods, axis=0).astype(BF16), ones, preferred_element_type=F32))
        if i > 0:
            b0 = b_ref[pl.ds(row0 + r0, 1), :]
            qt = (q[r0:r0 + SUB] * jnp.exp2(b[r0:r0 + SUB] - b0)).astype(BF16)
            kt = jnp.concatenate([(k[:r0] * jnp.exp2(b0 - b[:r0])).astype(BF16), jnp.zeros((C - r0, dk), BF16)], axis=0)
            offs.append(lax.dot_general(qt, kt, _NT, preferred_element_type=F32))
    return o_inter, new_state, diag_sums, offs


def _gla_output(o_inter, diag_sums, offs, v):
    C, SUB, HALF = GLA_CHUNK, GLA_SUB, GLA_SUB // 2
    lane = lax.broadcasted_iota(jnp.int32, (HALF, LANES), 1)
    row = lax.broadcasted_iota(jnp.int32, (HALF, LANES), 0)
    a_rows = []
    for i in range(C // SUB):
        r0 = i * SUB
        acc = [jnp.zeros((HALF, LANES), F32) for _ in range(2)]
        piece = 0
        for s in range(SUB):
            for u in range(s // HALF, 2):
                acc[u] = jnp.where(lane == r0 + s, diag_sums[i][piece * HALF:(piece + 1) * HALF, :], acc[u])
                piece += 1
        a_i = jnp.concatenate([jnp.where(lane <= row + r0 + u * HALF, acc[u], 0.0) for u in range(2)], axis=0)[:, :C]
        a_rows.append(a_i + offs[i - 1] if i > 0 else a_i)
    a = jnp.concatenate(a_rows, axis=0).astype(BF16)
    return o_inter + jnp.dot(a, v, preferred_element_type=F32)


def _gla_kernel(q_ref, k_ref, v0_ref, v1_ref, z0_ref, z1_ref, lr_ref, x_ref, wgu_ref, bg_ref, gout_ref, wout_ref,
                o_ref, state_ref, y_ref, b_ref, q32_ref, k32_ref):
    tm = q_ref.shape[0]
    C = GLA_CHUNK
    heads_per_ref = v0_ref.shape[1] // GLA_DV_HEAD

    def head_cols(refs, h):
        u = h % heads_per_ref
        return refs[h // heads_per_ref], slice(u * GLA_DV_HEAD, (u + 1) * GLA_DV_HEAD)

    @pl.when(pl.program_id(0) == 0)
    def _():
        state_ref[...] = jnp.zeros_like(state_ref)

    pre = jnp.dot(lr_ref[...].astype(BF16), wgu_ref[...], preferred_element_type=F32) + bg_ref[...]
    log2_a = (jnp.minimum(pre, 0.0) - jnp.log1p(jnp.exp(-jnp.abs(pre)))) * (LOG2_E / GLA_GATE_TAU)
    tb = min(tm, GLA_CUMSUM_ROWS)
    r = lax.broadcasted_iota(jnp.int32, (tb, tb), 0)
    c = lax.broadcasted_iota(jnp.int32, (tb, tb), 1)
    tri = ((c <= r) & ((c // C) == (r // C))).astype(BF16)
    b_all = jnp.concatenate(
        [sum(jnp.dot(tri, piece, preferred_element_type=F32) for piece in _split3(log2_a[t * tb:(t + 1) * tb]))
         for t in range(tm // tb)], axis=0)
    for h in range(GLA_HEADS):
        kl = slice(h * GLA_DK_HEAD, (h + 1) * GLA_DK_HEAD)
        b_ref[h] = b_all[:, kl]
        q32_ref[h] = q_ref[:, kl].astype(F32) * (GLA_DK_HEAD ** -0.5)
        k32_ref[h] = k_ref[:, kl].astype(F32)

    gout = gout_ref[...]

    out_rows = GLA_OUT_ROWS
    for ci in range(tm // C):
        row0 = ci * C
        rows = slice(row0, row0 + C)
        vls = [slice(h * GLA_DV_HEAD, (h + 1) * GLA_DV_HEAD) for h in range(GLA_HEADS)]
        parts = []
        for h in range(GLA_HEADS):
            v_ref, vl = head_cols((v0_ref, v1_ref), h)
            o_inter, new_state, diag_sums, offs = _gla_scores(b_ref.at[h], q32_ref.at[h], k32_ref.at[h], row0,
                                                              v_ref[rows, vl], state_ref[h])
            state_ref[h] = new_state
            parts.append((o_inter, diag_sums, offs))
        for h in range(GLA_HEADS):
            v_ref, vl = head_cols((v0_ref, v1_ref), h)
            z_ref, zl = head_cols((z0_ref, z1_ref), h)
            o = _gla_output(*parts[h], v_ref[rows, vl])
            o = o * lax.rsqrt(jnp.mean(o * o, axis=-1, keepdims=True) + RMS_EPS) * gout
            z = z_ref[rows, zl].astype(F32)
            y_ref[rows, vls[h]] = (o * (z * jax.nn.sigmoid(z))).astype(BF16)
        if (row0 + C) % out_rows == 0:
            done = slice(row0 + C - out_rows, row0 + C)
            o_ref[done, :] = x_ref[done, :] + jnp.dot(y_ref[done, :], wout_ref[...], preferred_element_type=F32)


def _gla_layer(x, proj, lr, w_gate_up, b_gate, g_out, w_out, *, tm):
    S, D = x.shape
    half = D_INNER // 2
    v_blk, z_blk = 2 * GLA_DK // half, (2 * GLA_DK + D_INNER) // half
    return pl.pallas_call(
        _gla_kernel,
        grid=(S // tm,),
        in_specs=[
            pl.BlockSpec((tm, GLA_DK), lambda m: (m, 0)),
            pl.BlockSpec((tm, GLA_DK), lambda m: (m, 1)),
            pl.BlockSpec((tm, half), lambda m: (m, v_blk)),
            pl.BlockSpec((tm, half), lambda m: (m, v_blk + 1)),
            pl.BlockSpec((tm, half), lambda m: (m, z_blk)),
            pl.BlockSpec((tm, half), lambda m: (m, z_blk + 1)),
            pl.BlockSpec((tm, LANES), lambda m: (m, 0)),
            pl.BlockSpec((tm, D), lambda m: (m, 0)),
            pl.BlockSpec((LANES, GLA_DK), lambda m: (0, 0)),
            pl.BlockSpec((1, GLA_DK), lambda m: (0, 0)),
            pl.BlockSpec((1, GLA_DV_HEAD), lambda m: (0, 0)),
            pl.BlockSpec((D_INNER, D), lambda m: (0, 0)),
        ],
        out_specs=pl.BlockSpec((tm, D), lambda m: (m, 0)),
        out_shape=jax.ShapeDtypeStruct((S, D), F32),
        scratch_shapes=[
            pltpu.VMEM((GLA_HEADS, GLA_DK_HEAD, GLA_DV_HEAD), F32),
            pltpu.VMEM((tm, D_INNER), BF16),
            pltpu.VMEM((GLA_HEADS, tm, GLA_DK_HEAD), F32),
            pltpu.VMEM((GLA_HEADS, tm, GLA_DK_HEAD), F32),
            pltpu.VMEM((GLA_HEADS, tm, GLA_DK_HEAD), F32),
        ],
        compiler_params=pltpu.CompilerParams(
            dimension_semantics=("arbitrary",), vmem_limit_bytes=VMEM_LIMIT),
        name="gla",
    )(proj, proj, proj, proj, proj, proj, lr, x, w_gate_up, b_gate, g_out, w_out)


def _block_rows(ref):
    return ref[...].reshape(-1, ref.shape[-1])


def _dil_attn_kernel(hq_ref, hkv_ref, wq_ref, wk_ref, wv_ref, o_ref, m_ref, l_ref,
                     q_s, k_s, v_s, bias_ref, m_acc, l_acc, *, slopes, dilation, qb):
    blk = DIL_BLOCK
    i = pl.program_id(1)
    new_rows = slice(blk, (qb + 1) * blk)

    @pl.when((pl.program_id(0) == 0) & (i == 0))
    def _():
        row = lax.broadcasted_iota(jnp.int32, (blk, blk), 0)
        col = lax.broadcasted_iota(jnp.int32, (blk, blk), 1)
        dist_prev = jnp.where(col >= row, ((row + blk - col) * dilation).astype(F32), MASK_DISTANCE)
        dist_cur = jnp.where(col <= row, ((row - col) * dilation).astype(F32), MASK_DISTANCE)
        for h in range(DIL_HEADS):
            bias_ref[h, :, :blk] = -(slopes[h] * LOG2_E) * dist_prev
            bias_ref[h, :, blk:] = -(slopes[h] * LOG2_E) * dist_cur

    def all_heads(first_step):
        hq = _block_rows(hq_ref)
        hkv = _block_rows(hkv_ref)
        heads_per_chunk = PROJ_COLS // DIL_HEAD_DIM
        n_chunks = D_INNER // PROJ_COLS
        projected = [0]

        def project_through(chunk):
            while projected[0] <= min(chunk, n_chunks - 1):
                cols = slice(projected[0] * PROJ_COLS, (projected[0] + 1) * PROJ_COLS)
                q_s[:, cols] = jnp.dot(hq, wq_ref[:, cols], preferred_element_type=F32).astype(BF16)
                k_s[new_rows, cols] = jnp.dot(hkv, wk_ref[:, cols], preferred_element_type=F32).astype(BF16)
                v_s[new_rows, cols] = jnp.dot(hkv, wv_ref[:, cols], preferred_element_type=F32).astype(BF16)
                projected[0] += 1

        def key_rows(j):
            return slice((j + 1) * blk, (j + 2) * blk) if first_step and j == 0 else slice(j * blk, (j + 2) * blk)

        def scores(unit):
            h, j = divmod(unit, qb)
            project_through(h // heads_per_chunk + PROJ_LOOKAHEAD)
            hl = slice(h * DIL_HEAD_DIM, (h + 1) * DIL_HEAD_DIM)
            keys = key_rows(j)
            bias = bias_ref[h, :, 2 * blk - (keys.stop - keys.start):]
            q = q_s[j * blk:(j + 1) * blk, hl]
            return lax.dot_general(q, k_s[keys, hl], _NT, preferred_element_type=F32) + bias

        n_units = DIL_HEADS * qb
        ahead = [scores(u) for u in range(SCORE_LOOKAHEAD)]
        for unit in range(n_units):
            h, j = divmod(unit, qb)
            hl = slice(h * DIL_HEAD_DIM, (h + 1) * DIL_HEAD_DIM)
            s = ahead.pop(0)
            if unit + SCORE_LOOKAHEAD < n_units:
                ahead.append(scores(unit + SCORE_LOOKAHEAD))
            m = jnp.max(s, axis=-1, keepdims=True)
            p = jnp.exp2(s - m)
            l = jnp.sum(p, axis=-1, keepdims=True)
            o = jnp.dot(p.astype(BF16), v_s[key_rows(j), hl], preferred_element_type=F32).astype(o_ref.dtype)
            if len(o_ref.shape) == 2:
                o_ref[j * blk:(j + 1) * blk, hl] = o
            else:
                pieces = o_ref.shape[0] // qb
                o_ref[j * pieces:(j + 1) * pieces, :, hl] = o.reshape(pieces, o_ref.shape[1], DIL_HEAD_DIM)
            m_acc[j * blk:(j + 1) * blk, h:h + 1] = m
            l_acc[j * blk:(j + 1) * blk, h:h + 1] = l

    m_acc[...] = jnp.zeros_like(m_acc)
    l_acc[...] = jnp.zeros_like(l_acc)

    @pl.when(i == 0)
    def _():
        all_heads(True)

    @pl.when(i > 0)
    def _():
        all_heads(False)

    m_ref[...] = m_acc[...].reshape(m_ref.shape)
    l_ref[...] = l_acc[...].reshape(l_ref.shape)
    k_s[:blk, :] = k_s[qb * blk:, :]
    v_s[:blk, :] = v_s[qb * blk:, :]


def _dil_attn_group(hv, w_kv, w_q, g, *, S, qb):
    _, d = DIL_GROUPS[g]
    D = hv.shape[-1]
    rows = qb * DIL_BLOCK
    n_steps = S // d // rows
    n_all = N_GROUPS * DIL_HEADS
    slopes = tuple(2.0 ** (-ALIBI_MAX_EXP * (g * DIL_HEADS + h + 1.0) / n_all) for h in range(DIL_HEADS))
    kcol, vcol, qcol = g, N_GROUPS + g, g
    if d == 1:
        h_spec = lambda v: pl.BlockSpec((None, rows, D), lambda r, i: (v, i, 0))
        spec = lambda width: pl.BlockSpec((rows, width), lambda r, i: (i, 0))
        shape = lambda width, dt: jax.ShapeDtypeStruct((S, width), dt)
    else:
        run = PERM_ROWS // d
        pieces = rows // run
        hv = hv.reshape(hv.shape[0], S // PERM_ROWS, d, run, D)
        h_spec = lambda v: pl.BlockSpec((None, pieces, None, run, D), lambda r, i: (v, i, r, 0, 0))
        spec = lambda width: pl.BlockSpec((pieces, None, run, width), lambda r, i: (i, r, 0, 0))
        shape = lambda width, dt: jax.ShapeDtypeStruct((S // PERM_ROWS, d, run, width), dt)
    w_spec = lambda col: pl.BlockSpec((D, D_INNER), lambda r, i: (0, col))
    o, m, l = pl.pallas_call(
        functools.partial(_dil_attn_kernel, slopes=slopes, dilation=d, qb=qb),
        grid=(d, n_steps),
        in_specs=[h_spec(N_GROUPS + g), h_spec(g), w_spec(qcol), w_spec(kcol), w_spec(vcol)],
        out_specs=[spec(D_INNER), spec(LANES), spec(LANES)],
        out_shape=[shape(D_INNER, BF16), shape(LANES, F32), shape(LANES, F32)],
        scratch_shapes=[
            pltpu.VMEM((rows, D_INNER), BF16),
            pltpu.VMEM((rows + DIL_BLOCK, D_INNER), BF16),
            pltpu.VMEM((rows + DIL_BLOCK, D_INNER), BF16),
            pltpu.VMEM((DIL_HEADS, DIL_BLOCK, 2 * DIL_BLOCK), F32),
            pltpu.VMEM((rows, LANES), F32),
            pltpu.VMEM((rows, LANES), F32),
        ],
        compiler_params=pltpu.CompilerParams(
            dimension_semantics=("arbitrary", "arbitrary"), vmem_limit_bytes=VMEM_LIMIT),
        name=f"dil_attn_g{g}",
    )(hv, hv, w_q, w_kv, w_kv)
    return o.reshape(S, D_INNER), m.reshape(S, LANES), l.reshape(S, LANES)


def _combine_kernel(o0_ref, o1_ref, o2_ref, m0_ref, m1_ref, m2_ref, l0_ref, l1_ref, l2_ref, z_ref, x_ref,
                    wout_ref, g_ref, out_ref, y_ref, ot_ref):
    tm = x_ref.shape[0]
    o_refs = (o0_ref, o1_ref, o2_ref)
    m_refs = (m0_ref, m1_ref, m2_ref)
    l_refs = (l0_ref, l1_ref, l2_ref)
    lane = lax.broadcasted_iota(jnp.int32, (PERM_ROWS, LANES), 1)
    unperms = [None] + [_deinterleave_matrix(d, transpose=True) for _, d in DIL_GROUPS[1:]]

    def to_token_order(b):
        rows = slice(b * PERM_ROWS, (b + 1) * PERM_ROWS)
        ms, ls = [m_refs[0][rows, :]], [l_refs[0][rows, :]]
        for g in range(1, N_GROUPS):
            ot_ref[g - 1, rows, :] = jnp.dot(unperms[g], o_refs[g][rows, :], preferred_element_type=F32)
            for stats, ref in ((ms, m_refs[g]), (ls, l_refs[g])):
                stats.append(sum(jnp.dot(unperms[g], piece, preferred_element_type=F32)
                                 for piece in _split3(ref[rows, :])))
        return ms, ls

    def mix_and_project(b, ms, ls):
        rows = slice(b * PERM_ROWS, (b + 1) * PERM_ROWS)
        m = jnp.maximum(jnp.maximum(ms[0], ms[1]), ms[2])
        es = [jnp.exp2(mg - m) for mg in ms]
        den = es[0] * ls[0] + es[1] * ls[1] + es[2] * ls[2]
        ws = [jnp.where(lane < DIL_HEADS, e / den, 0.0) for e in es]
        for h in range(DIL_HEADS):
            hl = slice(h * DIL_HEAD_DIM, (h + 1) * DIL_HEAD_DIM)
            acc = ws[0][:, h:h + 1] * o_refs[0][rows, hl].astype(F32)
            for g in range(1, N_GROUPS):
                acc = acc + ws[g][:, h:h + 1] * ot_ref[g - 1, rows, hl]
            z = z_ref[rows, hl].astype(F32)
            y_ref[rows, hl] = (acc * (z * jax.nn.sigmoid(z))).astype(BF16)
        x = x_ref[rows, :] + jnp.dot(y_ref[rows, :], wout_ref[...], preferred_element_type=F32)
        y = x * lax.rsqrt(jnp.mean(x * x, axis=-1, keepdims=True) + RMS_EPS)
        out_ref[rows, :] = y * g_ref[...]

    n_blocks = tm // PERM_ROWS
    stats = to_token_order(0)
    for b in range(n_blocks):
        nxt = to_token_order(b + 1) if b + 1 < n_blocks else None
        mix_and_project(b, *stats)
        stats = nxt


def _combine(os_, ms, ls, z, x, w_out, g_final, *, tm):
    S, D = x.shape
    row_blk = lambda m: (m, 0)
    return pl.pallas_call(
        _combine_kernel,
        grid=(S // tm,),
        in_specs=[pl.BlockSpec((tm, D_INNER), row_blk)] * 3 + [pl.BlockSpec((tm, LANES), row_blk)] * 6 + [
            pl.BlockSpec((tm, D_INNER), row_blk),
            pl.BlockSpec((tm, D), row_blk),
            pl.BlockSpec((D_INNER, D), lambda m: (0, 0)),
            pl.BlockSpec((1, D), lambda m: (0, 0)),
        ],
        out_specs=pl.BlockSpec((tm, D), row_blk),
        out_shape=jax.ShapeDtypeStruct((S, D), F32),
        scratch_shapes=[pltpu.VMEM((tm, D_INNER), BF16), pltpu.VMEM((N_GROUPS - 1, tm, D_INNER), F32)],
        compiler_params=pltpu.CompilerParams(
            dimension_semantics=("arbitrary",), vmem_limit_bytes=VMEM_LIMIT),
        name="combine",
    )(*os_, *ms, *ls, z, x, w_out, g_final)


def kernel(x, a_norm, a_w_in, a_w_gate_up, a_b_gate, a_g_out, a_w_out, kv_norm, w_kv, b_norm, b_w_in, b_w_out,
           final_norm):
    B, S, D = x.shape
    assert B == 1 and D == D_MODEL and a_norm.shape[0] == 1 and b_norm.shape[0] == 1
    assert S % (ATTN_BLOCKS_PER_STEP * DIL_BLOCK * max(d for _, d in DIL_GROUPS)) == 0
    x0 = x.reshape(S, D)

    w_a = a_w_in[0].astype(BF16)
    n_a = 2 * GLA_DK + 2 * D_INNER
    w_lr = jnp.pad(w_a[:, n_a:], ((0, 0), (0, LANES - GLA_GATE_RANK)))
    proj_a, lr = _norm_proj(x0, a_norm, [(0, 1)], jnp.zeros((2,), jnp.int32), w_a, w_lr, n_out=n_a, tm=1024,
                            tn=n_a // 2)
    w_gu = jnp.pad(a_w_gate_up[0], ((0, LANES - GLA_GATE_RANK), (0, 0))).astype(BF16)
    x1 = _gla_layer(x0, proj_a, lr, w_gu, a_b_gate, a_g_out, a_w_out[0].astype(BF16), tm=512)

    w_kv_b, w_qz = w_kv.astype(BF16), b_w_in[0].astype(BF16)
    q_scale = DIL_HEAD_DIM ** -0.5 * LOG2_E
    gains = jnp.stack([kv_norm, b_norm[0] * q_scale, b_norm[0]])
    dils = [d for _, d in DIL_GROUPS]
    emit = [(0, d) for d in dils] + [(1, d) for d in dils]
    z, hv = _norm_proj(x1, gains, [(2, 1)], jnp.zeros((2,), jnp.int32), w_qz, None, emit, n_out=D_INNER,
                       w_col0=N_GROUPS * D_INNER, tm=1024, tn=D_INNER // 2)

    outs = [_dil_attn_group(hv, w_kv_b, w_qz, g, S=S, qb=ATTN_BLOCKS_PER_STEP) for g in range(N_GROUPS)]
    out = _combine([o for o, _, _ in outs], [m for _, m, _ in outs], [l for _, _, l in outs], z, x1,
                   b_w_out[0].astype(BF16), final_norm.reshape(1, D), tm=512)
    return out.reshape(B, S, D)
```

```python
import functools

import jax
import jax.numpy as jnp
from jax import lax
from jax.experimental import pallas as pl
from jax.experimental.pallas import tpu as pltpu

F32 = jnp.float32
BF16 = jnp.bfloat16

RMS_EPS = 1e-6
D_MODEL = 1024
D_INNER = 2048
GLA_HEADS = 4
GLA_DK_HEAD = 128
GLA_DV_HEAD = 512
GLA_DK = GLA_HEADS * GLA_DK_HEAD
GLA_GATE_RANK = 16
GLA_GATE_TAU = 16.0
GLA_CHUNK = 64
GLA_SUB = 16
GLA_CUMSUM_ROWS = 256
GLA_OUT_ROWS = 256
DIL_GROUPS = ((128, 1), (512, 4), (2048, 16))
N_GROUPS = 3
DIL_HEADS = 16
DIL_HEAD_DIM = 128
DIL_BLOCK = 128
ALIBI_MAX_EXP = 8.0
MASK_DISTANCE = 1e34
PERM_ROWS = 256
LOG2_E = 1.4426950408889634
PROJ_COLS = 256
PROJ_LOOKAHEAD = 2
SCORE_LOOKAHEAD = 8
ATTN_BLOCKS_PER_STEP = 4
COMBINE_SUB_ROWS = 128

LANES = 128
VMEM_LIMIT = 56 * 1024 * 1024

_NT = (((1,), (1,)), ((), ()))


def _deinterleave_matrix(d, transpose=False):
    n = PERM_ROWS // d
    p = lax.broadcasted_iota(jnp.int32, (PERM_ROWS, PERM_ROWS), 1 if transpose else 0)
    j = lax.broadcasted_iota(jnp.int32, (PERM_ROWS, PERM_ROWS), 0 if transpose else 1)
    return (j == (p % n) * d + p // n).astype(BF16)


def _norm_proj_kernel(var_ref, x_ref, g_ref, w_ref, *rest, mm_variants, emit_variants, has_aux):
    rest = list(rest)
    waux_ref = rest.pop(0) if has_aux else None
    o_ref = rest.pop(0)
    aux_ref = rest.pop(0) if has_aux else None
    hv_ref = rest.pop(0) if emit_variants else None
    (h_ref,) = rest
    n = pl.program_id(1)
    tm = x_ref.shape[0]

    @pl.when(n == 0)
    def _():
        x = x_ref[...]
        y = x * lax.rsqrt(jnp.mean(x * x, axis=-1, keepdims=True) + RMS_EPS)
        targets = ([(h_ref, i, v) for i, v in enumerate(mm_variants)]
                   + [(hv_ref, i, v) for i, v in enumerate(emit_variants)])
        for gi in sorted({g for _, _, (g, _) in targets}):
            h = (y * g_ref[gi:gi + 1, :]).astype(BF16)
            for ref, i, (g, d) in targets:
                if g != gi:
                    continue
                if d == 1:
                    ref[i] = h
                else:
                    perm = _deinterleave_matrix(d)
                    for b in range(tm // PERM_ROWS):
                        rows = slice(b * PERM_ROWS, (b + 1) * PERM_ROWS)
                        ref[i, rows, :] = jnp.dot(perm, h[rows], preferred_element_type=F32).astype(BF16)
        if has_aux:
            aux_ref[...] = jnp.dot(h_ref[0], waux_ref[...], preferred_element_type=F32)

    h = h_ref[var_ref[n]]
    o_ref[...] = jnp.dot(h, w_ref[...], preferred_element_type=F32).astype(o_ref.dtype)


def _norm_proj(x, gains, mm_variants, var_of_block, w, w_aux, emit_variants=(), *, n_out, w_col0=0, tm, tn):
    S, D = x.shape
    N = n_out
    n_gains = gains.shape[0]
    has_aux = w_aux is not None
    grid = (S // tm, N // tn)
    w_blk0 = w_col0 // tn
    in_specs = [
        pl.BlockSpec((tm, D), lambda m, n, var: (m, 0)),
        pl.BlockSpec((n_gains, D), lambda m, n, var: (0, 0)),
        pl.BlockSpec((D, tn), lambda m, n, var: (0, w_blk0 + n)),
    ]
    out_shape = [jax.ShapeDtypeStruct((S, N), BF16)]
    out_specs = [pl.BlockSpec((tm, tn), lambda m, n, var: (m, n))]
    args = [x, gains, w]
    if has_aux:
        in_specs.append(pl.BlockSpec((D, LANES), lambda m, n, var: (0, 0)))
        out_shape.append(jax.ShapeDtypeStruct((S, LANES), F32))
        out_specs.append(pl.BlockSpec((tm, LANES), lambda m, n, var: (m, 0)))
        args.append(w_aux)
    if emit_variants:
        out_shape.append(jax.ShapeDtypeStruct((len(emit_variants), S, D), BF16))
        out_specs.append(pl.BlockSpec((len(emit_variants), tm, D), lambda m, n, var: (0, m, 0)))
    return pl.pallas_call(
        functools.partial(_norm_proj_kernel, mm_variants=tuple(mm_variants), emit_variants=tuple(emit_variants),
                          has_aux=has_aux),
        grid_spec=pltpu.PrefetchScalarGridSpec(
            num_scalar_prefetch=1, grid=grid, in_specs=in_specs, out_specs=out_specs,
            scratch_shapes=[pltpu.VMEM((len(mm_variants), tm, D), BF16)]),
        out_shape=out_shape,
        compiler_params=pltpu.CompilerParams(
            dimension_semantics=("arbitrary", "arbitrary"), vmem_limit_bytes=VMEM_LIMIT,
            allow_input_fusion=[False, False, False, True] + [True] * has_aux),
        name="norm_proj_aux" if has_aux else "norm_proj",
    )(var_of_block, *args)


def _split3(a):
    hi = a.astype(BF16)
    r1 = a - hi.astype(F32)
    mid = r1.astype(BF16)
    lo = (r1 - mid.astype(F32)).astype(BF16)
    return hi, mid, lo


def _gla_scores(b_ref, q_ref, k_ref, row0, v, state):
    C, SUB, HALF = GLA_CHUNK, GLA_SUB, GLA_SUB // 2
    rows = pl.ds(row0, C)
    b, q, k = b_ref[rows, :], q_ref[rows, :], k_ref[rows, :]
    dk = q.shape[1]
    b_last = b_ref[pl.ds(row0 + C - 1, 1), :]
    o_inter = jnp.dot((q * jnp.exp2(b)).astype(BF16), state.astype(BF16), preferred_element_type=F32)
    k_dec = k * jnp.exp2(b_last - b)
    upd = jnp.dot(k_dec.T.astype(BF16), v, preferred_element_type=F32)
    e_col = jnp.broadcast_to(jnp.exp2(b_last), (dk, dk)).T
    new_state = state * jnp.concatenate([e_col] * (v.shape[1] // dk), axis=1) + upd

    ones = jnp.ones((dk, LANES), BF16)
    diag_sums, offs = [], []
    for i in range(C // SUB):
        r0 = i * SUB
        halves = [(b[r0 + u * HALF:r0 + (u + 1) * HALF], q[r0 + u * HALF:r0 + (u + 1) * HALF]) for u in range(2)]
        prods = []
        for s in range(SUB):
            bs = b_ref[pl.ds(row0 + r0 + s, 1), :]
            ks = k_ref[pl.ds(row0 + r0 + s, 1), :]
            for u in range(s // HALF, 2):
                prods.append(halves[u][1] * jnp.exp2(halves[u][0] - bs) * ks)
        diag_sums.append(jnp.dot(jnp.concatenate(prods, axis=0).astype(BF16), ones, preferred_element_type=F32))
        if i > 0:
            b0 = b_ref[pl.ds(row0 + r0, 1), :]
            qt = (q[r0:r0 + SUB] * jnp.exp2(b[r0:r0 + SUB] - b0)).astype(BF16)
            kt = jnp.concatenate([(k[:r0] * jnp.exp2(b0 - b[:r0])).astype(BF16), jnp.zeros((C - r0, dk), BF16)], axis=0)
            offs.append(lax.dot_general(qt, kt, _NT, preferred_element_type=F32))
    return o_inter, new_state, diag_sums, offs


def _gla_output(o_inter, diag_sums, offs, v):
    C, SUB, HALF = GLA_CHUNK, GLA_SUB, GLA_SUB // 2
    lane = lax.broadcasted_iota(jnp.int32, (HALF, LANES), 1)
    row = lax.broadcasted_iota(jnp.int32, (HALF, LANES), 0)
    a_rows = []
    for i in range(C // SUB):
        r0 = i * SUB
        acc = [jnp.zeros((HALF, LANES), F32) for _ in range(2)]
        piece = 0
        for s in range(SUB):
            for u in range(s // HALF, 2):
                acc[u] = jnp.where(lane == r0 + s, diag_sums[i][piece * HALF:(piece + 1) * HALF, :], acc[u])
                piece += 1
        a_i = jnp.concatenate([jnp.where(lane <= row + r0 + u * HALF, acc[u], 0.0) for u in range(2)], axis=0)[:, :C]
        a_rows.append(a_i + offs[i - 1] if i > 0 else a_i)
    a = jnp.concatenate(a_rows, axis=0).astype(BF16)
    return o_inter + jnp.dot(a, v, preferred_element_type=F32)


def _gla_kernel(q_ref, k_ref, v0_ref, v1_ref, z0_ref, z1_ref, lr_ref, x_ref, wgu_ref, bg_ref, gout_ref, wout_ref,
                o_ref, state_ref, y_ref, b_ref, q32_ref, k32_ref):
    tm = q_ref.shape[0]
    C = GLA_CHUNK
    heads_per_ref = v0_ref.shape[1] // GLA_DV_HEAD

    def head_cols(refs, h):
        u = h % heads_per_ref
        return refs[h // heads_per_ref], slice(u * GLA_DV_HEAD, (u + 1) * GLA_DV_HEAD)

    @pl.when(pl.program_id(0) == 0)
    def _():
        state_ref[...] = jnp.zeros_like(state_ref)

    pre = jnp.dot(lr_ref[...].astype(BF16), wgu_ref[...], preferred_element_type=F32) + bg_ref[...]
    log2_a = (jnp.minimum(pre, 0.0) - jnp.log1p(jnp.exp(-jnp.abs(pre)))) * (LOG2_E / GLA_GATE_TAU)
    tb = min(tm, GLA_CUMSUM_ROWS)
    r = lax.broadcasted_iota(jnp.int32, (tb, tb), 0)
    c = lax.broadcasted_iota(jnp.int32, (tb, tb), 1)
    tri = ((c <= r) & ((c // C) == (r // C))).astype(BF16)
    b_all = jnp.concatenate(
        [sum(jnp.dot(tri, piece, preferred_element_type=F32) for piece in _split3(log2_a[t * tb:(t + 1) * tb]))
         for t in range(tm // tb)], axis=0)
    for h in range(GLA_HEADS):
        kl = slice(h * GLA_DK_HEAD, (h + 1) * GLA_DK_HEAD)
        b_ref[h] = b_all[:, kl]
        q32_ref[h] = q_ref[:, kl].astype(F32) * (GLA_DK_HEAD ** -0.5)
        k32_ref[h] = k_ref[:, kl].astype(F32)

    gout = gout_ref[...]

    out_rows = GLA_OUT_ROWS
    for ci in range(tm // C):
        row0 = ci * C
        rows = slice(row0, row0 + C)
        vls = [slice(h * GLA_DV_HEAD, (h + 1) * GLA_DV_HEAD) for h in range(GLA_HEADS)]
        parts = []
        for h in range(GLA_HEADS):
            v_ref, vl = head_cols((v0_ref, v1_ref), h)
            o_inter, new_state, diag_sums, offs = _gla_scores(b_ref.at[h], q32_ref.at[h], k32_ref.at[h], row0,
                                                              v_ref[rows, vl], state_ref[h])
            state_ref[h] = new_state
            parts.append((o_inter, diag_sums, offs))
        for h in range(GLA_HEADS):
            v_ref, vl = head_cols((v0_ref, v1_ref), h)
            z_ref, zl = head_cols((z0_ref, z1_ref), h)
            o = _gla_output(*parts[h], v_ref[rows, vl])
            o = o * lax.rsqrt(jnp.mean(o * o, axis=-1, keepdims=True) + RMS_EPS) * gout
            z = z_ref[rows, zl].astype(F32)
            y_ref[rows, vls[h]] = (o * (z * jax.nn.sigmoid(z))).astype(BF16)
        if (row0 + C) % out_rows == 0:
            done = slice(row0 + C - out_rows, row0 + C)
            o_ref[done, :] = x_ref[done, :] + jnp.dot(y_ref[done, :], wout_ref[...], preferred_element_type=F32)


def _gla_layer(x, proj, lr, w_gate_up, b_gate, g_out, w_out, *, tm):
    S, D = x.shape
    half = D_INNER // 2
    v_blk, z_blk = 2 * GLA_DK // half, (2 * GLA_DK + D_INNER) // half
    return pl.pallas_call(
        _gla_kernel,
        grid=(S // tm,),
        in_specs=[
            pl.BlockSpec((tm, GLA_DK), lambda m: (m, 0)),
            pl.BlockSpec((tm, GLA_DK), lambda m: (m, 1)),
            pl.BlockSpec((tm, half), lambda m: (m, v_blk)),
            pl.BlockSpec((tm, half), lambda m: (m, v_blk + 1)),
            pl.BlockSpec((tm, half), lambda m: (m, z_blk)),
            pl.BlockSpec((tm, half), lambda m: (m, z_blk + 1)),
            pl.BlockSpec((tm, LANES), lambda m: (m, 0)),
            pl.BlockSpec((tm, D), lambda m: (m, 0)),
            pl.BlockSpec((LANES, GLA_DK), lambda m: (0, 0)),
            pl.BlockSpec((1, GLA_DK), lambda m: (0, 0)),
            pl.BlockSpec((1, GLA_DV_HEAD), lambda m: (0, 0)),
            pl.BlockSpec((D_INNER, D), lambda m: (0, 0)),
        ],
        out_specs=pl.BlockSpec((tm, D), lambda m: (m, 0)),
        out_shape=jax.ShapeDtypeStruct((S, D), F32),
        scratch_shapes=[
            pltpu.VMEM((GLA_HEADS, GLA_DK_HEAD, GLA_DV_HEAD), F32),
            pltpu.VMEM((tm, D_INNER), BF16),
            pltpu.VMEM((GLA_HEADS, tm, GLA_DK_HEAD), F32),
            pltpu.VMEM((GLA_HEADS, tm, GLA_DK_HEAD), F32),
            pltpu.VMEM((GLA_HEADS, tm, GLA_DK_HEAD), F32),
        ],
        compiler_params=pltpu.CompilerParams(
            dimension_semantics=("arbitrary",), vmem_limit_bytes=VMEM_LIMIT),
        name="gla",
    )(proj, proj, proj, proj, proj, proj, lr, x, w_gate_up, b_gate, g_out, w_out)


def _block_rows(ref):
    return ref[...].reshape(-1, ref.shape[-1])


def _dil_attn_kernel(hq_ref, hkv_ref, wq_ref, wk_ref, wv_ref, o_ref, m_ref, l_ref,
                     q_s, k_s, v_s, bias_ref, m_acc, l_acc, *, slopes, dilation, qb):
    blk = DIL_BLOCK
    i = pl.program_id(1)
    new_rows = slice(blk, (qb + 1) * blk)

    @pl.when((pl.program_id(0) == 0) & (i == 0))
    def _():
        row = lax.broadcasted_iota(jnp.int32, (blk, blk), 0)
        col = lax.broadcasted_iota(jnp.int32, (blk, blk), 1)
        dist_prev = jnp.where(col >= row, ((row + blk - col) * dilation).astype(F32), MASK_DISTANCE)
        dist_cur = jnp.where(col <= row, ((row - col) * dilation).astype(F32), MASK_DISTANCE)
        for h in range(DIL_HEADS):
            bias_ref[h, :, :blk] = -(slopes[h] * LOG2_E) * dist_prev
            bias_ref[h, :, blk:] = -(slopes[h] * LOG2_E) * dist_cur

    def all_heads(first_step):
        hq = _block_rows(hq_ref)
        hkv = _block_rows(hkv_ref)
        heads_per_chunk = PROJ_COLS // DIL_HEAD_DIM
        n_chunks = D_INNER // PROJ_COLS
        projected = [0]

        def project_through(chunk):
            while projected[0] <= min(chunk, n_chunks - 1):
                cols = slice(projected[0] * PROJ_COLS, (projected[0] + 1) * PROJ_COLS)
                q_s[:, cols] = jnp.dot(hq, wq_ref[:, cols], preferred_element_type=F32).astype(BF16)
                k_s[new_rows, cols] = jnp.dot(hkv, wk_ref[:, cols], preferred_element_type=F32).astype(BF16)
                v_s[new_rows, cols] = jnp.dot(hkv, wv_ref[:, cols], preferred_element_type=F32).astype(BF16)
                projected[0] += 1

        def key_rows(j):
            return slice((j + 1) * blk, (j + 2) * blk) if first_step and j == 0 else slice(j * blk, (j + 2) * blk)

        def scores(unit):
            h, j = divmod(unit, qb)
            project_through(h // heads_per_chunk + PROJ_LOOKAHEAD)
            hl = slice(h * DIL_HEAD_DIM, (h + 1) * DIL_HEAD_DIM)
            keys = key_rows(j)
            bias = bias_ref[h, :, 2 * blk - (keys.stop - keys.start):]
            q = q_s[j * blk:(j + 1) * blk, hl]
            return lax.dot_general(q, k_s[keys, hl], _NT, preferred_element_type=F32) + bias

        n_units = DIL_HEADS * qb
        ahead = [scores(u) for u in range(SCORE_LOOKAHEAD)]
        for unit in range(n_units):
            h, j = divmod(unit, qb)
            hl = slice(h * DIL_HEAD_DIM, (h + 1) * DIL_HEAD_DIM)
            s = ahead.pop(0)
            if unit + SCORE_LOOKAHEAD < n_units:
                ahead.append(scores(unit + SCORE_LOOKAHEAD))
            m = jnp.max(s, axis=-1, keepdims=True)
            p = jnp.exp2(s - m)
            l = jnp.sum(p, axis=-1, keepdims=True)
            o = jnp.dot(p.astype(BF16), v_s[key_rows(j), hl], preferred_element_type=F32).astype(o_ref.dtype)
            if len(o_ref.shape) == 2:
                o_ref[j * blk:(j + 1) * blk, hl] = o
            else:
                pieces = o_ref.shape[0] // qb
                o_ref[j * pieces:(j + 1) * pieces, :, hl] = o.reshape(pieces, o_ref.shape[1], DIL_HEAD_DIM)
            m_acc[j * blk:(j + 1) * blk, h:h + 1] = m
            l_acc[j * blk:(j + 1) * blk, h:h + 1] = l

    m_acc[...] = jnp.zeros_like(m_acc)
    l_acc[...] = jnp.zeros_like(l_acc)

    @pl.when(i == 0)
    def _():
        all_heads(True)

    @pl.when(i > 0)
    def _():
        all_heads(False)

    m_ref[...] = m_acc[...].reshape(m_ref.shape)
    l_ref[...] = l_acc[...].reshape(l_ref.shape)
    k_s[:blk, :] = k_s[qb * blk:, :]
    v_s[:blk, :] = v_s[qb * blk:, :]


def _dil_attn_group(hv, w_kv, w_q, g, *, S, qb):
    _, d = DIL_GROUPS[g]
    D = hv.shape[-1]
    rows = qb * DIL_BLOCK
    n_steps = S // d // rows
    n_all = N_GROUPS * DIL_HEADS
    slopes = tuple(2.0 ** (-ALIBI_MAX_EXP * (g * DIL_HEADS + h + 1.0) / n_all) for h in range(DIL_HEADS))
    kcol, vcol, qcol = g, N_GROUPS + g, g
    if d == 1:
        h_spec = lambda v: pl.BlockSpec((None, rows, D), lambda r, i: (v, i, 0))
        spec = lambda width: pl.BlockSpec((rows, width), lambda r, i: (i, 0))
        shape = lambda width, dt: jax.ShapeDtypeStruct((S, width), dt)
    else:
        run = PERM_ROWS // d
        pieces = rows // run
        hv = hv.reshape(hv.shape[0], S // PERM_ROWS, d, run, D)
        h_spec = lambda v: pl.BlockSpec((None, pieces, None, run, D), lambda r, i: (v, i, r, 0, 0))
        spec = lambda width: pl.BlockSpec((pieces, None, run, width), lambda r, i: (i, r, 0, 0))
        shape = lambda width, dt: jax.ShapeDtypeStruct((S // PERM_ROWS, d, run, width), dt)
    w_spec = lambda col: pl.BlockSpec((D, D_INNER), lambda r, i: (0, col))
    o, m, l = pl.pallas_call(
        functools.partial(_dil_attn_kernel, slopes=slopes, dilation=d, qb=qb),
        grid=(d, n_steps),
        in_specs=[h_spec(N_GROUPS + g), h_spec(g), w_spec(qcol), w_spec(kcol), w_spec(vcol)],
        out_specs=[spec(D_INNER), spec(LANES), spec(LANES)],
        out_shape=[shape(D_INNER, BF16), shape(LANES, F32), shape(LANES, F32)],
        scratch_shapes=[
            pltpu.VMEM((rows, D_INNER), BF16),
            pltpu.VMEM((rows + DIL_BLOCK, D_INNER), BF16),
            pltpu.VMEM((rows + DIL_BLOCK, D_INNER), BF16),
            pltpu.VMEM((DIL_HEADS, DIL_BLOCK, 2 * DIL_BLOCK), F32),
            pltpu.VMEM((rows, LANES), F32),
            pltpu.VMEM((rows, LANES), F32),
        ],
        compiler_params=pltpu.CompilerParams(
            dimension_semantics=("arbitrary", "arbitrary"), vmem_limit_bytes=VMEM_LIMIT,
            allow_input_fusion=[False, False, True, True, True]),
        name=f"dil_attn_g{g}",
    )(hv, hv, w_q, w_kv, w_kv)
    return o.reshape(S, D_INNER), m.reshape(S, LANES), l.reshape(S, LANES)


def _combine_kernel(o0_ref, o1_ref, o2_ref, m0_ref, m1_ref, m2_ref, l0_ref, l1_ref, l2_ref, z_ref, x_ref,
                    wout_ref, g_ref, out_ref, y_ref, ot_ref):
    tm = x_ref.shape[0]
    o_refs = (o0_ref, o1_ref, o2_ref)
    m_refs = (m0_ref, m1_ref, m2_ref)
    l_refs = (l0_ref, l1_ref, l2_ref)
    unperms = [None] + [_deinterleave_matrix(d, transpose=True) for _, d in DIL_GROUPS[1:]]

    def to_token_order(b):
        rows = slice(b * PERM_ROWS, (b + 1) * PERM_ROWS)
        ms, ls = [m_refs[0][rows, :]], [l_refs[0][rows, :]]
        for g in range(1, N_GROUPS):
            ot_ref[g - 1, rows, :] = jnp.dot(unperms[g], o_refs[g][rows, :], preferred_element_type=F32)
            for stats, ref in ((ms, m_refs[g]), (ls, l_refs[g])):
                stats.append(sum(jnp.dot(unperms[g], piece, preferred_element_type=F32)
                                 for piece in _split3(ref[rows, :])))
        return ms, ls

    def mix_and_project(rows, ms, ls):
        m = jnp.maximum(jnp.maximum(ms[0], ms[1]), ms[2])
        es = [jnp.exp2(mg - m) for mg in ms]
        den = es[0] * ls[0] + es[1] * ls[1] + es[2] * ls[2]
        lane = lax.broadcasted_iota(jnp.int32, m.shape, 1)
        ws = [jnp.where(lane < DIL_HEADS, e / den, 0.0) for e in es]
        for h in range(DIL_HEADS):
            hl = slice(h * DIL_HEAD_DIM, (h + 1) * DIL_HEAD_DIM)
            acc = ws[0][:, h:h + 1] * o_refs[0][rows, hl].astype(F32)
            for g in range(1, N_GROUPS):
                acc = acc + ws[g][:, h:h + 1] * ot_ref[g - 1, rows, hl]
            z = z_ref[rows, hl].astype(F32)
            y_ref[rows, hl] = (acc * (z * jax.nn.sigmoid(z))).astype(BF16)
        x = x_ref[rows, :] + jnp.dot(y_ref[rows, :], wout_ref[...], preferred_element_type=F32)
        y = x * lax.rsqrt(jnp.mean(x * x, axis=-1, keepdims=True) + RMS_EPS)
        out_ref[rows, :] = y * g_ref[...]

    n_blocks = tm // PERM_ROWS
    sub = COMBINE_SUB_ROWS
    stats = to_token_order(0)
    for b in range(n_blocks):
        nxt = to_token_order(b + 1) if b + 1 < n_blocks else None
        for u in range(PERM_ROWS // sub):
            local = slice(u * sub, (u + 1) * sub)
            rows = slice(b * PERM_ROWS + u * sub, b * PERM_ROWS + (u + 1) * sub)
            mix_and_project(rows, [mg[local] for mg in stats[0]], [lg[local] for lg in stats[1]])
        stats = nxt


def _combine(os_, ms, ls, z, x, w_out, g_final, *, tm):
    S, D = x.shape
    row_blk = lambda m: (m, 0)
    return pl.pallas_call(
        _combine_kernel,
        grid=(S // tm,),
        in_specs=[pl.BlockSpec((tm, D_INNER), row_blk)] * 3 + [pl.BlockSpec((tm, LANES), row_blk)] * 6 + [
            pl.BlockSpec((tm, D_INNER), row_blk),
            pl.BlockSpec((tm, D), row_blk),
            pl.BlockSpec((D_INNER, D), lambda m: (0, 0)),
            pl.BlockSpec((1, D), lambda m: (0, 0)),
        ],
        out_specs=pl.BlockSpec((tm, D), row_blk),
        out_shape=jax.ShapeDtypeStruct((S, D), F32),
        scratch_shapes=[pltpu.VMEM((tm, D_INNER), BF16), pltpu.VMEM((N_GROUPS - 1, tm, D_INNER), F32)],
        compiler_params=pltpu.CompilerParams(
            dimension_semantics=("arbitrary",), vmem_limit_bytes=VMEM_LIMIT),
        name="combine",
    )(*os_, *ms, *ls, z, x, w_out, g_final)


def kernel(x, a_norm, a_w_in, a_w_gate_up, a_b_gate, a_g_out, a_w_out, kv_norm, w_kv, b_norm, b_w_in, b_w_out,
           final_norm):
    B, S, D = x.shape
    assert B == 1 and D == D_MODEL and a_norm.shape[0] == 1 and b_norm.shape[0] == 1
    assert S % (ATTN_BLOCKS_PER_STEP * DIL_BLOCK * max(d for _, d in DIL_GROUPS)) == 0
    x0 = x.reshape(S, D)

    w_a = a_w_in[0].astype(BF16)
    n_a = 2 * GLA_DK + 2 * D_INNER
    w_lr = jnp.pad(w_a[:, n_a:], ((0, 0), (0, LANES - GLA_GATE_RANK)))
    proj_a, lr = _norm_proj(x0, a_norm, [(0, 1)], jnp.zeros((2,), jnp.int32), w_a, w_lr, n_out=n_a, tm=1024,
                            tn=n_a // 2)
    w_gu = jnp.pad(a_w_gate_up[0], ((0, LANES - GLA_GATE_RANK), (0, 0))).astype(BF16)
    x1 = _gla_layer(x0, proj_a, lr, w_gu, a_b_gate, a_g_out, a_w_out[0].astype(BF16), tm=512)

    w_kv_b, w_qz = w_kv.astype(BF16), b_w_in[0].astype(BF16)
    q_scale = DIL_HEAD_DIM ** -0.5 * LOG2_E
    gains = jnp.stack([kv_norm, b_norm[0] * q_scale, b_norm[0]])
    dils = [d for _, d in DIL_GROUPS]
    emit = [(0, d) for d in dils] + [(1, d) for d in dils]
    z, hv = _norm_proj(x1, gains, [(2, 1)], jnp.zeros((2,), jnp.int32), w_qz, None, emit, n_out=D_INNER,
                       w_col0=N_GROUPS * D_INNER, tm=1024, tn=D_INNER // 2)

    outs = [_dil_attn_group(hv, w_kv_b, w_qz, g, S=S, qb=ATTN_BLOCKS_PER_STEP) for g in range(N_GROUPS)]
    out = _combine([o for o, _, _ in outs], [m for _, m, _ in outs], [l for _, _, l in outs], z, x1,
                   b_w_out[0].astype(BF16), final_norm.reshape(1, D), tm=512)
    return out.reshape(B, S, D)
```

```python
import functools

import jax
import jax.numpy as jnp
from jax import lax
from jax.experimental import pallas as pl
from jax.experimental.pallas import tpu as pltpu

F32 = jnp.float32
BF16 = jnp.bfloat16

RMS_EPS = 1e-6
D_MODEL = 1024
D_INNER = 2048
GLA_HEADS = 4
GLA_DK_HEAD = 128
GLA_DV_HEAD = 512
GLA_DK = GLA_HEADS * GLA_DK_HEAD
GLA_GATE_RANK = 16
GLA_GATE_TAU = 16.0
GLA_CHUNK = 64
GLA_SUB = 16
GLA_CUMSUM_ROWS = 256
GLA_OUT_ROWS = 256
DIL_GROUPS = ((128, 1), (512, 4), (2048, 16))
N_GROUPS = 3
DIL_HEADS = 16
DIL_HEAD_DIM = 128
DIL_BLOCK = 128
ALIBI_MAX_EXP = 8.0
MASK_DISTANCE = 1e34
PERM_ROWS = 256
LOG2_E = 1.4426950408889634
PROJ_COLS = 256
PROJ_LOOKAHEAD = 2
SCORE_LOOKAHEAD = 8
ATTN_BLOCKS_PER_STEP = 4
COMBINE_SUB_ROWS = 128

LANES = 128
VMEM_LIMIT = 56 * 1024 * 1024

_NT = (((1,), (1,)), ((), ()))


def _scale_cast_kernel(sel_ref, w_ref, g_ref, o_ref):
    del sel_ref
    g = g_ref[...]
    for j in range(w_ref.shape[1] // LANES):
        cols = slice(j * LANES, (j + 1) * LANES)
        o_ref[:, cols] = (w_ref[:, cols] * g).astype(BF16)


def _scale_cast(w, gains, gain_of_block, *, tn):
    K, N = w.shape
    g_lanes = jnp.broadcast_to(gains[:, :, None], (gains.shape[0], K, LANES))
    return pl.pallas_call(
        _scale_cast_kernel,
        grid_spec=pltpu.PrefetchScalarGridSpec(
            num_scalar_prefetch=1, grid=(N // tn,),
            in_specs=[pl.BlockSpec((K, tn), lambda n, sel: (0, n)),
                      pl.BlockSpec((None, K, LANES), lambda n, sel: (sel[n], 0, 0))],
            out_specs=pl.BlockSpec((K, tn), lambda n, sel: (0, n))),
        out_shape=jax.ShapeDtypeStruct((K, N), BF16),
        compiler_params=pltpu.CompilerParams(dimension_semantics=("arbitrary",), vmem_limit_bytes=VMEM_LIMIT),
        name="scale_cast",
    )(gain_of_block, w, g_lanes)


def _deinterleave_matrix(d, transpose=False):
    n = PERM_ROWS // d
    p = lax.broadcasted_iota(jnp.int32, (PERM_ROWS, PERM_ROWS), 1 if transpose else 0)
    j = lax.broadcasted_iota(jnp.int32, (PERM_ROWS, PERM_ROWS), 0 if transpose else 1)
    return (j == (p % n) * d + p // n).astype(BF16)


def _norm_proj_kernel(var_ref, x_ref, g_ref, w_ref, *rest, mm_variants, emit_variants, has_aux):
    rest = list(rest)
    waux_ref = rest.pop(0) if has_aux else None
    o_ref = rest.pop(0)
    aux_ref = rest.pop(0) if has_aux else None
    hv_ref = rest.pop(0) if emit_variants else None
    (h_ref,) = rest
    n = pl.program_id(1)
    tm = x_ref.shape[0]

    @pl.when(n == 0)
    def _():
        x = x_ref[...]
        y = x * lax.rsqrt(jnp.mean(x * x, axis=-1, keepdims=True) + RMS_EPS)
        targets = ([(h_ref, i, v) for i, v in enumerate(mm_variants)]
                   + [(hv_ref, i, v) for i, v in enumerate(emit_variants)])
        for gi in sorted({g for _, _, (g, _) in targets}):
            h = (y if gi < 0 else y * g_ref[gi:gi + 1, :]).astype(BF16)
            for ref, i, (g, d) in targets:
                if g != gi:
                    continue
                if d == 1:
                    ref[i] = h
                else:
                    perm = _deinterleave_matrix(d)
                    for b in range(tm // PERM_ROWS):
                        rows = slice(b * PERM_ROWS, (b + 1) * PERM_ROWS)
                        ref[i, rows, :] = jnp.dot(perm, h[rows], preferred_element_type=F32).astype(BF16)
        if has_aux:
            aux_ref[...] = jnp.dot(h_ref[0], waux_ref[...], preferred_element_type=F32)

    h = h_ref[var_ref[n]]
    o_ref[...] = jnp.dot(h, w_ref[...], preferred_element_type=F32).astype(o_ref.dtype)


def _norm_proj(x, gains, mm_variants, var_of_block, w, w_aux, emit_variants=(), *, n_out, w_col0=0, tm, tn):
    S, D = x.shape
    N = n_out
    n_gains = gains.shape[0]
    has_aux = w_aux is not None
    grid = (S // tm, N // tn)
    w_blk0 = w_col0 // tn
    in_specs = [
        pl.BlockSpec((tm, D), lambda m, n, var: (m, 0)),
        pl.BlockSpec((n_gains, D), lambda m, n, var: (0, 0)),
        pl.BlockSpec((D, tn), lambda m, n, var: (0, w_blk0 + n)),
    ]
    out_shape = [jax.ShapeDtypeStruct((S, N), BF16)]
    out_specs = [pl.BlockSpec((tm, tn), lambda m, n, var: (m, n))]
    args = [x, gains, w]
    if has_aux:
        in_specs.append(pl.BlockSpec((D, LANES), lambda m, n, var: (0, 0)))
        out_shape.append(jax.ShapeDtypeStruct((S, LANES), F32))
        out_specs.append(pl.BlockSpec((tm, LANES), lambda m, n, var: (m, 0)))
        args.append(w_aux)
    if emit_variants:
        out_shape.append(jax.ShapeDtypeStruct((len(emit_variants), S, D), BF16))
        out_specs.append(pl.BlockSpec((len(emit_variants), tm, D), lambda m, n, var: (0, m, 0)))
    return pl.pallas_call(
        functools.partial(_norm_proj_kernel, mm_variants=tuple(mm_variants), emit_variants=tuple(emit_variants),
                          has_aux=has_aux),
        grid_spec=pltpu.PrefetchScalarGridSpec(
            num_scalar_prefetch=1, grid=grid, in_specs=in_specs, out_specs=out_specs,
            scratch_shapes=[pltpu.VMEM((len(mm_variants), tm, D), BF16)]),
        out_shape=out_shape,
        compiler_params=pltpu.CompilerParams(
            dimension_semantics=("arbitrary", "arbitrary"), vmem_limit_bytes=VMEM_LIMIT),
        name="norm_proj_aux" if has_aux else "norm_proj",
    )(var_of_block, *args)


def _split3(a):
    hi = a.astype(BF16)
    r1 = a - hi.astype(F32)
    mid = r1.astype(BF16)
    lo = (r1 - mid.astype(F32)).astype(BF16)
    return hi, mid, lo


def _gla_scores(b_ref, q_ref, k_ref, row0, v, state):
    C, SUB, HALF = GLA_CHUNK, GLA_SUB, GLA_SUB // 2
    rows = pl.ds(row0, C)
    b, q, k = b_ref[rows, :], q_ref[rows, :], k_ref[rows, :]
    dk = q.shape[1]
    b_last = b_ref[pl.ds(row0 + C - 1, 1), :]
    o_inter = jnp.dot((q * jnp.exp2(b)).astype(BF16), state.astype(BF16), preferred_element_type=F32)
    k_dec = k * jnp.exp2(b_last - b)
    upd = jnp.dot(k_dec.T.astype(BF16), v, preferred_element_type=F32)
    e_col = jnp.broadcast_to(jnp.exp2(b_last), (dk, dk)).T
    new_state = state * jnp.concatenate([e_col] * (v.shape[1] // dk), axis=1) + upd

    ones = jnp.ones((dk, LANES), BF16)
    diag_sums, offs = [], []
    for i in range(C // SUB):
        r0 = i * SUB
        halves = [(b[r0 + u * HALF:r0 + (u + 1) * HALF], q[r0 + u * HALF:r0 + (u + 1) * HALF]) for u in range(2)]
        prods = []
        for s in range(SUB):
            bs = b_ref[pl.ds(row0 + r0 + s, 1), :]
            ks = k_ref[pl.ds(row0 + r0 + s, 1), :]
            for u in range(s // HALF, 2):
                prods.append(halves[u][1] * jnp.exp2(halves[u][0] - bs) * ks)
        diag_sums.append(jnp.dot(jnp.concatenate(prods, axis=0).astype(BF16), ones, preferred_element_type=F32))
        if i > 0:
            b0 = b_ref[pl.ds(row0 + r0, 1), :]
            qt = (q[r0:r0 + SUB] * jnp.exp2(b[r0:r0 + SUB] - b0)).astype(BF16)
            kt = jnp.concatenate([(k[:r0] * jnp.exp2(b0 - b[:r0])).astype(BF16), jnp.zeros((C - r0, dk), BF16)], axis=0)
            offs.append(lax.dot_general(qt, kt, _NT, preferred_element_type=F32))
    return o_inter, new_state, diag_sums, offs


def _gla_output(o_inter, diag_sums, offs, v):
    C, SUB, HALF = GLA_CHUNK, GLA_SUB, GLA_SUB // 2
    lane = lax.broadcasted_iota(jnp.int32, (HALF, LANES), 1)
    row = lax.broadcasted_iota(jnp.int32, (HALF, LANES), 0)
    a_rows = []
    for i in range(C // SUB):
        r0 = i * SUB
        acc = [jnp.zeros((HALF, LANES), F32) for _ in range(2)]
        piece = 0
        for s in range(SUB):
            for u in range(s // HALF, 2):
                acc[u] = jnp.where(lane == r0 + s, diag_sums[i][piece * HALF:(piece + 1) * HALF, :], acc[u])
                piece += 1
        a_i = jnp.concatenate([jnp.where(lane <= row + r0 + u * HALF, acc[u], 0.0) for u in range(2)], axis=0)[:, :C]
        a_rows.append(a_i + offs[i - 1] if i > 0 else a_i)
    a = jnp.concatenate(a_rows, axis=0).astype(BF16)
    return o_inter + jnp.dot(a, v, preferred_element_type=F32)


def _gla_kernel(q_ref, k_ref, v0_ref, v1_ref, z0_ref, z1_ref, lr_ref, x_ref, wgu_ref, bg_ref, gout_ref, wout_ref,
                o_ref, state_ref, y_ref, b_ref, q32_ref, k32_ref):
    tm = q_ref.shape[0]
    C = GLA_CHUNK
    heads_per_ref = v0_ref.shape[1] // GLA_DV_HEAD

    def head_cols(refs, h):
        u = h % heads_per_ref
        return refs[h // heads_per_ref], slice(u * GLA_DV_HEAD, (u + 1) * GLA_DV_HEAD)

    @pl.when(pl.program_id(0) == 0)
    def _():
        state_ref[...] = jnp.zeros_like(state_ref)

    pre = jnp.dot(lr_ref[...].astype(BF16), wgu_ref[...], preferred_element_type=F32) + bg_ref[...]
    log2_a = (jnp.minimum(pre, 0.0) - jnp.log1p(jnp.exp(-jnp.abs(pre)))) * (LOG2_E / GLA_GATE_TAU)
    tb = min(tm, GLA_CUMSUM_ROWS)
    r = lax.broadcasted_iota(jnp.int32, (tb, tb), 0)
    c = lax.broadcasted_iota(jnp.int32, (tb, tb), 1)
    tri = ((c <= r) & ((c // C) == (r // C))).astype(BF16)
    b_all = jnp.concatenate(
        [sum(jnp.dot(tri, piece, preferred_element_type=F32) for piece in _split3(log2_a[t * tb:(t + 1) * tb]))
         for t in range(tm // tb)], axis=0)
    for h in range(GLA_HEADS):
        kl = slice(h * GLA_DK_HEAD, (h + 1) * GLA_DK_HEAD)
        b_ref[h] = b_all[:, kl]
        q32_ref[h] = q_ref[:, kl].astype(F32) * (GLA_DK_HEAD ** -0.5)
        k32_ref[h] = k_ref[:, kl].astype(F32)

    gout = gout_ref[...]

    out_rows = GLA_OUT_ROWS
    for ci in range(tm // C):
        row0 = ci * C
        rows = slice(row0, row0 + C)
        vls = [slice(h * GLA_DV_HEAD, (h + 1) * GLA_DV_HEAD) for h in range(GLA_HEADS)]
        parts = []
        for h in range(GLA_HEADS):
            v_ref, vl = head_cols((v0_ref, v1_ref), h)
            o_inter, new_state, diag_sums, offs = _gla_scores(b_ref.at[h], q32_ref.at[h], k32_ref.at[h], row0,
                                                              v_ref[rows, vl], state_ref[h])
            state_ref[h] = new_state
            parts.append((o_inter, diag_sums, offs))
        for h in range(GLA_HEADS):
            v_ref, vl = head_cols((v0_ref, v1_ref), h)
            z_ref, zl = head_cols((z0_ref, z1_ref), h)
            o = _gla_output(*parts[h], v_ref[rows, vl])
            o = o * lax.rsqrt(jnp.mean(o * o, axis=-1, keepdims=True) + RMS_EPS) * gout
            z = z_ref[rows, zl].astype(F32)
            y_ref[rows, vls[h]] = (o * (z * jax.nn.sigmoid(z))).astype(BF16)
        if (row0 + C) % out_rows == 0:
            done = slice(row0 + C - out_rows, row0 + C)
            o_ref[done, :] = x_ref[done, :] + jnp.dot(y_ref[done, :], wout_ref[...], preferred_element_type=F32)


def _gla_layer(x, proj, lr, w_gate_up, b_gate, g_out, w_out, *, tm):
    S, D = x.shape
    half = D_INNER // 2
    v_blk, z_blk = 2 * GLA_DK // half, (2 * GLA_DK + D_INNER) // half
    return pl.pallas_call(
        _gla_kernel,
        grid=(S // tm,),
        in_specs=[
            pl.BlockSpec((tm, GLA_DK), lambda m: (m, 0)),
            pl.BlockSpec((tm, GLA_DK), lambda m: (m, 1)),
            pl.BlockSpec((tm, half), lambda m: (m, v_blk)),
            pl.BlockSpec((tm, half), lambda m: (m, v_blk + 1)),
            pl.BlockSpec((tm, half), lambda m: (m, z_blk)),
            pl.BlockSpec((tm, half), lambda m: (m, z_blk + 1)),
            pl.BlockSpec((tm, LANES), lambda m: (m, 0)),
            pl.BlockSpec((tm, D), lambda m: (m, 0)),
            pl.BlockSpec((LANES, GLA_DK), lambda m: (0, 0)),
            pl.BlockSpec((1, GLA_DK), lambda m: (0, 0)),
            pl.BlockSpec((1, GLA_DV_HEAD), lambda m: (0, 0)),
            pl.BlockSpec((D_INNER, D), lambda m: (0, 0)),
        ],
        out_specs=pl.BlockSpec((tm, D), lambda m: (m, 0)),
        out_shape=jax.ShapeDtypeStruct((S, D), F32),
        scratch_shapes=[
            pltpu.VMEM((GLA_HEADS, GLA_DK_HEAD, GLA_DV_HEAD), F32),
            pltpu.VMEM((tm, D_INNER), BF16),
            pltpu.VMEM((GLA_HEADS, tm, GLA_DK_HEAD), F32),
            pltpu.VMEM((GLA_HEADS, tm, GLA_DK_HEAD), F32),
            pltpu.VMEM((GLA_HEADS, tm, GLA_DK_HEAD), F32),
        ],
        compiler_params=pltpu.CompilerParams(
            dimension_semantics=("arbitrary",), vmem_limit_bytes=VMEM_LIMIT),
        name="gla",
    )(proj, proj, proj, proj, proj, proj, lr, x, w_gate_up, b_gate, g_out, w_out)


def _block_rows(ref):
    return ref[...].reshape(-1, ref.shape[-1])


def _dil_attn_kernel(h_ref, wq_ref, wk_ref, wv_ref, o_ref, m_ref, l_ref,
                     q_s, k_s, v_s, bias_ref, m_acc, l_acc, *, slopes, dilation, qb):
    blk = DIL_BLOCK
    i = pl.program_id(1)
    new_rows = slice(blk, (qb + 1) * blk)

    @pl.when((pl.program_id(0) == 0) & (i == 0))
    def _():
        row = lax.broadcasted_iota(jnp.int32, (blk, blk), 0)
        col = lax.broadcasted_iota(jnp.int32, (blk, blk), 1)
        dist_prev = jnp.where(col >= row, ((row + blk - col) * dilation).astype(F32), MASK_DISTANCE)
        dist_cur = jnp.where(col <= row, ((row - col) * dilation).astype(F32), MASK_DISTANCE)
        for h in range(DIL_HEADS):
            bias_ref[h, :, :blk] = -(slopes[h] * LOG2_E) * dist_prev
            bias_ref[h, :, blk:] = -(slopes[h] * LOG2_E) * dist_cur

    def all_heads(first_step):
        hq = hkv = _block_rows(h_ref)
        heads_per_chunk = PROJ_COLS // DIL_HEAD_DIM
        n_chunks = D_INNER // PROJ_COLS
        projected = [0]

        def project_through(chunk):
            while projected[0] <= min(chunk, n_chunks - 1):
                cols = slice(projected[0] * PROJ_COLS, (projected[0] + 1) * PROJ_COLS)
                q_s[:, cols] = jnp.dot(hq, wq_ref[:, cols], preferred_element_type=F32).astype(BF16)
                k_s[new_rows, cols] = jnp.dot(hkv, wk_ref[:, cols], preferred_element_type=F32).astype(BF16)
                v_s[new_rows, cols] = jnp.dot(hkv, wv_ref[:, cols], preferred_element_type=F32).astype(BF16)
                projected[0] += 1

        def key_rows(j):
            return slice((j + 1) * blk, (j + 2) * blk) if first_step and j == 0 else slice(j * blk, (j + 2) * blk)

        def scores(unit):
            h, j = divmod(unit, qb)
            project_through(h // heads_per_chunk + PROJ_LOOKAHEAD)
            hl = slice(h * DIL_HEAD_DIM, (h + 1) * DIL_HEAD_DIM)
            keys = key_rows(j)
            bias = bias_ref[h, :, 2 * blk - (keys.stop - keys.start):]
            q = q_s[j * blk:(j + 1) * blk, hl]
            return lax.dot_general(q, k_s[keys, hl], _NT, preferred_element_type=F32) + bias

        n_units = DIL_HEADS * qb
        ahead = [scores(u) for u in range(SCORE_LOOKAHEAD)]
        for unit in range(n_units):
            h, j = divmod(unit, qb)
            hl = slice(h * DIL_HEAD_DIM, (h + 1) * DIL_HEAD_DIM)
            s = ahead.pop(0)
            if unit + SCORE_LOOKAHEAD < n_units:
                ahead.append(scores(unit + SCORE_LOOKAHEAD))
            m = jnp.max(s, axis=-1, keepdims=True)
            p = jnp.exp2(s - m)
            l = jnp.sum(p, axis=-1, keepdims=True)
            o = jnp.dot(p.astype(BF16), v_s[key_rows(j), hl], preferred_element_type=F32).astype(o_ref.dtype)
            if len(o_ref.shape) == 2:
                o_ref[j * blk:(j + 1) * blk, hl] = o
            else:
                pieces = o_ref.shape[0] // qb
                o_ref[j * pieces:(j + 1) * pieces, :, hl] = o.reshape(pieces, o_ref.shape[1], DIL_HEAD_DIM)
            m_acc[j * blk:(j + 1) * blk, h:h + 1] = m
            l_acc[j * blk:(j + 1) * blk, h:h + 1] = l

    m_acc[...] = jnp.zeros_like(m_acc)
    l_acc[...] = jnp.zeros_like(l_acc)

    @pl.when(i == 0)
    def _():
        all_heads(True)

    @pl.when(i > 0)
    def _():
        all_heads(False)

    m_ref[...] = m_acc[...].reshape(m_ref.shape)
    l_ref[...] = l_acc[...].reshape(l_ref.shape)
    k_s[:blk, :] = k_s[qb * blk:, :]
    v_s[:blk, :] = v_s[qb * blk:, :]


def _dil_attn_group(hv, w_kv, w_q, g, *, S, qb):
    _, d = DIL_GROUPS[g]
    D = hv.shape[-1]
    rows = qb * DIL_BLOCK
    n_steps = S // d // rows
    n_all = N_GROUPS * DIL_HEADS
    slopes = tuple(2.0 ** (-ALIBI_MAX_EXP * (g * DIL_HEADS + h + 1.0) / n_all) for h in range(DIL_HEADS))
    kcol, vcol, qcol = g, N_GROUPS + g, g
    if d == 1:
        h_spec = lambda v: pl.BlockSpec((None, rows, D), lambda r, i: (v, i, 0))
        spec = lambda width: pl.BlockSpec((rows, width), lambda r, i: (i, 0))
        shape = lambda width, dt: jax.ShapeDtypeStruct((S, width), dt)
    else:
        run = PERM_ROWS // d
        pieces = rows // run
        hv = hv.reshape(hv.shape[0], S // PERM_ROWS, d, run, D)
        h_spec = lambda v: pl.BlockSpec((None, pieces, None, run, D), lambda r, i: (v, i, r, 0, 0))
        spec = lambda width: pl.BlockSpec((pieces, None, run, width), lambda r, i: (i, r, 0, 0))
        shape = lambda width, dt: jax.ShapeDtypeStruct((S // PERM_ROWS, d, run, width), dt)
    w_spec = lambda col: pl.BlockSpec((D, D_INNER), lambda r, i: (0, col))
    o, m, l = pl.pallas_call(
        functools.partial(_dil_attn_kernel, slopes=slopes, dilation=d, qb=qb),
        grid=(d, n_steps),
        in_specs=[h_spec(g), w_spec(qcol), w_spec(kcol), w_spec(vcol)],
        out_specs=[spec(D_INNER), spec(LANES), spec(LANES)],
        out_shape=[shape(D_INNER, BF16), shape(LANES, F32), shape(LANES, F32)],
        scratch_shapes=[
            pltpu.VMEM((rows, D_INNER), BF16),
            pltpu.VMEM((rows + DIL_BLOCK, D_INNER), BF16),
            pltpu.VMEM((rows + DIL_BLOCK, D_INNER), BF16),
            pltpu.VMEM((DIL_HEADS, DIL_BLOCK, 2 * DIL_BLOCK), F32),
            pltpu.VMEM((rows, LANES), F32),
            pltpu.VMEM((rows, LANES), F32),
        ],
        compiler_params=pltpu.CompilerParams(
            dimension_semantics=("arbitrary", "arbitrary"), vmem_limit_bytes=VMEM_LIMIT),
        name=f"dil_attn_g{g}",
    )(hv, w_q, w_kv, w_kv)
    return o.reshape(S, D_INNER), m.reshape(S, LANES), l.reshape(S, LANES)


def _combine_kernel(o0_ref, o1_ref, o2_ref, m0_ref, m1_ref, m2_ref, l0_ref, l1_ref, l2_ref, z_ref, x_ref,
                    wout_ref, g_ref, out_ref, y_ref, ot_ref):
    tm = x_ref.shape[0]
    o_refs = (o0_ref, o1_ref, o2_ref)
    m_refs = (m0_ref, m1_ref, m2_ref)
    l_refs = (l0_ref, l1_ref, l2_ref)
    unperms = [None] + [_deinterleave_matrix(d, transpose=True) for _, d in DIL_GROUPS[1:]]

    def to_token_order(b):
        rows = slice(b * PERM_ROWS, (b + 1) * PERM_ROWS)
        ms, ls = [m_refs[0][rows, :]], [l_refs[0][rows, :]]
        for g in range(1, N_GROUPS):
            ot_ref[g - 1, rows, :] = jnp.dot(unperms[g], o_refs[g][rows, :], preferred_element_type=F32)
            for stats, ref in ((ms, m_refs[g]), (ls, l_refs[g])):
                stats.append(sum(jnp.dot(unperms[g], piece, preferred_element_type=F32)
                                 for piece in _split3(ref[rows, :])))
        return ms, ls

    def mix_and_project(rows, ms, ls):
        m = jnp.maximum(jnp.maximum(ms[0], ms[1]), ms[2])
        es = [jnp.exp2(mg - m) for mg in ms]
        den = es[0] * ls[0] + es[1] * ls[1] + es[2] * ls[2]
        lane = lax.broadcasted_iota(jnp.int32, m.shape, 1)
        ws = [jnp.where(lane < DIL_HEADS, e / den, 0.0) for e in es]
        for h in range(DIL_HEADS):
            hl = slice(h * DIL_HEAD_DIM, (h + 1) * DIL_HEAD_DIM)
            acc = ws[0][:, h:h + 1] * o_refs[0][rows, hl].astype(F32)
            for g in range(1, N_GROUPS):
                acc = acc + ws[g][:, h:h + 1] * ot_ref[g - 1, rows, hl]
            z = z_ref[rows, hl].astype(F32)
            y_ref[rows, hl] = (acc * (z * jax.nn.sigmoid(z))).astype(BF16)
        x = x_ref[rows, :] + jnp.dot(y_ref[rows, :], wout_ref[...], preferred_element_type=F32)
        y = x * lax.rsqrt(jnp.mean(x * x, axis=-1, keepdims=True) + RMS_EPS)
        out_ref[rows, :] = y * g_ref[...]

    n_blocks = tm // PERM_ROWS
    sub = COMBINE_SUB_ROWS
    stats = to_token_order(0)
    for b in range(n_blocks):
        nxt = to_token_order(b + 1) if b + 1 < n_blocks else None
        for u in range(PERM_ROWS // sub):
            local = slice(u * sub, (u + 1) * sub)
            rows = slice(b * PERM_ROWS + u * sub, b * PERM_ROWS + (u + 1) * sub)
            mix_and_project(rows, [mg[local] for mg in stats[0]], [lg[local] for lg in stats[1]])
        stats = nxt


def _combine(os_, ms, ls, z, x, w_out, g_final, *, tm):
    S, D = x.shape
    row_blk = lambda m: (m, 0)
    return pl.pallas_call(
        _combine_kernel,
        grid=(S // tm,),
        in_specs=[pl.BlockSpec((tm, D_INNER), row_blk)] * 3 + [pl.BlockSpec((tm, LANES), row_blk)] * 6 + [
            pl.BlockSpec((tm, D_INNER), row_blk),
            pl.BlockSpec((tm, D), row_blk),
            pl.BlockSpec((D_INNER, D), lambda m: (0, 0)),
            pl.BlockSpec((1, D), lambda m: (0, 0)),
        ],
        out_specs=pl.BlockSpec((tm, D), row_blk),
        out_shape=jax.ShapeDtypeStruct((S, D), F32),
        scratch_shapes=[pltpu.VMEM((tm, D_INNER), BF16), pltpu.VMEM((N_GROUPS - 1, tm, D_INNER), F32)],
        compiler_params=pltpu.CompilerParams(
            dimension_semantics=("arbitrary",), vmem_limit_bytes=VMEM_LIMIT),
        name="combine",
    )(*os_, *ms, *ls, z, x, w_out, g_final)


def kernel(x, a_norm, a_w_in, a_w_gate_up, a_b_gate, a_g_out, a_w_out, kv_norm, w_kv, b_norm, b_w_in, b_w_out,
           final_norm):
    B, S, D = x.shape
    assert B == 1 and D == D_MODEL and a_norm.shape[0] == 1 and b_norm.shape[0] == 1
    assert S % (ATTN_BLOCKS_PER_STEP * DIL_BLOCK * max(d for _, d in DIL_GROUPS)) == 0
    x0 = x.reshape(S, D)

    n_a = 2 * GLA_DK + 2 * D_INNER
    w_a = a_w_in[0, :, :n_a].astype(BF16)
    w_lr = jnp.pad(a_w_in[0, :, n_a:], ((0, 0), (0, LANES - GLA_GATE_RANK))).astype(BF16)
    proj_a, lr = _norm_proj(x0, a_norm, [(0, 1)], jnp.zeros((2,), jnp.int32), w_a, w_lr, n_out=n_a, tm=1024,
                            tn=n_a // 2)
    w_gu = jnp.pad(a_w_gate_up[0], ((0, LANES - GLA_GATE_RANK), (0, 0))).astype(BF16)
    x1 = _gla_layer(x0, proj_a, lr, w_gu, a_b_gate, a_g_out, a_w_out[0].astype(BF16), tm=512)

    q_scale = DIL_HEAD_DIM ** -0.5 * LOG2_E
    w_kv_b = _scale_cast(w_kv, kv_norm[None], jnp.zeros((2 * N_GROUPS,), jnp.int32), tn=D_INNER)
    w_qz = _scale_cast(b_w_in[0], jnp.stack([b_norm[0] * q_scale, b_norm[0]]),
                       jnp.array([0] * N_GROUPS + [1], jnp.int32), tn=D_INNER)
    variants = [(-1, d) for _, d in DIL_GROUPS]
    z, hv = _norm_proj(x1, jnp.ones((1, D), F32), variants[:1], jnp.zeros((2,), jnp.int32), w_qz, None, variants,
                       n_out=D_INNER, w_col0=N_GROUPS * D_INNER, tm=1024, tn=D_INNER // 2)

    outs = [_dil_attn_group(hv, w_kv_b, w_qz, g, S=S, qb=ATTN_BLOCKS_PER_STEP) for g in range(N_GROUPS)]
    out = _combine([o for o, _, _ in outs], [m for _, m, _ in outs], [l for _, _, l in outs], z, x1,
                   b_w_out[0].astype(BF16), final_norm.reshape(1, D), tm=512)
    return out.reshape(B, S, D)
```

```python
import functools

import jax
import jax.numpy as jnp
from jax import lax
from jax.experimental import pallas as pl
from jax.experimental.pallas import tpu as pltpu

F32 = jnp.float32
BF16 = jnp.bfloat16

RMS_EPS = 1e-6
D_MODEL = 1024
D_INNER = 2048
GLA_HEADS = 4
GLA_DK_HEAD = 128
GLA_DV_HEAD = 512
GLA_DK = GLA_HEADS * GLA_DK_HEAD
GLA_GATE_RANK = 16
GLA_GATE_TAU = 16.0
GLA_CHUNK = 64
GLA_SUB = 16
GLA_CUMSUM_ROWS = 256
GLA_OUT_ROWS = 256
DIL_GROUPS = ((128, 1), (512, 4), (2048, 16))
N_GROUPS = 3
DIL_HEADS = 16
DIL_HEAD_DIM = 128
DIL_BLOCK = 128
ALIBI_MAX_EXP = 8.0
MASK_DISTANCE = 1e34
PERM_ROWS = 256
LOG2_E = 1.4426950408889634
PROJ_COLS = 256
PROJ_LOOKAHEAD = 2
SCORE_LOOKAHEAD = 8
ATTN_BLOCKS_PER_STEP = 4
COMBINE_SUB_ROWS = 128

LANES = 128
VMEM_LIMIT = 56 * 1024 * 1024

_NT = (((1,), (1,)), ((), ()))


def _scale_cast_kernel(sel_ref, w_ref, g_ref, o_ref):
    del sel_ref
    g = g_ref[...]
    for j in range(w_ref.shape[1] // LANES):
        cols = slice(j * LANES, (j + 1) * LANES)
        o_ref[:, cols] = (w_ref[:, cols] * g).astype(BF16)


def _scale_cast(w, gains, gain_of_block, *, n_out, tn):
    K = w.shape[-2]
    lead = (None,) * (w.ndim - 2)
    g_lanes = jnp.broadcast_to(gains[:, :, None], (gains.shape[0], K, LANES))
    return pl.pallas_call(
        _scale_cast_kernel,
        grid_spec=pltpu.PrefetchScalarGridSpec(
            num_scalar_prefetch=1, grid=(n_out // tn,),
            in_specs=[pl.BlockSpec(lead + (K, tn), lambda n, sel: (0,) * len(lead) + (0, n)),
                      pl.BlockSpec((None, K, LANES), lambda n, sel: (sel[n], 0, 0))],
            out_specs=pl.BlockSpec((K, tn), lambda n, sel: (0, n))),
        out_shape=jax.ShapeDtypeStruct((K, n_out), BF16),
        compiler_params=pltpu.CompilerParams(dimension_semantics=("arbitrary",), vmem_limit_bytes=VMEM_LIMIT),
        name="scale_cast",
    )(gain_of_block, w, g_lanes)


def _deinterleave_matrix(d, transpose=False):
    n = PERM_ROWS // d
    p = lax.broadcasted_iota(jnp.int32, (PERM_ROWS, PERM_ROWS), 1 if transpose else 0)
    j = lax.broadcasted_iota(jnp.int32, (PERM_ROWS, PERM_ROWS), 0 if transpose else 1)
    return (j == (p % n) * d + p // n).astype(BF16)


def _norm_proj_kernel(var_ref, x_ref, g_ref, w_ref, *rest, mm_variants, emit_variants, has_aux):
    rest = list(rest)
    waux_ref = rest.pop(0) if has_aux else None
    o_ref = rest.pop(0)
    aux_ref = rest.pop(0) if has_aux else None
    hv_ref = rest.pop(0) if emit_variants else None
    (h_ref,) = rest
    n = pl.program_id(1)
    tm = x_ref.shape[0]

    @pl.when(n == 0)
    def _():
        x = x_ref[...]
        y = x * lax.rsqrt(jnp.mean(x * x, axis=-1, keepdims=True) + RMS_EPS)
        targets = ([(h_ref, i, v) for i, v in enumerate(mm_variants)]
                   + [(hv_ref, i, v) for i, v in enumerate(emit_variants)])
        for gi in sorted({g for _, _, (g, _) in targets}):
            h = (y if gi < 0 else y * g_ref[gi:gi + 1, :]).astype(BF16)
            for ref, i, (g, d) in targets:
                if g != gi:
                    continue
                if d == 1:
                    ref[i] = h
                else:
                    perm = _deinterleave_matrix(d)
                    for b in range(tm // PERM_ROWS):
                        rows = slice(b * PERM_ROWS, (b + 1) * PERM_ROWS)
                        ref[i, rows, :] = jnp.dot(perm, h[rows], preferred_element_type=F32).astype(BF16)
        if has_aux:
            aux_ref[...] = jnp.dot(h_ref[0], waux_ref[...], preferred_element_type=F32)

    h = h_ref[var_ref[n]]
    o_ref[...] = jnp.dot(h, w_ref[...], preferred_element_type=F32).astype(o_ref.dtype)


def _norm_proj(x, gains, mm_variants, var_of_block, w, w_aux, emit_variants=(), *, n_out, w_col0=0, tm, tn):
    S, D = x.shape
    N = n_out
    n_gains = gains.shape[0]
    has_aux = w_aux is not None
    grid = (S // tm, N // tn)
    w_blk0 = w_col0 // tn
    in_specs = [
        pl.BlockSpec((tm, D), lambda m, n, var: (m, 0)),
        pl.BlockSpec((n_gains, D), lambda m, n, var: (0, 0)),
        pl.BlockSpec((D, tn), lambda m, n, var: (0, w_blk0 + n)),
    ]
    out_shape = [jax.ShapeDtypeStruct((S, N), BF16)]
    out_specs = [pl.BlockSpec((tm, tn), lambda m, n, var: (m, n))]
    args = [x, gains, w]
    if has_aux:
        in_specs.append(pl.BlockSpec((D, LANES), lambda m, n, var: (0, 0)))
        out_shape.append(jax.ShapeDtypeStruct((S, LANES), F32))
        out_specs.append(pl.BlockSpec((tm, LANES), lambda m, n, var: (m, 0)))
        args.append(w_aux)
    if emit_variants:
        out_shape.append(jax.ShapeDtypeStruct((len(emit_variants), S, D), BF16))
        out_specs.append(pl.BlockSpec((len(emit_variants), tm, D), lambda m, n, var: (0, m, 0)))
    return pl.pallas_call(
        functools.partial(_norm_proj_kernel, mm_variants=tuple(mm_variants), emit_variants=tuple(emit_variants),
                          has_aux=has_aux),
        grid_spec=pltpu.PrefetchScalarGridSpec(
            num_scalar_prefetch=1, grid=grid, in_specs=in_specs, out_specs=out_specs,
            scratch_shapes=[pltpu.VMEM((len(mm_variants), tm, D), BF16)]),
        out_shape=out_shape,
        compiler_params=pltpu.CompilerParams(
            dimension_semantics=("arbitrary", "arbitrary"), vmem_limit_bytes=VMEM_LIMIT),
        name="norm_proj_aux" if has_aux else "norm_proj",
    )(var_of_block, *args)


def _split3(a):
    hi = a.astype(BF16)
    r1 = a - hi.astype(F32)
    mid = r1.astype(BF16)
    lo = (r1 - mid.astype(F32)).astype(BF16)
    return hi, mid, lo


def _gla_scores(b_ref, q_ref, k_ref, row0, v, state):
    C, SUB, HALF = GLA_CHUNK, GLA_SUB, GLA_SUB // 2
    rows = pl.ds(row0, C)
    b, q, k = b_ref[rows, :], q_ref[rows, :], k_ref[rows, :]
    dk = q.shape[1]
    b_last = b_ref[pl.ds(row0 + C - 1, 1), :]
    o_inter = jnp.dot((q * jnp.exp2(b)).astype(BF16), state.astype(BF16), preferred_element_type=F32)
    k_dec = k * jnp.exp2(b_last - b)
    upd = jnp.dot(k_dec.T.astype(BF16), v, preferred_element_type=F32)
    e_col = jnp.broadcast_to(jnp.exp2(b_last), (dk, dk)).T
    new_state = state * jnp.concatenate([e_col] * (v.shape[1] // dk), axis=1) + upd

    ones = jnp.ones((dk, LANES), BF16)
    diag_sums, offs = [], []
    for i in range(C // SUB):
        r0 = i * SUB
        halves = [(b[r0 + u * HALF:r0 + (u + 1) * HALF], q[r0 + u * HALF:r0 + (u + 1) * HALF]) for u in range(2)]
        prods = []
        for s in range(SUB):
            bs = b_ref[pl.ds(row0 + r0 + s, 1), :]
            ks = k_ref[pl.ds(row0 + r0 + s, 1), :]
            for u in range(s // HALF, 2):
                prods.append(halves[u][1] * jnp.exp2(halves[u][0] - bs) * ks)
        diag_sums.append(jnp.dot(jnp.concatenate(prods, axis=0).astype(BF16), ones, preferred_element_type=F32))
        if i > 0:
            b0 = b_ref[pl.ds(row0 + r0, 1), :]
            qt = (q[r0:r0 + SUB] * jnp.exp2(b[r0:r0 + SUB] - b0)).astype(BF16)
            kt = jnp.concatenate([(k[:r0] * jnp.exp2(b0 - b[:r0])).astype(BF16), jnp.zeros((C - r0, dk), BF16)], axis=0)
            offs.append(lax.dot_general(qt, kt, _NT, preferred_element_type=F32))
    return o_inter, new_state, diag_sums, offs


def _gla_output(o_inter, diag_sums, offs, v):
    C, SUB, HALF = GLA_CHUNK, GLA_SUB, GLA_SUB // 2
    lane = lax.broadcasted_iota(jnp.int32, (HALF, LANES), 1)
    row = lax.broadcasted_iota(jnp.int32, (HALF, LANES), 0)
    a_rows = []
    for i in range(C // SUB):
        r0 = i * SUB
        acc = [jnp.zeros((HALF, LANES), F32) for _ in range(2)]
        piece = 0
        for s in range(SUB):
            for u in range(s // HALF, 2):
                acc[u] = jnp.where(lane == r0 + s, diag_sums[i][piece * HALF:(piece + 1) * HALF, :], acc[u])
                piece += 1
        a_i = jnp.concatenate([jnp.where(lane <= row + r0 + u * HALF, acc[u], 0.0) for u in range(2)], axis=0)[:, :C]
        a_rows.append(a_i + offs[i - 1] if i > 0 else a_i)
    a = jnp.concatenate(a_rows, axis=0).astype(BF16)
    return o_inter + jnp.dot(a, v, preferred_element_type=F32)


def _gla_kernel(q_ref, k_ref, v0_ref, v1_ref, z0_ref, z1_ref, lr_ref, x_ref, wgu_ref, bg_ref, gout_ref, wout_ref,
                o_ref, state_ref, y_ref, b_ref, q32_ref, k32_ref):
    tm = q_ref.shape[0]
    C = GLA_CHUNK
    heads_per_ref = v0_ref.shape[1] // GLA_DV_HEAD

    def head_cols(refs, h):
        u = h % heads_per_ref
        return refs[h // heads_per_ref], slice(u * GLA_DV_HEAD, (u + 1) * GLA_DV_HEAD)

    @pl.when(pl.program_id(0) == 0)
    def _():
        state_ref[...] = jnp.zeros_like(state_ref)

    pre = jnp.dot(lr_ref[...].astype(BF16), wgu_ref[...], preferred_element_type=F32) + bg_ref[...]
    log2_a = (jnp.minimum(pre, 0.0) - jnp.log1p(jnp.exp(-jnp.abs(pre)))) * (LOG2_E / GLA_GATE_TAU)
    tb = min(tm, GLA_CUMSUM_ROWS)
    r = lax.broadcasted_iota(jnp.int32, (tb, tb), 0)
    c = lax.broadcasted_iota(jnp.int32, (tb, tb), 1)
    tri = ((c <= r) & ((c // C) == (r // C))).astype(BF16)
    b_all = jnp.concatenate(
        [sum(jnp.dot(tri, piece, preferred_element_type=F32) for piece in _split3(log2_a[t * tb:(t + 1) * tb]))
         for t in range(tm // tb)], axis=0)
    for h in range(GLA_HEADS):
        kl = slice(h * GLA_DK_HEAD, (h + 1) * GLA_DK_HEAD)
        b_ref[h] = b_all[:, kl]
        q32_ref[h] = q_ref[:, kl].astype(F32) * (GLA_DK_HEAD ** -0.5)
        k32_ref[h] = k_ref[:, kl].astype(F32)

    gout = gout_ref[...]

    out_rows = GLA_OUT_ROWS
    for ci in range(tm // C):
        row0 = ci * C
        rows = slice(row0, row0 + C)
        vls = [slice(h * GLA_DV_HEAD, (h + 1) * GLA_DV_HEAD) for h in range(GLA_HEADS)]
        parts = []
        for h in range(GLA_HEADS):
            v_ref, vl = head_cols((v0_ref, v1_ref), h)
            o_inter, new_state, diag_sums, offs = _gla_scores(b_ref.at[h], q32_ref.at[h], k32_ref.at[h], row0,
                                                              v_ref[rows, vl], state_ref[h])
            state_ref[h] = new_state
            parts.append((o_inter, diag_sums, offs))
        for h in range(GLA_HEADS):
            v_ref, vl = head_cols((v0_ref, v1_ref), h)
            z_ref, zl = head_cols((z0_ref, z1_ref), h)
            o = _gla_output(*parts[h], v_ref[rows, vl])
            o = o * lax.rsqrt(jnp.mean(o * o, axis=-1, keepdims=True) + RMS_EPS) * gout
            z = z_ref[rows, zl].astype(F32)
            y_ref[rows, vls[h]] = (o * (z * jax.nn.sigmoid(z))).astype(BF16)
        if (row0 + C) % out_rows == 0:
            done = slice(row0 + C - out_rows, row0 + C)
            o_ref[done, :] = x_ref[done, :] + jnp.dot(y_ref[done, :], wout_ref[...], preferred_element_type=F32)


def _gla_layer(x, proj, lr, w_gate_up, b_gate, g_out, w_out, *, tm):
    S, D = x.shape
    half = D_INNER // 2
    v_blk, z_blk = 2 * GLA_DK // half, (2 * GLA_DK + D_INNER) // half
    return pl.pallas_call(
        _gla_kernel,
        grid=(S // tm,),
        in_specs=[
            pl.BlockSpec((tm, GLA_DK), lambda m: (m, 0)),
            pl.BlockSpec((tm, GLA_DK), lambda m: (m, 1)),
            pl.BlockSpec((tm, half), lambda m: (m, v_blk)),
            pl.BlockSpec((tm, half), lambda m: (m, v_blk + 1)),
            pl.BlockSpec((tm, half), lambda m: (m, z_blk)),
            pl.BlockSpec((tm, half), lambda m: (m, z_blk + 1)),
            pl.BlockSpec((tm, LANES), lambda m: (m, 0)),
            pl.BlockSpec((tm, D), lambda m: (m, 0)),
            pl.BlockSpec((LANES, GLA_DK), lambda m: (0, 0)),
            pl.BlockSpec((1, GLA_DK), lambda m: (0, 0)),
            pl.BlockSpec((1, GLA_DV_HEAD), lambda m: (0, 0)),
            pl.BlockSpec((D_INNER, D), lambda m: (0, 0)),
        ],
        out_specs=pl.BlockSpec((tm, D), lambda m: (m, 0)),
        out_shape=jax.ShapeDtypeStruct((S, D), F32),
        scratch_shapes=[
            pltpu.VMEM((GLA_HEADS, GLA_DK_HEAD, GLA_DV_HEAD), F32),
            pltpu.VMEM((tm, D_INNER), BF16),
            pltpu.VMEM((GLA_HEADS, tm, GLA_DK_HEAD), F32),
            pltpu.VMEM((GLA_HEADS, tm, GLA_DK_HEAD), F32),
            pltpu.VMEM((GLA_HEADS, tm, GLA_DK_HEAD), F32),
        ],
        compiler_params=pltpu.CompilerParams(
            dimension_semantics=("arbitrary",), vmem_limit_bytes=VMEM_LIMIT),
        name="gla",
    )(proj, proj, proj, proj, proj, proj, lr, x, w_gate_up, b_gate, g_out, w_out)


def _block_rows(ref):
    return ref[...].reshape(-1, ref.shape[-1])


def _dil_attn_kernel(h_ref, wq_ref, wk_ref, wv_ref, o_ref, m_ref, l_ref,
                     q_s, k_s, v_s, bias_ref, m_acc, l_acc, *, slopes, dilation, qb):
    blk = DIL_BLOCK
    i = pl.program_id(1)
    new_rows = slice(blk, (qb + 1) * blk)

    @pl.when((pl.program_id(0) == 0) & (i == 0))
    def _():
        row = lax.broadcasted_iota(jnp.int32, (blk, blk), 0)
        col = lax.broadcasted_iota(jnp.int32, (blk, blk), 1)
        dist_prev = jnp.where(col >= row, ((row + blk - col) * dilation).astype(F32), MASK_DISTANCE)
        dist_cur = jnp.where(col <= row, ((row - col) * dilation).astype(F32), MASK_DISTANCE)
        for h in range(DIL_HEADS):
            bias_ref[h, :, :blk] = -(slopes[h] * LOG2_E) * dist_prev
            bias_ref[h, :, blk:] = -(slopes[h] * LOG2_E) * dist_cur

    def all_heads(first_step):
        hq = hkv = _block_rows(h_ref)
        heads_per_chunk = PROJ_COLS // DIL_HEAD_DIM
        n_chunks = D_INNER // PROJ_COLS
        projected = [0]

        def project_through(chunk):
            while projected[0] <= min(chunk, n_chunks - 1):
                cols = slice(projected[0] * PROJ_COLS, (projected[0] + 1) * PROJ_COLS)
                q_s[:, cols] = jnp.dot(hq, wq_ref[:, cols], preferred_element_type=F32).astype(BF16)
                k_s[new_rows, cols] = jnp.dot(hkv, wk_ref[:, cols], preferred_element_type=F32).astype(BF16)
                v_s[new_rows, cols] = jnp.dot(hkv, wv_ref[:, cols], preferred_element_type=F32).astype(BF16)
                projected[0] += 1

        def key_rows(j):
            return slice((j + 1) * blk, (j + 2) * blk) if first_step and j == 0 else slice(j * blk, (j + 2) * blk)

        def scores(unit):
            h, j = divmod(unit, qb)
            project_through(h // heads_per_chunk + PROJ_LOOKAHEAD)
            hl = slice(h * DIL_HEAD_DIM, (h + 1) * DIL_HEAD_DIM)
            keys = key_rows(j)
            bias = bias_ref[h, :, 2 * blk - (keys.stop - keys.start):]
            q = q_s[j * blk:(j + 1) * blk, hl]
            return lax.dot_general(q, k_s[keys, hl], _NT, preferred_element_type=F32) + bias

        n_units = DIL_HEADS * qb
        ahead = [scores(u) for u in range(SCORE_LOOKAHEAD)]
        for unit in range(n_units):
            h, j = divmod(unit, qb)
            hl = slice(h * DIL_HEAD_DIM, (h + 1) * DIL_HEAD_DIM)
            s = ahead.pop(0)
            if unit + SCORE_LOOKAHEAD < n_units:
                ahead.append(scores(unit + SCORE_LOOKAHEAD))
            m = jnp.max(s, axis=-1, keepdims=True)
            p = jnp.exp2(s - m)
            l = jnp.sum(p, axis=-1, keepdims=True)
            o = jnp.dot(p.astype(BF16), v_s[key_rows(j), hl], preferred_element_type=F32).astype(o_ref.dtype)
            if len(o_ref.shape) == 2:
                o_ref[j * blk:(j + 1) * blk, hl] = o
            else:
                pieces = o_ref.shape[0] // qb
                o_ref[j * pieces:(j + 1) * pieces, :, hl] = o.reshape(pieces, o_ref.shape[1], DIL_HEAD_DIM)
            m_acc[j * blk:(j + 1) * blk, h:h + 1] = m
            l_acc[j * blk:(j + 1) * blk, h:h + 1] = l

    m_acc[...] = jnp.zeros_like(m_acc)
    l_acc[...] = jnp.zeros_like(l_acc)

    @pl.when(i == 0)
    def _():
        all_heads(True)

    @pl.when(i > 0)
    def _():
        all_heads(False)

    m_ref[...] = m_acc[...].reshape(m_ref.shape)
    l_ref[...] = l_acc[...].reshape(l_ref.shape)
    k_s[:blk, :] = k_s[qb * blk:, :]
    v_s[:blk, :] = v_s[qb * blk:, :]


def _dil_attn_group(hv, w_kv, w_q, g, *, S, qb):
    _, d = DIL_GROUPS[g]
    D = hv.shape[-1]
    rows = qb * DIL_BLOCK
    n_steps = S // d // rows
    n_all = N_GROUPS * DIL_HEADS
    slopes = tuple(2.0 ** (-ALIBI_MAX_EXP * (g * DIL_HEADS + h + 1.0) / n_all) for h in range(DIL_HEADS))
    kcol, vcol, qcol = g, N_GROUPS + g, g
    if d == 1:
        h_spec = lambda v: pl.BlockSpec((None, rows, D), lambda r, i: (v, i, 0))
        spec = lambda width: pl.BlockSpec((rows, width), lambda r, i: (i, 0))
        shape = lambda width, dt: jax.ShapeDtypeStruct((S, width), dt)
    else:
        run = PERM_ROWS // d
        pieces = rows // run
        hv = hv.reshape(hv.shape[0], S // PERM_ROWS, d, run, D)
        h_spec = lambda v: pl.BlockSpec((None, pieces, None, run, D), lambda r, i: (v, i, r, 0, 0))
        spec = lambda width: pl.BlockSpec((pieces, None, run, width), lambda r, i: (i, r, 0, 0))
        shape = lambda width, dt: jax.ShapeDtypeStruct((S // PERM_ROWS, d, run, width), dt)
    w_spec = lambda col: pl.BlockSpec((D, D_INNER), lambda r, i: (0, col))
    o, m, l = pl.pallas_call(
        functools.partial(_dil_attn_kernel, slopes=slopes, dilation=d, qb=qb),
        grid=(d, n_steps),
        in_specs=[h_spec(g), w_spec(qcol), w_spec(kcol), w_spec(vcol)],
        out_specs=[spec(D_INNER), spec(LANES), spec(LANES)],
        out_shape=[shape(D_INNER, BF16), shape(LANES, F32), shape(LANES, F32)],
        scratch_shapes=[
            pltpu.VMEM((rows, D_INNER), BF16),
            pltpu.VMEM((rows + DIL_BLOCK, D_INNER), BF16),
            pltpu.VMEM((rows + DIL_BLOCK, D_INNER), BF16),
            pltpu.VMEM((DIL_HEADS, DIL_BLOCK, 2 * DIL_BLOCK), F32),
            pltpu.VMEM((rows, LANES), F32),
            pltpu.VMEM((rows, LANES), F32),
        ],
        compiler_params=pltpu.CompilerParams(
            dimension_semantics=("arbitrary", "arbitrary"), vmem_limit_bytes=VMEM_LIMIT),
        name=f"dil_attn_g{g}",
    )(hv, w_q, w_kv, w_kv)
    return o.reshape(S, D_INNER), m.reshape(S, LANES), l.reshape(S, LANES)


def _combine_kernel(o0_ref, o1_ref, o2_ref, m0_ref, m1_ref, m2_ref, l0_ref, l1_ref, l2_ref, z_ref, x_ref,
                    wout_ref, g_ref, out_ref, y_ref, ot_ref):
    tm = x_ref.shape[0]
    o_refs = (o0_ref, o1_ref, o2_ref)
    m_refs = (m0_ref, m1_ref, m2_ref)
    l_refs = (l0_ref, l1_ref, l2_ref)
    unperms = [None] + [_deinterleave_matrix(d, transpose=True) for _, d in DIL_GROUPS[1:]]

    def to_token_order(b):
        rows = slice(b * PERM_ROWS, (b + 1) * PERM_ROWS)
        ms, ls = [m_refs[0][rows, :]], [l_refs[0][rows, :]]
        for g in range(1, N_GROUPS):
            ot_ref[g - 1, rows, :] = jnp.dot(unperms[g], o_refs[g][rows, :], preferred_element_type=F32)
            for stats, ref in ((ms, m_refs[g]), (ls, l_refs[g])):
                stats.append(sum(jnp.dot(unperms[g], piece, preferred_element_type=F32)
                                 for piece in _split3(ref[rows, :])))
        return ms, ls

    def mix_and_project(rows, ms, ls):
        m = jnp.maximum(jnp.maximum(ms[0], ms[1]), ms[2])
        es = [jnp.exp2(mg - m) for mg in ms]
        den = es[0] * ls[0] + es[1] * ls[1] + es[2] * ls[2]
        lane = lax.broadcasted_iota(jnp.int32, m.shape, 1)
        ws = [jnp.where(lane < DIL_HEADS, e / den, 0.0) for e in es]
        for h in range(DIL_HEADS):
            hl = slice(h * DIL_HEAD_DIM, (h + 1) * DIL_HEAD_DIM)
            acc = ws[0][:, h:h + 1] * o_refs[0][rows, hl].astype(F32)
            for g in range(1, N_GROUPS):
                acc = acc + ws[g][:, h:h + 1] * ot_ref[g - 1, rows, hl]
            z = z_ref[rows, hl].astype(F32)
            y_ref[rows, hl] = (acc * (z * jax.nn.sigmoid(z))).astype(BF16)
        x = x_ref[rows, :] + jnp.dot(y_ref[rows, :], wout_ref[...], preferred_element_type=F32)
        y = x * lax.rsqrt(jnp.mean(x * x, axis=-1, keepdims=True) + RMS_EPS)
        out_ref[rows, :] = y * g_ref[...]

    n_blocks = tm // PERM_ROWS
    sub = COMBINE_SUB_ROWS
    stats = to_token_order(0)
    for b in range(n_blocks):
        nxt = to_token_order(b + 1) if b + 1 < n_blocks else None
        for u in range(PERM_ROWS // sub):
            local = slice(u * sub, (u + 1) * sub)
            rows = slice(b * PERM_ROWS + u * sub, b * PERM_ROWS + (u + 1) * sub)
            mix_and_project(rows, [mg[local] for mg in stats[0]], [lg[local] for lg in stats[1]])
        stats = nxt


def _combine(os_, ms, ls, z, x, w_out, g_final, *, tm):
    S, D = x.shape
    row_blk = lambda m: (m, 0)
    return pl.pallas_call(
        _combine_kernel,
        grid=(S // tm,),
        in_specs=[pl.BlockSpec((tm, D_INNER), row_blk)] * 3 + [pl.BlockSpec((tm, LANES), row_blk)] * 6 + [
            pl.BlockSpec((tm, D_INNER), row_blk),
            pl.BlockSpec((tm, D), row_blk),
            pl.BlockSpec((D_INNER, D), lambda m: (0, 0)),
            pl.BlockSpec((1, D), lambda m: (0, 0)),
        ],
        out_specs=pl.BlockSpec((tm, D), row_blk),
        out_shape=jax.ShapeDtypeStruct((S, D), F32),
        scratch_shapes=[pltpu.VMEM((tm, D_INNER), BF16), pltpu.VMEM((N_GROUPS - 1, tm, D_INNER), F32)],
        compiler_params=pltpu.CompilerParams(
            dimension_semantics=("arbitrary",), vmem_limit_bytes=VMEM_LIMIT),
        name="combine",
    )(*os_, *ms, *ls, z, x, w_out, g_final)


def kernel(x, a_norm, a_w_in, a_w_gate_up, a_b_gate, a_g_out, a_w_out, kv_norm, w_kv, b_norm, b_w_in, b_w_out,
           final_norm):
    B, S, D = x.shape
    assert B == 1 and D == D_MODEL and a_norm.shape[0] == 1 and b_norm.shape[0] == 1
    assert S % (ATTN_BLOCKS_PER_STEP * DIL_BLOCK * max(d for _, d in DIL_GROUPS)) == 0
    x0 = x.reshape(S, D)

    n_a = 2 * GLA_DK + 2 * D_INNER
    w_a = _scale_cast(a_w_in, a_norm, jnp.zeros((2,), jnp.int32), n_out=n_a, tn=n_a // 2)
    w_lr = _scale_cast(jnp.pad(a_w_in[0, :, n_a:], ((0, 0), (0, LANES - GLA_GATE_RANK))), a_norm,
                       jnp.zeros((1,), jnp.int32), n_out=LANES, tn=LANES)
    no_gain = jnp.ones((1, D), F32)
    proj_a, lr = _norm_proj(x0, no_gain, [(-1, 1)], jnp.zeros((2,), jnp.int32), w_a, w_lr, n_out=n_a, tm=1024,
                            tn=n_a // 2)
    w_gu = jnp.pad(a_w_gate_up[0], ((0, LANES - GLA_GATE_RANK), (0, 0))).astype(BF16)
    x1 = _gla_layer(x0, proj_a, lr, w_gu, a_b_gate, a_g_out, a_w_out[0].astype(BF16), tm=512)

    q_scale = DIL_HEAD_DIM ** -0.5 * LOG2_E
    w_kv_b = _scale_cast(w_kv, kv_norm[None], jnp.zeros((2 * N_GROUPS,), jnp.int32), n_out=w_kv.shape[1], tn=D_INNER)
    w_qz = _scale_cast(b_w_in, jnp.stack([b_norm[0] * q_scale, b_norm[0]]),
                       jnp.array([0] * N_GROUPS + [1], jnp.int32), n_out=b_w_in.shape[2], tn=D_INNER)
    variants = [(-1, d) for _, d in DIL_GROUPS]
    z, hv = _norm_proj(x1, no_gain, variants[:1], jnp.zeros((2,), jnp.int32), w_qz, None, variants,
                       n_out=D_INNER, w_col0=N_GROUPS * D_INNER, tm=1024, tn=D_INNER // 2)

    outs = [_dil_attn_group(hv, w_kv_b, w_qz, g, S=S, qb=ATTN_BLOCKS_PER_STEP) for g in range(N_GROUPS)]
    out = _combine([o for o, _, _ in outs], [m for _, m, _ in outs], [l for _, _, l in outs], z, x1,
                   b_w_out[0].astype(BF16), final_norm.reshape(1, D), tm=512)
    return out.reshape(B, S, D)
```

```python
import functools

import jax
import jax.numpy as jnp
from jax import lax
from jax.experimental import pallas as pl
from jax.experimental.pallas import tpu as pltpu

F32 = jnp.float32
BF16 = jnp.bfloat16

RMS_EPS = 1e-6
D_MODEL = 1024
D_INNER = 2048
GLA_HEADS = 4
GLA_DK_HEAD = 128
GLA_DV_HEAD = 512
GLA_DK = GLA_HEADS * GLA_DK_HEAD
GLA_GATE_RANK = 16
GLA_GATE_TAU = 16.0
GLA_CHUNK = 64
GLA_SUB = 16
GLA_CUMSUM_ROWS = 256
GLA_OUT_ROWS = 256
DIL_GROUPS = ((128, 1), (512, 4), (2048, 16))
N_GROUPS = 3
DIL_HEADS = 16
DIL_HEAD_DIM = 128
DIL_BLOCK = 128
ALIBI_MAX_EXP = 8.0
MASK_DISTANCE = 1e34
PERM_ROWS = 256
LOG2_E = 1.4426950408889634
PROJ_COLS = 256
PROJ_LOOKAHEAD = 2
SCORE_LOOKAHEAD = 8
ATTN_BLOCKS_PER_STEP = 4
COMBINE_SUB_ROWS = 128

LANES = 128
VMEM_LIMIT = 56 * 1024 * 1024

_NT = (((1,), (1,)), ((), ()))


def _scale_cast_kernel(sel_ref, w_ref, g_ref, o_ref):
    del sel_ref
    g = g_ref[...]
    for j in range(w_ref.shape[1] // LANES):
        cols = slice(j * LANES, (j + 1) * LANES)
        o_ref[:, cols] = (w_ref[:, cols] * g).astype(BF16)


def _scale_cast(w, gains, gain_of_block, *, n_out, tn):
    K = w.shape[-2]
    lead = (None,) * (w.ndim - 2)
    g_lanes = jnp.broadcast_to(gains[:, :, None], (gains.shape[0], K, LANES))
    return pl.pallas_call(
        _scale_cast_kernel,
        grid_spec=pltpu.PrefetchScalarGridSpec(
            num_scalar_prefetch=1, grid=(n_out // tn,),
            in_specs=[pl.BlockSpec(lead + (K, tn), lambda n, sel: (0,) * len(lead) + (0, n)),
                      pl.BlockSpec((None, K, LANES), lambda n, sel: (sel[n], 0, 0))],
            out_specs=pl.BlockSpec((K, tn), lambda n, sel: (0, n))),
        out_shape=jax.ShapeDtypeStruct((K, n_out), BF16),
        compiler_params=pltpu.CompilerParams(dimension_semantics=("arbitrary",), vmem_limit_bytes=VMEM_LIMIT),
        name="scale_cast",
    )(gain_of_block, w, g_lanes)


def _deinterleave_matrix(d, transpose=False):
    n = PERM_ROWS // d
    p = lax.broadcasted_iota(jnp.int32, (PERM_ROWS, PERM_ROWS), 1 if transpose else 0)
    j = lax.broadcasted_iota(jnp.int32, (PERM_ROWS, PERM_ROWS), 0 if transpose else 1)
    return (j == (p % n) * d + p // n).astype(BF16)


def _norm_proj_kernel(var_ref, x_ref, g_ref, w_ref, *rest, mm_variants, emit_variants, has_aux):
    rest = list(rest)
    waux_ref = rest.pop(0) if has_aux else None
    o_ref = rest.pop(0)
    aux_ref = rest.pop(0) if has_aux else None
    hv_ref = rest.pop(0) if emit_variants else None
    (h_ref,) = rest
    n = pl.program_id(1)
    tm = x_ref.shape[0]

    @pl.when(n == 0)
    def _():
        x = x_ref[...]
        y = x * lax.rsqrt(jnp.mean(x * x, axis=-1, keepdims=True) + RMS_EPS)
        targets = ([(h_ref, i, v) for i, v in enumerate(mm_variants)]
                   + [(hv_ref, i, v) for i, v in enumerate(emit_variants)])
        for gi in sorted({g for _, _, (g, _) in targets}):
            h = (y if gi < 0 else y * g_ref[gi:gi + 1, :]).astype(BF16)
            for ref, i, (g, d) in targets:
                if g != gi:
                    continue
                if d == 1:
                    ref[i] = h
                else:
                    perm = _deinterleave_matrix(d)
                    for b in range(tm // PERM_ROWS):
                        rows = slice(b * PERM_ROWS, (b + 1) * PERM_ROWS)
                        ref[i, rows, :] = jnp.dot(perm, h[rows], preferred_element_type=F32).astype(BF16)
        if has_aux:
            aux_ref[...] = jnp.dot(h_ref[0], waux_ref[...], preferred_element_type=F32)

    h = h_ref[var_ref[n]]
    o_ref[...] = jnp.dot(h, w_ref[...], preferred_element_type=F32).astype(o_ref.dtype)


def _norm_proj(x, gains, mm_variants, var_of_block, w, w_aux, emit_variants=(), *, n_out, w_col0=0, tm, tn):
    S, D = x.shape
    N = n_out
    n_gains = gains.shape[0]
    has_aux = w_aux is not None
    grid = (S // tm, N // tn)
    w_blk0 = w_col0 // tn
    in_specs = [
        pl.BlockSpec((tm, D), lambda m, n, var: (m, 0)),
        pl.BlockSpec((n_gains, D), lambda m, n, var: (0, 0)),
        pl.BlockSpec((D, tn), lambda m, n, var: (0, w_blk0 + n)),
    ]
    out_shape = [jax.ShapeDtypeStruct((S, N), BF16)]
    out_specs = [pl.BlockSpec((tm, tn), lambda m, n, var: (m, n))]
    args = [x, gains, w]
    if has_aux:
        in_specs.append(pl.BlockSpec((D, LANES), lambda m, n, var: (0, 0)))
        out_shape.append(jax.ShapeDtypeStruct((S, LANES), F32))
        out_specs.append(pl.BlockSpec((tm, LANES), lambda m, n, var: (m, 0)))
        args.append(w_aux)
    if emit_variants:
        out_shape.append(jax.ShapeDtypeStruct((len(emit_variants), S, D), BF16))
        out_specs.append(pl.BlockSpec((len(emit_variants), tm, D), lambda m, n, var: (0, m, 0)))
    return pl.pallas_call(
        functools.partial(_norm_proj_kernel, mm_variants=tuple(mm_variants), emit_variants=tuple(emit_variants),
                          has_aux=has_aux),
        grid_spec=pltpu.PrefetchScalarGridSpec(
            num_scalar_prefetch=1, grid=grid, in_specs=in_specs, out_specs=out_specs,
            scratch_shapes=[pltpu.VMEM((len(mm_variants), tm, D), BF16)]),
        out_shape=out_shape,
        compiler_params=pltpu.CompilerParams(
            dimension_semantics=("arbitrary", "arbitrary"), vmem_limit_bytes=VMEM_LIMIT),
        name="norm_proj_aux" if has_aux else "norm_proj",
    )(var_of_block, *args)


def _split3(a):
    hi = a.astype(BF16)
    r1 = a - hi.astype(F32)
    mid = r1.astype(BF16)
    lo = (r1 - mid.astype(F32)).astype(BF16)
    return hi, mid, lo


def _gla_scores(b_ref, q_ref, k_ref, row0, v, state):
    C, SUB, HALF = GLA_CHUNK, GLA_SUB, GLA_SUB // 2
    rows = pl.ds(row0, C)
    b, q, k = b_ref[rows, :], q_ref[rows, :], k_ref[rows, :]
    dk = q.shape[1]
    b_last = b_ref[pl.ds(row0 + C - 1, 1), :]
    o_inter = jnp.dot((q * jnp.exp2(b)).astype(BF16), state.astype(BF16), preferred_element_type=F32)
    k_dec = k * jnp.exp2(b_last - b)
    upd = jnp.dot(k_dec.T.astype(BF16), v, preferred_element_type=F32)
    e_col = jnp.broadcast_to(jnp.exp2(b_last), (dk, dk)).T
    new_state = state * jnp.concatenate([e_col] * (v.shape[1] // dk), axis=1) + upd

    ones = jnp.ones((dk, LANES), BF16)
    diag_sums, offs = [], []
    for i in range(C // SUB):
        r0 = i * SUB
        halves = [(b[r0 + u * HALF:r0 + (u + 1) * HALF], q[r0 + u * HALF:r0 + (u + 1) * HALF]) for u in range(2)]
        prods = []
        for s in range(SUB):
            bs = b_ref[pl.ds(row0 + r0 + s, 1), :]
            ks = k_ref[pl.ds(row0 + r0 + s, 1), :]
            for u in range(s // HALF, 2):
                prods.append(halves[u][1] * jnp.exp2(halves[u][0] - bs) * ks)
        diag_sums.append(jnp.dot(jnp.concatenate(prods, axis=0).astype(BF16), ones, preferred_element_type=F32))
        if i > 0:
            b0 = b_ref[pl.ds(row0 + r0, 1), :]
            qt = (q[r0:r0 + SUB] * jnp.exp2(b[r0:r0 + SUB] - b0)).astype(BF16)
            kt = jnp.concatenate([(k[:r0] * jnp.exp2(b0 - b[:r0])).astype(BF16), jnp.zeros((C - r0, dk), BF16)], axis=0)
            offs.append(lax.dot_general(qt, kt, _NT, preferred_element_type=F32))
    return o_inter, new_state, diag_sums, offs


def _gla_output(o_inter, diag_sums, offs, v):
    C, SUB, HALF = GLA_CHUNK, GLA_SUB, GLA_SUB // 2
    lane = lax.broadcasted_iota(jnp.int32, (HALF, LANES), 1)
    row = lax.broadcasted_iota(jnp.int32, (HALF, LANES), 0)
    a_rows = []
    for i in range(C // SUB):
        r0 = i * SUB
        acc = [jnp.zeros((HALF, LANES), F32) for _ in range(2)]
        piece = 0
        for s in range(SUB):
            for u in range(s // HALF, 2):
                acc[u] = jnp.where(lane == r0 + s, diag_sums[i][piece * HALF:(piece + 1) * HALF, :], acc[u])
                piece += 1
        a_i = jnp.concatenate([jnp.where(lane <= row + r0 + u * HALF, acc[u], 0.0) for u in range(2)], axis=0)[:, :C]
        a_rows.append(a_i + offs[i - 1] if i > 0 else a_i)
    a = jnp.concatenate(a_rows, axis=0).astype(BF16)
    return o_inter + jnp.dot(a, v, preferred_element_type=F32)


def _gla_kernel(q_ref, k_ref, v0_ref, v1_ref, z0_ref, z1_ref, lr_ref, x_ref, wgu_ref, bg_ref, gout_ref, wout_ref,
                o_ref, state_ref, y_ref, b_ref, q32_ref, k32_ref):
    tm = q_ref.shape[0]
    C = GLA_CHUNK
    heads_per_ref = v0_ref.shape[1] // GLA_DV_HEAD

    def head_cols(refs, h):
        u = h % heads_per_ref
        return refs[h // heads_per_ref], slice(u * GLA_DV_HEAD, (u + 1) * GLA_DV_HEAD)

    @pl.when(pl.program_id(0) == 0)
    def _():
        state_ref[...] = jnp.zeros_like(state_ref)

    pre = jnp.dot(lr_ref[...].astype(BF16), wgu_ref[...], preferred_element_type=F32) + bg_ref[...]
    log2_a = (jnp.minimum(pre, 0.0) - jnp.log1p(jnp.exp(-jnp.abs(pre)))) * (LOG2_E / GLA_GATE_TAU)
    tb = min(tm, GLA_CUMSUM_ROWS)
    r = lax.broadcasted_iota(jnp.int32, (tb, tb), 0)
    c = lax.broadcasted_iota(jnp.int32, (tb, tb), 1)
    tri = ((c <= r) & ((c // C) == (r // C))).astype(BF16)
    b_all = jnp.concatenate(
        [sum(jnp.dot(tri, piece, preferred_element_type=F32) for piece in _split3(log2_a[t * tb:(t + 1) * tb]))
         for t in range(tm // tb)], axis=0)
    for h in range(GLA_HEADS):
        kl = slice(h * GLA_DK_HEAD, (h + 1) * GLA_DK_HEAD)
        b_ref[h] = b_all[:, kl]
        q32_ref[h] = q_ref[:, kl].astype(F32) * (GLA_DK_HEAD ** -0.5)
        k32_ref[h] = k_ref[:, kl].astype(F32)

    gout = gout_ref[...]

    out_rows = GLA_OUT_ROWS
    for ci in range(tm // C):
        row0 = ci * C
        rows = slice(row0, row0 + C)
        vls = [slice(h * GLA_DV_HEAD, (h + 1) * GLA_DV_HEAD) for h in range(GLA_HEADS)]
        parts = []
        for h in range(GLA_HEADS):
            v_ref, vl = head_cols((v0_ref, v1_ref), h)
            o_inter, new_state, diag_sums, offs = _gla_scores(b_ref.at[h], q32_ref.at[h], k32_ref.at[h], row0,
                                                              v_ref[rows, vl], state_ref[h])
            state_ref[h] = new_state
            parts.append((o_inter, diag_sums, offs))
        for h in range(GLA_HEADS):
            v_ref, vl = head_cols((v0_ref, v1_ref), h)
            z_ref, zl = head_cols((z0_ref, z1_ref), h)
            o = _gla_output(*parts[h], v_ref[rows, vl])
            o = o * lax.rsqrt(jnp.mean(o * o, axis=-1, keepdims=True) + RMS_EPS) * gout
            z = z_ref[rows, zl].astype(F32)
            y_ref[rows, vls[h]] = (o * (z * jax.nn.sigmoid(z))).astype(BF16)
        if (row0 + C) % out_rows == 0:
            done = slice(row0 + C - out_rows, row0 + C)
            o_ref[done, :] = x_ref[done, :] + jnp.dot(y_ref[done, :], wout_ref[...], preferred_element_type=F32)


def _gla_layer(x, proj, lr, w_gate_up, b_gate, g_out, w_out, *, tm):
    S, D = x.shape
    half = D_INNER // 2
    v_blk, z_blk = 2 * GLA_DK // half, (2 * GLA_DK + D_INNER) // half
    return pl.pallas_call(
        _gla_kernel,
        grid=(S // tm,),
        in_specs=[
            pl.BlockSpec((tm, GLA_DK), lambda m: (m, 0)),
            pl.BlockSpec((tm, GLA_DK), lambda m: (m, 1)),
            pl.BlockSpec((tm, half), lambda m: (m, v_blk)),
            pl.BlockSpec((tm, half), lambda m: (m, v_blk + 1)),
            pl.BlockSpec((tm, half), lambda m: (m, z_blk)),
            pl.BlockSpec((tm, half), lambda m: (m, z_blk + 1)),
            pl.BlockSpec((tm, LANES), lambda m: (m, 0)),
            pl.BlockSpec((tm, D), lambda m: (m, 0)),
            pl.BlockSpec((LANES, GLA_DK), lambda m: (0, 0)),
            pl.BlockSpec((1, GLA_DK), lambda m: (0, 0)),
            pl.BlockSpec((1, GLA_DV_HEAD), lambda m: (0, 0)),
            pl.BlockSpec((D_INNER, D), lambda m: (0, 0)),
        ],
        out_specs=pl.BlockSpec((tm, D), lambda m: (m, 0)),
        out_shape=jax.ShapeDtypeStruct((S, D), F32),
        scratch_shapes=[
            pltpu.VMEM((GLA_HEADS, GLA_DK_HEAD, GLA_DV_HEAD), F32),
            pltpu.VMEM((tm, D_INNER), BF16),
            pltpu.VMEM((GLA_HEADS, tm, GLA_DK_HEAD), F32),
            pltpu.VMEM((GLA_HEADS, tm, GLA_DK_HEAD), F32),
            pltpu.VMEM((GLA_HEADS, tm, GLA_DK_HEAD), F32),
        ],
        compiler_params=pltpu.CompilerParams(
            dimension_semantics=("arbitrary",), vmem_limit_bytes=VMEM_LIMIT),
        name="gla",
    )(proj, proj, proj, proj, proj, proj, lr, x, w_gate_up, b_gate, g_out, w_out)


def _block_rows(ref):
    return ref[...].reshape(-1, ref.shape[-1])


def _dil_attn_kernel(h_ref, wq_ref, wk_ref, wv_ref, o_ref, m_ref, l_ref,
                     q_s, k_s, v_s, bias_ref, m_acc, l_acc, *, slopes, dilation, qb):
    blk = DIL_BLOCK
    i = pl.program_id(1)
    new_rows = slice(blk, (qb + 1) * blk)

    @pl.when((pl.program_id(0) == 0) & (i == 0))
    def _():
        row = lax.broadcasted_iota(jnp.int32, (blk, blk), 0)
        col = lax.broadcasted_iota(jnp.int32, (blk, blk), 1)
        dist_prev = jnp.where(col >= row, ((row + blk - col) * dilation).astype(F32), MASK_DISTANCE)
        dist_cur = jnp.where(col <= row, ((row - col) * dilation).astype(F32), MASK_DISTANCE)
        for h in range(DIL_HEADS):
            bias_ref[h, :, :blk] = -(slopes[h] * LOG2_E) * dist_prev
            bias_ref[h, :, blk:] = -(slopes[h] * LOG2_E) * dist_cur

    def all_heads(first_step):
        hq = hkv = _block_rows(h_ref)
        heads_per_chunk = PROJ_COLS // DIL_HEAD_DIM
        n_chunks = D_INNER // PROJ_COLS
        projected = [0]

        def project_through(chunk):
            while projected[0] <= min(chunk, n_chunks - 1):
                cols = slice(projected[0] * PROJ_COLS, (projected[0] + 1) * PROJ_COLS)
                q_s[:, cols] = jnp.dot(hq, wq_ref[:, cols], preferred_element_type=F32).astype(BF16)
                k_s[new_rows, cols] = jnp.dot(hkv, wk_ref[:, cols], preferred_element_type=F32).astype(BF16)
                v_s[new_rows, cols] = jnp.dot(hkv, wv_ref[:, cols], preferred_element_type=F32).astype(BF16)
                projected[0] += 1

        def key_rows(j):
            return slice((j + 1) * blk, (j + 2) * blk) if first_step and j == 0 else slice(j * blk, (j + 2) * blk)

        def scores(unit):
            h, j = divmod(unit, qb)
            project_through(h // heads_per_chunk + PROJ_LOOKAHEAD)
            hl = slice(h * DIL_HEAD_DIM, (h + 1) * DIL_HEAD_DIM)
            keys = key_rows(j)
            bias = bias_ref[h, :, 2 * blk - (keys.stop - keys.start):]
            q = q_s[j * blk:(j + 1) * blk, hl]
            return lax.dot_general(q, k_s[keys, hl], _NT, preferred_element_type=F32) + bias

        n_units = DIL_HEADS * qb
        ahead = [scores(u) for u in range(SCORE_LOOKAHEAD)]
        for unit in range(n_units):
            h, j = divmod(unit, qb)
            hl = slice(h * DIL_HEAD_DIM, (h + 1) * DIL_HEAD_DIM)
            s = ahead.pop(0)
            if unit + SCORE_LOOKAHEAD < n_units:
                ahead.append(scores(unit + SCORE_LOOKAHEAD))
            m = jnp.max(s, axis=-1, keepdims=True)
            p = jnp.exp2(s - m)
            l = jnp.sum(p, axis=-1, keepdims=True)
            o = jnp.dot(p.astype(BF16), v_s[key_rows(j), hl], preferred_element_type=F32).astype(o_ref.dtype)
            if len(o_ref.shape) == 2:
                o_ref[j * blk:(j + 1) * blk, hl] = o
            else:
                pieces = o_ref.shape[0] // qb
                o_ref[j * pieces:(j + 1) * pieces, :, hl] = o.reshape(pieces, o_ref.shape[1], DIL_HEAD_DIM)
            m_acc[j * blk:(j + 1) * blk, h:h + 1] = m
            l_acc[j * blk:(j + 1) * blk, h:h + 1] = l

    m_acc[...] = jnp.zeros_like(m_acc)
    l_acc[...] = jnp.zeros_like(l_acc)

    @pl.when(i == 0)
    def _():
        all_heads(True)

    @pl.when(i > 0)
    def _():
        all_heads(False)

    m_ref[...] = m_acc[...].reshape(m_ref.shape)
    l_ref[...] = l_acc[...].reshape(l_ref.shape)
    k_s[:blk, :] = k_s[qb * blk:, :]
    v_s[:blk, :] = v_s[qb * blk:, :]


def _dil_attn_group(hv, w_kv, w_q, g, *, S, qb):
    _, d = DIL_GROUPS[g]
    D = hv.shape[-1]
    rows = qb * DIL_BLOCK
    n_steps = S // d // rows
    n_all = N_GROUPS * DIL_HEADS
    slopes = tuple(2.0 ** (-ALIBI_MAX_EXP * (g * DIL_HEADS + h + 1.0) / n_all) for h in range(DIL_HEADS))
    kcol, vcol, qcol = g, N_GROUPS + g, g
    if d == 1:
        h_spec = lambda v: pl.BlockSpec((None, rows, D), lambda r, i: (v, i, 0))
        spec = lambda width: pl.BlockSpec((rows, width), lambda r, i: (i, 0))
        shape = lambda width, dt: jax.ShapeDtypeStruct((S, width), dt)
    else:
        run = PERM_ROWS // d
        pieces = rows // run
        hv = hv.reshape(hv.shape[0], S // PERM_ROWS, d, run, D)
        h_spec = lambda v: pl.BlockSpec((None, pieces, None, run, D), lambda r, i: (v, i, r, 0, 0))
        spec = lambda width: pl.BlockSpec((pieces, None, run, width), lambda r, i: (i, r, 0, 0))
        shape = lambda width, dt: jax.ShapeDtypeStruct((S // PERM_ROWS, d, run, width), dt)
    w_spec = lambda col: pl.BlockSpec((D, D_INNER), lambda r, i: (0, col))
    o, m, l = pl.pallas_call(
        functools.partial(_dil_attn_kernel, slopes=slopes, dilation=d, qb=qb),
        grid=(d, n_steps),
        in_specs=[h_spec(g), w_spec(qcol), w_spec(kcol), w_spec(vcol)],
        out_specs=[spec(D_INNER), spec(LANES), spec(LANES)],
        out_shape=[shape(D_INNER, BF16), shape(LANES, F32), shape(LANES, F32)],
        scratch_shapes=[
            pltpu.VMEM((rows, D_INNER), BF16),
            pltpu.VMEM((rows + DIL_BLOCK, D_INNER), BF16),
            pltpu.VMEM((rows + DIL_BLOCK, D_INNER), BF16),
            pltpu.VMEM((DIL_HEADS, DIL_BLOCK, 2 * DIL_BLOCK), F32),
            pltpu.VMEM((rows, LANES), F32),
            pltpu.VMEM((rows, LANES), F32),
        ],
        compiler_params=pltpu.CompilerParams(
            dimension_semantics=("arbitrary", "arbitrary"), vmem_limit_bytes=VMEM_LIMIT),
        name=f"dil_attn_g{g}",
    )(hv, w_q, w_kv, w_kv)
    return o.reshape(S, D_INNER), m.reshape(S, LANES), l.reshape(S, LANES)


def _combine_kernel(o0_ref, o1_ref, o2_ref, m0_ref, m1_ref, m2_ref, l0_ref, l1_ref, l2_ref, z_ref, x_ref,
                    wout_ref, g_ref, out_ref, y_ref, ot_ref):
    tm = x_ref.shape[0]
    o_refs = (o0_ref, o1_ref, o2_ref)
    m_refs = (m0_ref, m1_ref, m2_ref)
    l_refs = (l0_ref, l1_ref, l2_ref)
    unperms = [None] + [_deinterleave_matrix(d, transpose=True) for _, d in DIL_GROUPS[1:]]

    def to_token_order(b):
        rows = slice(b * PERM_ROWS, (b + 1) * PERM_ROWS)
        ms, ls = [m_refs[0][rows, :]], [l_refs[0][rows, :]]
        for g in range(1, N_GROUPS):
            ot_ref[g - 1, rows, :] = jnp.dot(unperms[g], o_refs[g][rows, :], preferred_element_type=F32)
            for stats, ref in ((ms, m_refs[g]), (ls, l_refs[g])):
                stats.append(sum(jnp.dot(unperms[g], piece, preferred_element_type=F32)
                                 for piece in _split3(ref[rows, :])))
        return ms, ls

    def mix_and_project(rows, ms, ls):
        m = jnp.maximum(jnp.maximum(ms[0], ms[1]), ms[2])
        es = [jnp.exp2(mg - m) for mg in ms]
        den = es[0] * ls[0] + es[1] * ls[1] + es[2] * ls[2]
        lane = lax.broadcasted_iota(jnp.int32, m.shape, 1)
        ws = [jnp.where(lane < DIL_HEADS, e / den, 0.0) for e in es]
        for h in range(DIL_HEADS):
            hl = slice(h * DIL_HEAD_DIM, (h + 1) * DIL_HEAD_DIM)
            acc = ws[0][:, h:h + 1] * o_refs[0][rows, hl].astype(F32)
            for g in range(1, N_GROUPS):
                acc = acc + ws[g][:, h:h + 1] * ot_ref[g - 1, rows, hl]
            z = z_ref[rows, hl].astype(F32)
            y_ref[rows, hl] = (acc * (z * jax.nn.sigmoid(z))).astype(BF16)
        x = x_ref[rows, :] + jnp.dot(y_ref[rows, :], wout_ref[...], preferred_element_type=F32)
        y = x * lax.rsqrt(jnp.mean(x * x, axis=-1, keepdims=True) + RMS_EPS)
        out_ref[rows, :] = y * g_ref[...]

    n_blocks = tm // PERM_ROWS
    sub = COMBINE_SUB_ROWS
    stats = to_token_order(0)
    for b in range(n_blocks):
        nxt = to_token_order(b + 1) if b + 1 < n_blocks else None
        for u in range(PERM_ROWS // sub):
            local = slice(u * sub, (u + 1) * sub)
            rows = slice(b * PERM_ROWS + u * sub, b * PERM_ROWS + (u + 1) * sub)
            mix_and_project(rows, [mg[local] for mg in stats[0]], [lg[local] for lg in stats[1]])
        stats = nxt


def _combine(os_, ms, ls, z, x, w_out, g_final, *, tm):
    S, D = x.shape
    row_blk = lambda m: (m, 0)
    return pl.pallas_call(
        _combine_kernel,
        grid=(S // tm,),
        in_specs=[pl.BlockSpec((tm, D_INNER), row_blk)] * 3 + [pl.BlockSpec((tm, LANES), row_blk)] * 6 + [
            pl.BlockSpec((tm, D_INNER), row_blk),
            pl.BlockSpec((tm, D), row_blk),
            pl.BlockSpec((D_INNER, D), lambda m: (0, 0)),
            pl.BlockSpec((1, D), lambda m: (0, 0)),
        ],
        out_specs=pl.BlockSpec((tm, D), row_blk),
        out_shape=jax.ShapeDtypeStruct((S, D), F32),
        scratch_shapes=[pltpu.VMEM((tm, D_INNER), BF16), pltpu.VMEM((N_GROUPS - 1, tm, D_INNER), F32)],
        compiler_params=pltpu.CompilerParams(
            dimension_semantics=("arbitrary",), vmem_limit_bytes=VMEM_LIMIT),
        name="combine",
    )(*os_, *ms, *ls, z, x, w_out, g_final)


def kernel(x, a_norm, a_w_in, a_w_gate_up, a_b_gate, a_g_out, a_w_out, kv_norm, w_kv, b_norm, b_w_in, b_w_out,
           final_norm):
    B, S, D = x.shape
    assert B == 1 and D == D_MODEL and a_norm.shape[0] == 1 and b_norm.shape[0] == 1
    assert S % (ATTN_BLOCKS_PER_STEP * DIL_BLOCK * max(d for _, d in DIL_GROUPS)) == 0
    x0 = x.reshape(S, D)

    n_a = 2 * GLA_DK + 2 * D_INNER
    w_a = a_w_in[0].astype(BF16)
    w_lr = jnp.pad(w_a[:, n_a:], ((0, 0), (0, LANES - GLA_GATE_RANK)))
    proj_a, lr = _norm_proj(x0, a_norm, [(0, 1)], jnp.zeros((2,), jnp.int32), w_a, w_lr, n_out=n_a, tm=1024,
                            tn=n_a // 2)
    w_gu = jnp.pad(a_w_gate_up[0], ((0, LANES - GLA_GATE_RANK), (0, 0))).astype(BF16)
    x1 = _gla_layer(x0, proj_a, lr, w_gu, a_b_gate, a_g_out, a_w_out[0].astype(BF16), tm=512)

    q_scale = DIL_HEAD_DIM ** -0.5 * LOG2_E
    w_kv_b = _scale_cast(w_kv, kv_norm[None], jnp.zeros((2 * N_GROUPS,), jnp.int32), n_out=w_kv.shape[1], tn=D_INNER)
    w_qz = _scale_cast(b_w_in, jnp.stack([b_norm[0] * q_scale, b_norm[0]]),
                       jnp.array([0] * N_GROUPS + [1], jnp.int32), n_out=b_w_in.shape[2], tn=D_INNER)
    variants = [(-1, d) for _, d in DIL_GROUPS]
    z, hv = _norm_proj(x1, jnp.ones((1, D), F32), variants[:1], jnp.zeros((2,), jnp.int32), w_qz, None, variants,
                       n_out=D_INNER, w_col0=N_GROUPS * D_INNER, tm=1024, tn=D_INNER // 2)

    outs = [_dil_attn_group(hv, w_kv_b, w_qz, g, S=S, qb=ATTN_BLOCKS_PER_STEP) for g in range(N_GROUPS)]
    out = _combine([o for o, _, _ in outs], [m for _, m, _ in outs], [l for _, _, l in outs], z, x1,
                   b_w_out[0].astype(BF16), final_norm.reshape(1, D), tm=512)
    return out.reshape(B, S, D)
```

```python
import functools

import jax
import jax.numpy as jnp
from jax import lax
from jax.experimental import pallas as pl
from jax.experimental.pallas import tpu as pltpu

F32 = jnp.float32
BF16 = jnp.bfloat16

RMS_EPS = 1e-6
D_MODEL = 1024
D_INNER = 2048
GLA_HEADS = 4
GLA_DK_HEAD = 128
GLA_DV_HEAD = 512
GLA_DK = GLA_HEADS * GLA_DK_HEAD
GLA_GATE_RANK = 16
GLA_GATE_TAU = 16.0
GLA_CHUNK = 64
GLA_SUB = 16
GLA_CUMSUM_ROWS = 256
GLA_OUT_ROWS = 256
DIL_GROUPS = ((128, 1), (512, 4), (2048, 16))
N_GROUPS = 3
DIL_HEADS = 16
DIL_HEAD_DIM = 128
DIL_BLOCK = 128
ALIBI_MAX_EXP = 8.0
MASK_DISTANCE = 1e34
PERM_ROWS = 256
LOG2_E = 1.4426950408889634
PROJ_COLS = 256
PROJ_LOOKAHEAD = 2
SCORE_LOOKAHEAD = 8
ATTN_BLOCKS_PER_STEP = 8
COMBINE_SUB_ROWS = 128

LANES = 128
VMEM_LIMIT = 56 * 1024 * 1024

_NT = (((1,), (1,)), ((), ()))


def _scale_cast_kernel(sel_ref, w_ref, g_ref, o_ref):
    del sel_ref
    g = g_ref[...]
    for j in range(w_ref.shape[1] // LANES):
        cols = slice(j * LANES, (j + 1) * LANES)
        o_ref[:, cols] = (w_ref[:, cols] * g).astype(BF16)


def _scale_cast(w, gains, gain_of_block, *, n_out, tn):
    K = w.shape[-2]
    lead = (None,) * (w.ndim - 2)
    g_lanes = jnp.broadcast_to(gains[:, :, None], (gains.shape[0], K, LANES))
    return pl.pallas_call(
        _scale_cast_kernel,
        grid_spec=pltpu.PrefetchScalarGridSpec(
            num_scalar_prefetch=1, grid=(n_out // tn,),
            in_specs=[pl.BlockSpec(lead + (K, tn), lambda n, sel: (0,) * len(lead) + (0, n)),
                      pl.BlockSpec((None, K, LANES), lambda n, sel: (sel[n], 0, 0))],
            out_specs=pl.BlockSpec((K, tn), lambda n, sel: (0, n))),
        out_shape=jax.ShapeDtypeStruct((K, n_out), BF16),
        compiler_params=pltpu.CompilerParams(dimension_semantics=("arbitrary",), vmem_limit_bytes=VMEM_LIMIT),
        name="scale_cast",
    )(gain_of_block, w, g_lanes)


def _deinterleave_matrix(d, transpose=False):
    n = PERM_ROWS // d
    p = lax.broadcasted_iota(jnp.int32, (PERM_ROWS, PERM_ROWS), 1 if transpose else 0)
    j = lax.broadcasted_iota(jnp.int32, (PERM_ROWS, PERM_ROWS), 0 if transpose else 1)
    return (j == (p % n) * d + p // n).astype(BF16)


def _norm_proj_kernel(var_ref, x_ref, g_ref, w_ref, *rest, mm_variants, emit_variants, has_aux):
    rest = list(rest)
    waux_ref = rest.pop(0) if has_aux else None
    o_ref = rest.pop(0)
    aux_ref = rest.pop(0) if has_aux else None
    hv_ref = rest.pop(0) if emit_variants else None
    (h_ref,) = rest
    n = pl.program_id(1)
    tm = x_ref.shape[0]

    @pl.when(n == 0)
    def _():
        x = x_ref[...]
        y = x * lax.rsqrt(jnp.mean(x * x, axis=-1, keepdims=True) + RMS_EPS)
        targets = ([(h_ref, i, v) for i, v in enumerate(mm_variants)]
                   + [(hv_ref, i, v) for i, v in enumerate(emit_variants)])
        for gi in sorted({g for _, _, (g, _) in targets}):
            h = (y if gi < 0 else y * g_ref[gi:gi + 1, :]).astype(BF16)
            for ref, i, (g, d) in targets:
                if g != gi:
                    continue
                if d == 1:
                    ref[i] = h
                else:
                    perm = _deinterleave_matrix(d)
                    for b in range(tm // PERM_ROWS):
                        rows = slice(b * PERM_ROWS, (b + 1) * PERM_ROWS)
                        ref[i, rows, :] = jnp.dot(perm, h[rows], preferred_element_type=F32).astype(BF16)
        if has_aux:
            aux_ref[...] = jnp.dot(h_ref[0], waux_ref[...], preferred_element_type=F32)

    h = h_ref[var_ref[n]]
    o_ref[...] = jnp.dot(h, w_ref[...], preferred_element_type=F32).astype(o_ref.dtype)


def _norm_proj(x, gains, mm_variants, var_of_block, w, w_aux, emit_variants=(), *, n_out, w_col0=0, tm, tn):
    S, D = x.shape
    N = n_out
    n_gains = gains.shape[0]
    has_aux = w_aux is not None
    grid = (S // tm, N // tn)
    w_blk0 = w_col0 // tn
    in_specs = [
        pl.BlockSpec((tm, D), lambda m, n, var: (m, 0)),
        pl.BlockSpec((n_gains, D), lambda m, n, var: (0, 0)),
        pl.BlockSpec((D, tn), lambda m, n, var: (0, w_blk0 + n)),
    ]
    out_shape = [jax.ShapeDtypeStruct((S, N), BF16)]
    out_specs = [pl.BlockSpec((tm, tn), lambda m, n, var: (m, n))]
    args = [x, gains, w]
    if has_aux:
        in_specs.append(pl.BlockSpec((D, LANES), lambda m, n, var: (0, 0)))
        out_shape.append(jax.ShapeDtypeStruct((S, LANES), F32))
        out_specs.append(pl.BlockSpec((tm, LANES), lambda m, n, var: (m, 0)))
        args.append(w_aux)
    if emit_variants:
        out_shape.append(jax.ShapeDtypeStruct((len(emit_variants), S, D), BF16))
        out_specs.append(pl.BlockSpec((len(emit_variants), tm, D), lambda m, n, var: (0, m, 0)))
    return pl.pallas_call(
        functools.partial(_norm_proj_kernel, mm_variants=tuple(mm_variants), emit_variants=tuple(emit_variants),
                          has_aux=has_aux),
        grid_spec=pltpu.PrefetchScalarGridSpec(
            num_scalar_prefetch=1, grid=grid, in_specs=in_specs, out_specs=out_specs,
            scratch_shapes=[pltpu.VMEM((len(mm_variants), tm, D), BF16)]),
        out_shape=out_shape,
        compiler_params=pltpu.CompilerParams(
            dimension_semantics=("arbitrary", "arbitrary"), vmem_limit_bytes=VMEM_LIMIT),
        name="norm_proj_aux" if has_aux else "norm_proj",
    )(var_of_block, *args)


def _split3(a):
    hi = a.astype(BF16)
    r1 = a - hi.astype(F32)
    mid = r1.astype(BF16)
    lo = (r1 - mid.astype(F32)).astype(BF16)
    return hi, mid, lo


def _gla_scores(b_ref, q_ref, k_ref, row0, v, state):
    C, SUB, HALF = GLA_CHUNK, GLA_SUB, GLA_SUB // 2
    rows = pl.ds(row0, C)
    b, q, k = b_ref[rows, :], q_ref[rows, :], k_ref[rows, :]
    dk = q.shape[1]
    b_last = b_ref[pl.ds(row0 + C - 1, 1), :]
    o_inter = jnp.dot((q * jnp.exp2(b)).astype(BF16), state.astype(BF16), preferred_element_type=F32)
    k_dec = k * jnp.exp2(b_last - b)
    upd = jnp.dot(k_dec.T.astype(BF16), v, preferred_element_type=F32)
    e_col = jnp.broadcast_to(jnp.exp2(b_last), (dk, dk)).T
    new_state = state * jnp.concatenate([e_col] * (v.shape[1] // dk), axis=1) + upd

    ones = jnp.ones((dk, LANES), BF16)
    diag_sums, offs = [], []
    for i in range(C // SUB):
        r0 = i * SUB
        halves = [(b[r0 + u * HALF:r0 + (u + 1) * HALF], q[r0 + u * HALF:r0 + (u + 1) * HALF]) for u in range(2)]
        prods = []
        for s in range(SUB):
            bs = b_ref[pl.ds(row0 + r0 + s, 1), :]
            ks = k_ref[pl.ds(row0 + r0 + s, 1), :]
            for u in range(s // HALF, 2):
                prods.append(halves[u][1] * jnp.exp2(halves[u][0] - bs) * ks)
        diag_sums.append(jnp.dot(jnp.concatenate(prods, axis=0).astype(BF16), ones, preferred_element_type=F32))
        if i > 0:
            b0 = b_ref[pl.ds(row0 + r0, 1), :]
            qt = (q[r0:r0 + SUB] * jnp.exp2(b[r0:r0 + SUB] - b0)).astype(BF16)
            kt = jnp.concatenate([(k[:r0] * jnp.exp2(b0 - b[:r0])).astype(BF16), jnp.zeros((C - r0, dk), BF16)], axis=0)
            offs.append(lax.dot_general(qt, kt, _NT, preferred_element_type=F32))
    return o_inter, new_state, diag_sums, offs


def _gla_output(o_inter, diag_sums, offs, v):
    C, SUB, HALF = GLA_CHUNK, GLA_SUB, GLA_SUB // 2
    lane = lax.broadcasted_iota(jnp.int32, (HALF, LANES), 1)
    row = lax.broadcasted_iota(jnp.int32, (HALF, LANES), 0)
    a_rows = []
    for i in range(C // SUB):
        r0 = i * SUB
        acc = [jnp.zeros((HALF, LANES), F32) for _ in range(2)]
        piece = 0
        for s in range(SUB):
            for u in range(s // HALF, 2):
                acc[u] = jnp.where(lane == r0 + s, diag_sums[i][piece * HALF:(piece + 1) * HALF, :], acc[u])
                piece += 1
        a_i = jnp.concatenate([jnp.where(lane <= row + r0 + u * HALF, acc[u], 0.0) for u in range(2)], axis=0)[:, :C]
        a_rows.append(a_i + offs[i - 1] if i > 0 else a_i)
    a = jnp.concatenate(a_rows, axis=0).astype(BF16)
    return o_inter + jnp.dot(a, v, preferred_element_type=F32)


def _gla_kernel(q_ref, k_ref, v0_ref, v1_ref, z0_ref, z1_ref, lr_ref, x_ref, wgu_ref, bg_ref, gout_ref, wout_ref,
                o_ref, state_ref, y_ref, b_ref, q32_ref, k32_ref):
    tm = q_ref.shape[0]
    C = GLA_CHUNK
    heads_per_ref = v0_ref.shape[1] // GLA_DV_HEAD

    def head_cols(refs, h):
        u = h % heads_per_ref
        return refs[h // heads_per_ref], slice(u * GLA_DV_HEAD, (u + 1) * GLA_DV_HEAD)

    @pl.when(pl.program_id(0) == 0)
    def _():
        state_ref[...] = jnp.zeros_like(state_ref)

    pre = jnp.dot(lr_ref[...].astype(BF16), wgu_ref[...], preferred_element_type=F32) + bg_ref[...]
    log2_a = (jnp.minimum(pre, 0.0) - jnp.log1p(jnp.exp(-jnp.abs(pre)))) * (LOG2_E / GLA_GATE_TAU)
    tb = min(tm, GLA_CUMSUM_ROWS)
    r = lax.broadcasted_iota(jnp.int32, (tb, tb), 0)
    c = lax.broadcasted_iota(jnp.int32, (tb, tb), 1)
    tri = ((c <= r) & ((c // C) == (r // C))).astype(BF16)
    b_all = jnp.concatenate(
        [sum(jnp.dot(tri, piece, preferred_element_type=F32) for piece in _split3(log2_a[t * tb:(t + 1) * tb]))
         for t in range(tm // tb)], axis=0)
    for h in range(GLA_HEADS):
        kl = slice(h * GLA_DK_HEAD, (h + 1) * GLA_DK_HEAD)
        b_ref[h] = b_all[:, kl]
        q32_ref[h] = q_ref[:, kl].astype(F32) * (GLA_DK_HEAD ** -0.5)
        k32_ref[h] = k_ref[:, kl].astype(F32)

    gout = gout_ref[...]

    out_rows = GLA_OUT_ROWS
    for ci in range(tm // C):
        row0 = ci * C
        rows = slice(row0, row0 + C)
        vls = [slice(h * GLA_DV_HEAD, (h + 1) * GLA_DV_HEAD) for h in range(GLA_HEADS)]
        parts = []
        for h in range(GLA_HEADS):
            v_ref, vl = head_cols((v0_ref, v1_ref), h)
            o_inter, new_state, diag_sums, offs = _gla_scores(b_ref.at[h], q32_ref.at[h], k32_ref.at[h], row0,
                                                              v_ref[rows, vl], state_ref[h])
            state_ref[h] = new_state
            parts.append((o_inter, diag_sums, offs))
        for h in range(GLA_HEADS):
            v_ref, vl = head_cols((v0_ref, v1_ref), h)
            z_ref, zl = head_cols((z0_ref, z1_ref), h)
            o = _gla_output(*parts[h], v_ref[rows, vl])
            o = o * lax.rsqrt(jnp.mean(o * o, axis=-1, keepdims=True) + RMS_EPS) * gout
            z = z_ref[rows, zl].astype(F32)
            y_ref[rows, vls[h]] = (o * (z * jax.nn.sigmoid(z))).astype(BF16)
        if (row0 + C) % out_rows == 0:
            done = slice(row0 + C - out_rows, row0 + C)
            o_ref[done, :] = x_ref[done, :] + jnp.dot(y_ref[done, :], wout_ref[...], preferred_element_type=F32)


def _gla_layer(x, proj, lr, w_gate_up, b_gate, g_out, w_out, *, tm):
    S, D = x.shape
    half = D_INNER // 2
    v_blk, z_blk = 2 * GLA_DK // half, (2 * GLA_DK + D_INNER) // half
    return pl.pallas_call(
        _gla_kernel,
        grid=(S // tm,),
        in_specs=[
            pl.BlockSpec((tm, GLA_DK), lambda m: (m, 0)),
            pl.BlockSpec((tm, GLA_DK), lambda m: (m, 1)),
            pl.BlockSpec((tm, half), lambda m: (m, v_blk)),
            pl.BlockSpec((tm, half), lambda m: (m, v_blk + 1)),
            pl.BlockSpec((tm, half), lambda m: (m, z_blk)),
            pl.BlockSpec((tm, half), lambda m: (m, z_blk + 1)),
            pl.BlockSpec((tm, LANES), lambda m: (m, 0)),
            pl.BlockSpec((tm, D), lambda m: (m, 0)),
            pl.BlockSpec((LANES, GLA_DK), lambda m: (0, 0)),
            pl.BlockSpec((1, GLA_DK), lambda m: (0, 0)),
            pl.BlockSpec((1, GLA_DV_HEAD), lambda m: (0, 0)),
            pl.BlockSpec((D_INNER, D), lambda m: (0, 0)),
        ],
        out_specs=pl.BlockSpec((tm, D), lambda m: (m, 0)),
        out_shape=jax.ShapeDtypeStruct((S, D), F32),
        scratch_shapes=[
            pltpu.VMEM((GLA_HEADS, GLA_DK_HEAD, GLA_DV_HEAD), F32),
            pltpu.VMEM((tm, D_INNER), BF16),
            pltpu.VMEM((GLA_HEADS, tm, GLA_DK_HEAD), F32),
            pltpu.VMEM((GLA_HEADS, tm, GLA_DK_HEAD), F32),
            pltpu.VMEM((GLA_HEADS, tm, GLA_DK_HEAD), F32),
        ],
        compiler_params=pltpu.CompilerParams(
            dimension_semantics=("arbitrary",), vmem_limit_bytes=VMEM_LIMIT),
        name="gla",
    )(proj, proj, proj, proj, proj, proj, lr, x, w_gate_up, b_gate, g_out, w_out)


def _block_rows(ref):
    return ref[...].reshape(-1, ref.shape[-1])


def _dil_attn_kernel(h_ref, wq_ref, wk_ref, wv_ref, o_ref, m_ref, l_ref,
                     q_s, k_s, v_s, bias_ref, m_acc, l_acc, *, slopes, dilation, qb):
    blk = DIL_BLOCK
    i = pl.program_id(1)
    new_rows = slice(blk, (qb + 1) * blk)

    @pl.when((pl.program_id(0) == 0) & (i == 0))
    def _():
        row = lax.broadcasted_iota(jnp.int32, (blk, blk), 0)
        col = lax.broadcasted_iota(jnp.int32, (blk, blk), 1)
        dist_prev = jnp.where(col >= row, ((row + blk - col) * dilation).astype(F32), MASK_DISTANCE)
        dist_cur = jnp.where(col <= row, ((row - col) * dilation).astype(F32), MASK_DISTANCE)
        for h in range(DIL_HEADS):
            bias_ref[h, :, :blk] = -(slopes[h] * LOG2_E) * dist_prev
            bias_ref[h, :, blk:] = -(slopes[h] * LOG2_E) * dist_cur

    def all_heads(first_step):
        hq = hkv = _block_rows(h_ref)
        heads_per_chunk = PROJ_COLS // DIL_HEAD_DIM
        n_chunks = D_INNER // PROJ_COLS
        projected = [0]

        def project_through(chunk):
            while projected[0] <= min(chunk, n_chunks - 1):
                cols = slice(projected[0] * PROJ_COLS, (projected[0] + 1) * PROJ_COLS)
                q_s[:, cols] = jnp.dot(hq, wq_ref[:, cols], preferred_element_type=F32).astype(BF16)
                k_s[new_rows, cols] = jnp.dot(hkv, wk_ref[:, cols], preferred_element_type=F32).astype(BF16)
                v_s[new_rows, cols] = jnp.dot(hkv, wv_ref[:, cols], preferred_element_type=F32).astype(BF16)
                projected[0] += 1

        def key_rows(j):
            return slice((j + 1) * blk, (j + 2) * blk) if first_step and j == 0 else slice(j * blk, (j + 2) * blk)

        def scores(unit):
            h, j = divmod(unit, qb)
            project_through(h // heads_per_chunk + PROJ_LOOKAHEAD)
            hl = slice(h * DIL_HEAD_DIM, (h + 1) * DIL_HEAD_DIM)
            keys = key_rows(j)
            bias = bias_ref[h, :, 2 * blk - (keys.stop - keys.start):]
            q = q_s[j * blk:(j + 1) * blk, hl]
            return lax.dot_general(q, k_s[keys, hl], _NT, preferred_element_type=F32) + bias

        n_units = DIL_HEADS * qb
        ahead = [scores(u) for u in range(SCORE_LOOKAHEAD)]
        for unit in range(n_units):
            h, j = divmod(unit, qb)
            hl = slice(h * DIL_HEAD_DIM, (h + 1) * DIL_HEAD_DIM)
            s = ahead.pop(0)
            if unit + SCORE_LOOKAHEAD < n_units:
                ahead.append(scores(unit + SCORE_LOOKAHEAD))
            m = jnp.max(s, axis=-1, keepdims=True)
            p = jnp.exp2(s - m)
            l = jnp.sum(p, axis=-1, keepdims=True)
            o = jnp.dot(p.astype(BF16), v_s[key_rows(j), hl], preferred_element_type=F32).astype(o_ref.dtype)
            if len(o_ref.shape) == 2:
                o_ref[j * blk:(j + 1) * blk, hl] = o
            else:
                pieces = o_ref.shape[0] // qb
                o_ref[j * pieces:(j + 1) * pieces, :, hl] = o.reshape(pieces, o_ref.shape[1], DIL_HEAD_DIM)
            m_acc[j * blk:(j + 1) * blk, h:h + 1] = m
            l_acc[j * blk:(j + 1) * blk, h:h + 1] = l

    m_acc[...] = jnp.zeros_like(m_acc)
    l_acc[...] = jnp.zeros_like(l_acc)

    @pl.when(i == 0)
    def _():
        all_heads(True)

    @pl.when(i > 0)
    def _():
        all_heads(False)

    m_ref[...] = m_acc[...].reshape(m_ref.shape)
    l_ref[...] = l_acc[...].reshape(l_ref.shape)
    k_s[:blk, :] = k_s[qb * blk:, :]
    v_s[:blk, :] = v_s[qb * blk:, :]


def _dil_attn_group(hv, w_kv, w_q, g, *, S, qb):
    _, d = DIL_GROUPS[g]
    D = hv.shape[-1]
    rows = qb * DIL_BLOCK
    n_steps = S // d // rows
    n_all = N_GROUPS * DIL_HEADS
    slopes = tuple(2.0 ** (-ALIBI_MAX_EXP * (g * DIL_HEADS + h + 1.0) / n_all) for h in range(DIL_HEADS))
    kcol, vcol, qcol = g, N_GROUPS + g, g
    if d == 1:
        h_spec = lambda v: pl.BlockSpec((None, rows, D), lambda r, i: (v, i, 0))
        spec = lambda width: pl.BlockSpec((rows, width), lambda r, i: (i, 0))
        shape = lambda width, dt: jax.ShapeDtypeStruct((S, width), dt)
    else:
        run = PERM_ROWS // d
        pieces = rows // run
        hv = hv.reshape(hv.shape[0], S // PERM_ROWS, d, run, D)
        h_spec = lambda v: pl.BlockSpec((None, pieces, None, run, D), lambda r, i: (v, i, r, 0, 0))
        spec = lambda width: pl.BlockSpec((pieces, None, run, width), lambda r, i: (i, r, 0, 0))
        shape = lambda width, dt: jax.ShapeDtypeStruct((S // PERM_ROWS, d, run, width), dt)
    w_spec = lambda col: pl.BlockSpec((D, D_INNER), lambda r, i: (0, col))
    o, m, l = pl.pallas_call(
        functools.partial(_dil_attn_kernel, slopes=slopes, dilation=d, qb=qb),
        grid=(d, n_steps),
        in_specs=[h_spec(g), w_spec(qcol), w_spec(kcol), w_spec(vcol)],
        out_specs=[spec(D_INNER), spec(LANES), spec(LANES)],
        out_shape=[shape(D_INNER, BF16), shape(LANES, F32), shape(LANES, F32)],
        scratch_shapes=[
            pltpu.VMEM((rows, D_INNER), BF16),
            pltpu.VMEM((rows + DIL_BLOCK, D_INNER), BF16),
            pltpu.VMEM((rows + DIL_BLOCK, D_INNER), BF16),
            pltpu.VMEM((DIL_HEADS, DIL_BLOCK, 2 * DIL_BLOCK), F32),
            pltpu.VMEM((rows, LANES), F32),
            pltpu.VMEM((rows, LANES), F32),
        ],
        compiler_params=pltpu.CompilerParams(
            dimension_semantics=("arbitrary", "arbitrary"), vmem_limit_bytes=VMEM_LIMIT),
        name=f"dil_attn_g{g}",
    )(hv, w_q, w_kv, w_kv)
    return o.reshape(S, D_INNER), m.reshape(S, LANES), l.reshape(S, LANES)


def _combine_kernel(o0_ref, o1_ref, o2_ref, m0_ref, m1_ref, m2_ref, l0_ref, l1_ref, l2_ref, z_ref, x_ref,
                    wout_ref, g_ref, out_ref, y_ref, ot_ref):
    tm = x_ref.shape[0]
    o_refs = (o0_ref, o1_ref, o2_ref)
    m_refs = (m0_ref, m1_ref, m2_ref)
    l_refs = (l0_ref, l1_ref, l2_ref)
    unperms = [None] + [_deinterleave_matrix(d, transpose=True) for _, d in DIL_GROUPS[1:]]

    def to_token_order(b):
        rows = slice(b * PERM_ROWS, (b + 1) * PERM_ROWS)
        ms, ls = [m_refs[0][rows, :]], [l_refs[0][rows, :]]
        for g in range(1, N_GROUPS):
            ot_ref[g - 1, rows, :] = jnp.dot(unperms[g], o_refs[g][rows, :], preferred_element_type=F32)
            for stats, ref in ((ms, m_refs[g]), (ls, l_refs[g])):
                stats.append(sum(jnp.dot(unperms[g], piece, preferred_element_type=F32)
                                 for piece in _split3(ref[rows, :])))
        return ms, ls

    def mix_and_project(rows, ms, ls):
        m = jnp.maximum(jnp.maximum(ms[0], ms[1]), ms[2])
        es = [jnp.exp2(mg - m) for mg in ms]
        den = es[0] * ls[0] + es[1] * ls[1] + es[2] * ls[2]
        lane = lax.broadcasted_iota(jnp.int32, m.shape, 1)
        ws = [jnp.where(lane < DIL_HEADS, e / den, 0.0) for e in es]
        for h in range(DIL_HEADS):
            hl = slice(h * DIL_HEAD_DIM, (h + 1) * DIL_HEAD_DIM)
            acc = ws[0][:, h:h + 1] * o_refs[0][rows, hl].astype(F32)
            for g in range(1, N_GROUPS):
                acc = acc + ws[g][:, h:h + 1] * ot_ref[g - 1, rows, hl]
            z = z_ref[rows, hl].astype(F32)
            y_ref[rows, hl] = (acc * (z * jax.nn.sigmoid(z))).astype(BF16)
        x = x_ref[rows, :] + jnp.dot(y_ref[rows, :], wout_ref[...], preferred_element_type=F32)
        y = x * lax.rsqrt(jnp.mean(x * x, axis=-1, keepdims=True) + RMS_EPS)
        out_ref[rows, :] = y * g_ref[...]

    n_blocks = tm // PERM_ROWS
    sub = COMBINE_SUB_ROWS
    stats = to_token_order(0)
    for b in range(n_blocks):
        nxt = to_token_order(b + 1) if b + 1 < n_blocks else None
        for u in range(PERM_ROWS // sub):
            local = slice(u * sub, (u + 1) * sub)
            rows = slice(b * PERM_ROWS + u * sub, b * PERM_ROWS + (u + 1) * sub)
            mix_and_project(rows, [mg[local] for mg in stats[0]], [lg[local] for lg in stats[1]])
        stats = nxt


def _combine(os_, ms, ls, z, x, w_out, g_final, *, tm):
    S, D = x.shape
    row_blk = lambda m: (m, 0)
    return pl.pallas_call(
        _combine_kernel,
        grid=(S // tm,),
        in_specs=[pl.BlockSpec((tm, D_INNER), row_blk)] * 3 + [pl.BlockSpec((tm, LANES), row_blk)] * 6 + [
            pl.BlockSpec((tm, D_INNER), row_blk),
            pl.BlockSpec((tm, D), row_blk),
            pl.BlockSpec((D_INNER, D), lambda m: (0, 0)),
            pl.BlockSpec((1, D), lambda m: (0, 0)),
        ],
        out_specs=pl.BlockSpec((tm, D), row_blk),
        out_shape=jax.ShapeDtypeStruct((S, D), F32),
        scratch_shapes=[pltpu.VMEM((tm, D_INNER), BF16), pltpu.VMEM((N_GROUPS - 1, tm, D_INNER), F32)],
        compiler_params=pltpu.CompilerParams(
            dimension_semantics=("arbitrary",), vmem_limit_bytes=VMEM_LIMIT),
        name="combine",
    )(*os_, *ms, *ls, z, x, w_out, g_final)


def kernel(x, a_norm, a_w_in, a_w_gate_up, a_b_gate, a_g_out, a_w_out, kv_norm, w_kv, b_norm, b_w_in, b_w_out,
           final_norm):
    B, S, D = x.shape
    assert B == 1 and D == D_MODEL and a_norm.shape[0] == 1 and b_norm.shape[0] == 1
    assert S % (ATTN_BLOCKS_PER_STEP * DIL_BLOCK * max(d for _, d in DIL_GROUPS)) == 0
    x0 = x.reshape(S, D)

    n_a = 2 * GLA_DK + 2 * D_INNER
    w_a = a_w_in[0].astype(BF16)
    w_lr = jnp.pad(w_a[:, n_a:], ((0, 0), (0, LANES - GLA_GATE_RANK)))
    proj_a, lr = _norm_proj(x0, a_norm, [(0, 1)], jnp.zeros((2,), jnp.int32), w_a, w_lr, n_out=n_a, tm=1024,
                            tn=n_a // 2)
    w_gu = jnp.pad(a_w_gate_up[0], ((0, LANES - GLA_GATE_RANK), (0, 0))).astype(BF16)
    x1 = _gla_layer(x0, proj_a, lr, w_gu, a_b_gate, a_g_out, a_w_out[0].astype(BF16), tm=512)

    q_scale = DIL_HEAD_DIM ** -0.5 * LOG2_E
    w_kv_b = _scale_cast(w_kv, kv_norm[None], jnp.zeros((2 * N_GROUPS,), jnp.int32), n_out=w_kv.shape[1], tn=D_INNER)
    w_qz = _scale_cast(b_w_in, jnp.stack([b_norm[0] * q_scale, b_norm[0]]),
                       jnp.array([0] * N_GROUPS + [1], jnp.int32), n_out=b_w_in.shape[2], tn=D_INNER)
    variants = [(-1, d) for _, d in DIL_GROUPS]
    z, hv = _norm_proj(x1, jnp.ones((1, D), F32), variants[:1], jnp.zeros((2,), jnp.int32), w_qz, None, variants,
                       n_out=D_INNER, w_col0=N_GROUPS * D_INNER, tm=1024, tn=D_INNER // 2)

    outs = [_dil_attn_group(hv, w_kv_b, w_qz, g, S=S, qb=ATTN_BLOCKS_PER_STEP) for g in range(N_GROUPS)]
    out = _combine([o for o, _, _ in outs], [m for _, m, _ in outs], [l for _, _, l in outs], z, x1,
                   b_w_out[0].astype(BF16), final_norm.reshape(1, D), tm=512)
    return out.reshape(B, S, D)
```

```python
import functools

import jax
import jax.numpy as jnp
from jax import lax
from jax.experimental import pallas as pl
from jax.experimental.pallas import tpu as pltpu

F32 = jnp.float32
BF16 = jnp.bfloat16

RMS_EPS = 1e-6
D_MODEL = 1024
D_INNER = 2048
GLA_HEADS = 4
GLA_DK_HEAD = 128
GLA_DV_HEAD = 512
GLA_DK = GLA_HEADS * GLA_DK_HEAD
GLA_GATE_RANK = 16
GLA_GATE_TAU = 16.0
GLA_CHUNK = 64
GLA_SUB = 16
GLA_CUMSUM_ROWS = 256
GLA_OUT_ROWS = 256
DIL_GROUPS = ((128, 1), (512, 4), (2048, 16))
N_GROUPS = 3
DIL_HEADS = 16
DIL_HEAD_DIM = 128
DIL_BLOCK = 128
ALIBI_MAX_EXP = 8.0
MASK_DISTANCE = 1e34
PERM_ROWS = 256
LOG2_E = 1.4426950408889634
PROJ_COLS = 256
PROJ_LOOKAHEAD = 2
SCORE_LOOKAHEAD = 8
ATTN_BLOCKS_PER_STEP = 4
COMBINE_SUB_ROWS = 128
NORM_PROJ_ROWS = 1024
GLA_ROWS = 512
COMBINE_ROWS = 512

LANES = 128
VMEM_LIMIT = 56 * 1024 * 1024

_NT = (((1,), (1,)), ((), ()))


def _scale_cast_kernel(sel_ref, w_ref, g_ref, o_ref):
    del sel_ref
    g = g_ref[...]
    for j in range(w_ref.shape[1] // LANES):
        cols = slice(j * LANES, (j + 1) * LANES)
        o_ref[:, cols] = (w_ref[:, cols] * g).astype(BF16)


def _scale_cast(w, gains, gain_of_block, *, n_out, tn):
    K = w.shape[-2]
    lead = (None,) * (w.ndim - 2)
    g_lanes = jnp.broadcast_to(gains[:, :, None], (gains.shape[0], K, LANES))
    return pl.pallas_call(
        _scale_cast_kernel,
        grid_spec=pltpu.PrefetchScalarGridSpec(
            num_scalar_prefetch=1, grid=(n_out // tn,),
            in_specs=[pl.BlockSpec(lead + (K, tn), lambda n, sel: (0,) * len(lead) + (0, n)),
                      pl.BlockSpec((None, K, LANES), lambda n, sel: (sel[n], 0, 0))],
            out_specs=pl.BlockSpec((K, tn), lambda n, sel: (0, n))),
        out_shape=jax.ShapeDtypeStruct((K, n_out), BF16),
        compiler_params=pltpu.CompilerParams(dimension_semantics=("arbitrary",), vmem_limit_bytes=VMEM_LIMIT),
        name="scale_cast",
    )(gain_of_block, w, g_lanes)


def _deinterleave_matrix(d, transpose=False):
    n = PERM_ROWS // d
    p = lax.broadcasted_iota(jnp.int32, (PERM_ROWS, PERM_ROWS), 1 if transpose else 0)
    j = lax.broadcasted_iota(jnp.int32, (PERM_ROWS, PERM_ROWS), 0 if transpose else 1)
    return (j == (p % n) * d + p // n).astype(BF16)


def _norm_proj_kernel(var_ref, x_ref, *rest, mm_variants, emit_variants, has_gains, has_aux):
    rest = list(rest)
    g_ref = rest.pop(0) if has_gains else None
    w_ref = rest.pop(0)
    waux_ref = rest.pop(0) if has_aux else None
    o_ref = rest.pop(0)
    aux_ref = rest.pop(0) if has_aux else None
    hv_ref = rest.pop(0) if emit_variants else None
    (h_ref,) = rest
    n = pl.program_id(1)
    tm = x_ref.shape[0]

    @pl.when(n == 0)
    def _():
        x = x_ref[...]
        y = x * lax.rsqrt(jnp.mean(x * x, axis=-1, keepdims=True) + RMS_EPS)
        targets = ([(h_ref, i, v) for i, v in enumerate(mm_variants)]
                   + [(hv_ref, i, v) for i, v in enumerate(emit_variants)])
        for gi in sorted({g for _, _, (g, _) in targets}):
            h = (y if gi < 0 else y * g_ref[gi:gi + 1, :]).astype(BF16)
            for ref, i, (g, d) in targets:
                if g != gi:
                    continue
                if d == 1:
                    ref[i] = h
                else:
                    perm = _deinterleave_matrix(d)
                    for b in range(tm // PERM_ROWS):
                        rows = slice(b * PERM_ROWS, (b + 1) * PERM_ROWS)
                        ref[i, rows, :] = jnp.dot(perm, h[rows], preferred_element_type=F32).astype(BF16)
        if has_aux:
            aux_ref[...] = jnp.dot(h_ref[0], waux_ref[...], preferred_element_type=F32)

    h = h_ref[var_ref[n]]
    o_ref[...] = jnp.dot(h, w_ref[...], preferred_element_type=F32).astype(o_ref.dtype)


def _norm_proj(x, gains, mm_variants, var_of_block, w, w_aux, emit_variants=(), *, n_out, w_col0=0, tm, tn):
    S, D = x.shape
    N = n_out
    has_gains = gains is not None
    assert has_gains or all(g < 0 for g, _ in tuple(mm_variants) + tuple(emit_variants))
    has_aux = w_aux is not None
    grid = (S // tm, N // tn)
    w_blk0 = w_col0 // tn
    in_specs = [pl.BlockSpec((tm, D), lambda m, n, var: (m, 0))]
    args = [x]
    if has_gains:
        in_specs.append(pl.BlockSpec(gains.shape, lambda m, n, var: (0, 0)))
        args.append(gains)
    in_specs.append(pl.BlockSpec((D, tn), lambda m, n, var: (0, w_blk0 + n)))
    args.append(w)
    out_shape = [jax.ShapeDtypeStruct((S, N), BF16)]
    out_specs = [pl.BlockSpec((tm, tn), lambda m, n, var: (m, n))]
    if has_aux:
        in_specs.append(pl.BlockSpec((D, LANES), lambda m, n, var: (0, 0)))
        out_shape.append(jax.ShapeDtypeStruct((S, LANES), F32))
        out_specs.append(pl.BlockSpec((tm, LANES), lambda m, n, var: (m, 0)))
        args.append(w_aux)
    if emit_variants:
        out_shape.append(jax.ShapeDtypeStruct((len(emit_variants), S, D), BF16))
        out_specs.append(pl.BlockSpec((len(emit_variants), tm, D), lambda m, n, var: (0, m, 0)))
    return pl.pallas_call(
        functools.partial(_norm_proj_kernel, mm_variants=tuple(mm_variants), emit_variants=tuple(emit_variants),
                          has_gains=has_gains, has_aux=has_aux),
        grid_spec=pltpu.PrefetchScalarGridSpec(
            num_scalar_prefetch=1, grid=grid, in_specs=in_specs, out_specs=out_specs,
            scratch_shapes=[pltpu.VMEM((len(mm_variants), tm, D), BF16)]),
        out_shape=out_shape,
        compiler_params=pltpu.CompilerParams(
            dimension_semantics=("arbitrary", "arbitrary"), vmem_limit_bytes=VMEM_LIMIT),
        name="norm_proj_aux" if has_aux else "norm_proj",
    )(var_of_block, *args)


def _split3(a):
    hi = a.astype(BF16)
    r1 = a - hi.astype(F32)
    mid = r1.astype(BF16)
    lo = (r1 - mid.astype(F32)).astype(BF16)
    return hi, mid, lo


def _gla_scores(b_ref, q_ref, k_ref, row0, v, state):
    C, SUB, HALF = GLA_CHUNK, GLA_SUB, GLA_SUB // 2
    rows = pl.ds(row0, C)
    b, q, k = b_ref[rows, :], q_ref[rows, :], k_ref[rows, :]
    dk = q.shape[1]
    b_last = b_ref[pl.ds(row0 + C - 1, 1), :]
    o_inter = jnp.dot((q * jnp.exp2(b)).astype(BF16), state.astype(BF16), preferred_element_type=F32)
    k_dec = k * jnp.exp2(b_last - b)
    upd = jnp.dot(k_dec.T.astype(BF16), v, preferred_element_type=F32)
    e_col = jnp.broadcast_to(jnp.exp2(b_last), (dk, dk)).T
    new_state = state * jnp.concatenate([e_col] * (v.shape[1] // dk), axis=1) + upd

    ones = jnp.ones((dk, LANES), BF16)
    diag_sums, offs = [], []
    for i in range(C // SUB):
        r0 = i * SUB
        halves = [(b[r0 + u * HALF:r0 + (u + 1) * HALF], q[r0 + u * HALF:r0 + (u + 1) * HALF]) for u in range(2)]
        prods = []
        for s in range(SUB):
            bs = b_ref[pl.ds(row0 + r0 + s, 1), :]
            ks = k_ref[pl.ds(row0 + r0 + s, 1), :]
            for u in range(s // HALF, 2):
                prods.append(halves[u][1] * jnp.exp2(halves[u][0] - bs) * ks)
        diag_sums.append(jnp.dot(jnp.concatenate(prods, axis=0).astype(BF16), ones, preferred_element_type=F32))
        if i > 0:
            b0 = b_ref[pl.ds(row0 + r0, 1), :]
            qt = (q[r0:r0 + SUB] * jnp.exp2(b[r0:r0 + SUB] - b0)).astype(BF16)
            kt = jnp.concatenate([(k[:r0] * jnp.exp2(b0 - b[:r0])).astype(BF16), jnp.zeros((C - r0, dk), BF16)], axis=0)
            offs.append(lax.dot_general(qt, kt, _NT, preferred_element_type=F32))
    return o_inter, new_state, diag_sums, offs


def _gla_output(o_inter, diag_sums, offs, v):
    C, SUB, HALF = GLA_CHUNK, GLA_SUB, GLA_SUB // 2
    lane = lax.broadcasted_iota(jnp.int32, (HALF, LANES), 1)
    row = lax.broadcasted_iota(jnp.int32, (HALF, LANES), 0)
    a_rows = []
    for i in range(C // SUB):
        r0 = i * SUB
        acc = [jnp.zeros((HALF, LANES), F32) for _ in range(2)]
        piece = 0
        for s in range(SUB):
            for u in range(s // HALF, 2):
                acc[u] = jnp.where(lane == r0 + s, diag_sums[i][piece * HALF:(piece + 1) * HALF, :], acc[u])
                piece += 1
        a_i = jnp.concatenate([jnp.where(lane <= row + r0 + u * HALF, acc[u], 0.0) for u in range(2)], axis=0)[:, :C]
        a_rows.append(a_i + offs[i - 1] if i > 0 else a_i)
    a = jnp.concatenate(a_rows, axis=0).astype(BF16)
    return o_inter + jnp.dot(a, v, preferred_element_type=F32)


def _gla_kernel(q_ref, k_ref, v0_ref, v1_ref, z0_ref, z1_ref, lr_ref, x_ref, wgu_ref, bg_ref, gout_ref, wout_ref,
                o_ref, state_ref, y_ref, b_ref, q32_ref, k32_ref):
    tm = q_ref.shape[0]
    C = GLA_CHUNK
    heads_per_ref = v0_ref.shape[1] // GLA_DV_HEAD

    def head_cols(refs, h):
        u = h % heads_per_ref
        return refs[h // heads_per_ref], slice(u * GLA_DV_HEAD, (u + 1) * GLA_DV_HEAD)

    @pl.when(pl.program_id(0) == 0)
    def _():
        state_ref[...] = jnp.zeros_like(state_ref)

    pre = jnp.dot(lr_ref[...].astype(BF16), wgu_ref[...], preferred_element_type=F32) + bg_ref[...]
    log2_a = (jnp.minimum(pre, 0.0) - jnp.log1p(jnp.exp(-jnp.abs(pre)))) * (LOG2_E / GLA_GATE_TAU)
    tb = min(tm, GLA_CUMSUM_ROWS)
    r = lax.broadcasted_iota(jnp.int32, (tb, tb), 0)
    c = lax.broadcasted_iota(jnp.int32, (tb, tb), 1)
    tri = ((c <= r) & ((c // C) == (r // C))).astype(BF16)
    b_all = jnp.concatenate(
        [sum(jnp.dot(tri, piece, preferred_element_type=F32) for piece in _split3(log2_a[t * tb:(t + 1) * tb]))
         for t in range(tm // tb)], axis=0)
    for h in range(GLA_HEADS):
        kl = slice(h * GLA_DK_HEAD, (h + 1) * GLA_DK_HEAD)
        b_ref[h] = b_all[:, kl]
        q32_ref[h] = q_ref[:, kl].astype(F32) * (GLA_DK_HEAD ** -0.5)
        k32_ref[h] = k_ref[:, kl].astype(F32)

    gout = gout_ref[...]

    out_rows = GLA_OUT_ROWS
    for ci in range(tm // C):
        row0 = ci * C
        rows = slice(row0, row0 + C)
        vls = [slice(h * GLA_DV_HEAD, (h + 1) * GLA_DV_HEAD) for h in range(GLA_HEADS)]
        parts = []
        for h in range(GLA_HEADS):
            v_ref, vl = head_cols((v0_ref, v1_ref), h)
            o_inter, new_state, diag_sums, offs = _gla_scores(b_ref.at[h], q32_ref.at[h], k32_ref.at[h], row0,
                                                              v_ref[rows, vl], state_ref[h])
            state_ref[h] = new_state
            parts.append((o_inter, diag_sums, offs))
        for h in range(GLA_HEADS):
            v_ref, vl = head_cols((v0_ref, v1_ref), h)
            z_ref, zl = head_cols((z0_ref, z1_ref), h)
            o = _gla_output(*parts[h], v_ref[rows, vl])
            o = o * lax.rsqrt(jnp.mean(o * o, axis=-1, keepdims=True) + RMS_EPS) * gout
            z = z_ref[rows, zl].astype(F32)
            y_ref[rows, vls[h]] = (o * (z * jax.nn.sigmoid(z))).astype(BF16)
        if (row0 + C) % out_rows == 0:
            done = slice(row0 + C - out_rows, row0 + C)
            o_ref[done, :] = x_ref[done, :] + jnp.dot(y_ref[done, :], wout_ref[...], preferred_element_type=F32)


def _gla_layer(x, proj, lr, w_gate_up, b_gate, g_out, w_out, *, tm):
    S, D = x.shape
    half = D_INNER // 2
    v_blk, z_blk = 2 * GLA_DK // half, (2 * GLA_DK + D_INNER) // half
    return pl.pallas_call(
        _gla_kernel,
        grid=(S // tm,),
        in_specs=[
            pl.BlockSpec((tm, GLA_DK), lambda m: (m, 0)),
            pl.BlockSpec((tm, GLA_DK), lambda m: (m, 1)),
            pl.BlockSpec((tm, half), lambda m: (m, v_blk)),
            pl.BlockSpec((tm, half), lambda m: (m, v_blk + 1)),
            pl.BlockSpec((tm, half), lambda m: (m, z_blk)),
            pl.BlockSpec((tm, half), lambda m: (m, z_blk + 1)),
            pl.BlockSpec((tm, LANES), lambda m: (m, 0)),
            pl.BlockSpec((tm, D), lambda m: (m, 0)),
            pl.BlockSpec((LANES, GLA_DK), lambda m: (0, 0)),
            pl.BlockSpec((1, GLA_DK), lambda m: (0, 0)),
            pl.BlockSpec((1, GLA_DV_HEAD), lambda m: (0, 0)),
            pl.BlockSpec((D_INNER, D), lambda m: (0, 0)),
        ],
        out_specs=pl.BlockSpec((tm, D), lambda m: (m, 0)),
        out_shape=jax.ShapeDtypeStruct((S, D), F32),
        scratch_shapes=[
            pltpu.VMEM((GLA_HEADS, GLA_DK_HEAD, GLA_DV_HEAD), F32),
            pltpu.VMEM((tm, D_INNER), BF16),
            pltpu.VMEM((GLA_HEADS, tm, GLA_DK_HEAD), F32),
            pltpu.VMEM((GLA_HEADS, tm, GLA_DK_HEAD), F32),
            pltpu.VMEM((GLA_HEADS, tm, GLA_DK_HEAD), F32),
        ],
        compiler_params=pltpu.CompilerParams(
            dimension_semantics=("arbitrary",), vmem_limit_bytes=VMEM_LIMIT),
        name="gla",
    )(proj, proj, proj, proj, proj, proj, lr, x, w_gate_up, b_gate, g_out, w_out)


def _block_rows(ref):
    return ref[...].reshape(-1, ref.shape[-1])


def _dil_attn_kernel(h_ref, wq_ref, wk_ref, wv_ref, o_ref, m_ref, l_ref,
                     q_s, k_s, v_s, bias_ref, m_acc, l_acc, *, slopes, dilation, qb):
    blk = DIL_BLOCK
    i = pl.program_id(1)
    new_rows = slice(blk, (qb + 1) * blk)

    @pl.when((pl.program_id(0) == 0) & (i == 0))
    def _():
        row = lax.broadcasted_iota(jnp.int32, (blk, blk), 0)
        col = lax.broadcasted_iota(jnp.int32, (blk, blk), 1)
        dist_prev = jnp.where(col >= row, ((row + blk - col) * dilation).astype(F32), MASK_DISTANCE)
        dist_cur = jnp.where(col <= row, ((row - col) * dilation).astype(F32), MASK_DISTANCE)
        for h in range(DIL_HEADS):
            bias_ref[h, :, :blk] = -(slopes[h] * LOG2_E) * dist_prev
            bias_ref[h, :, blk:] = -(slopes[h] * LOG2_E) * dist_cur

    def all_heads(first_step):
        hq = hkv = _block_rows(h_ref)
        heads_per_chunk = PROJ_COLS // DIL_HEAD_DIM
        n_chunks = D_INNER // PROJ_COLS
        projected = [0]

        def project_through(chunk):
            while projected[0] <= min(chunk, n_chunks - 1):
                cols = slice(projected[0] * PROJ_COLS, (projected[0] + 1) * PROJ_COLS)
                q_s[:, cols] = jnp.dot(hq, wq_ref[:, cols], preferred_element_type=F32).astype(BF16)
                k_s[new_rows, cols] = jnp.dot(hkv, wk_ref[:, cols], preferred_element_type=F32).astype(BF16)
                v_s[new_rows, cols] = jnp.dot(hkv, wv_ref[:, cols], preferred_element_type=F32).astype(BF16)
                projected[0] += 1

        def key_rows(j):
            return slice((j + 1) * blk, (j + 2) * blk) if first_step and j == 0 else slice(j * blk, (j + 2) * blk)

        def scores(unit):
            h, j = divmod(unit, qb)
            project_through(h // heads_per_chunk + PROJ_LOOKAHEAD)
            hl = slice(h * DIL_HEAD_DIM, (h + 1) * DIL_HEAD_DIM)
            keys = key_rows(j)
            bias = bias_ref[h, :, 2 * blk - (keys.stop - keys.start):]
            q = q_s[j * blk:(j + 1) * blk, hl]
            return lax.dot_general(q, k_s[keys, hl], _NT, preferred_element_type=F32) + bias

        n_units = DIL_HEADS * qb
        ahead = [scores(u) for u in range(SCORE_LOOKAHEAD)]
        for unit in range(n_units):
            h, j = divmod(unit, qb)
            hl = slice(h * DIL_HEAD_DIM, (h + 1) * DIL_HEAD_DIM)
            s = ahead.pop(0)
            if unit + SCORE_LOOKAHEAD < n_units:
                ahead.append(scores(unit + SCORE_LOOKAHEAD))
            m = jnp.max(s, axis=-1, keepdims=True)
            p = jnp.exp2(s - m)
            l = jnp.sum(p, axis=-1, keepdims=True)
            o = jnp.dot(p.astype(BF16), v_s[key_rows(j), hl], preferred_element_type=F32).astype(o_ref.dtype)
            if len(o_ref.shape) == 2:
                o_ref[j * blk:(j + 1) * blk, hl] = o
            else:
                pieces = o_ref.shape[0] // qb
                o_ref[j * pieces:(j + 1) * pieces, :, hl] = o.reshape(pieces, o_ref.shape[1], DIL_HEAD_DIM)
            m_acc[j * blk:(j + 1) * blk, h:h + 1] = m
            l_acc[j * blk:(j + 1) * blk, h:h + 1] = l

    m_acc[...] = jnp.zeros_like(m_acc)
    l_acc[...] = jnp.zeros_like(l_acc)

    @pl.when(i == 0)
    def _():
        all_heads(True)

    @pl.when(i > 0)
    def _():
        all_heads(False)

    m_ref[...] = m_acc[...].reshape(m_ref.shape)
    l_ref[...] = l_acc[...].reshape(l_ref.shape)
    k_s[:blk, :] = k_s[qb * blk:, :]
    v_s[:blk, :] = v_s[qb * blk:, :]


def _dil_attn_group(hv, w_kv, w_q, g, *, S, qb):
    _, d = DIL_GROUPS[g]
    D = hv.shape[-1]
    rows = qb * DIL_BLOCK
    n_steps = S // d // rows
    n_all = N_GROUPS * DIL_HEADS
    slopes = tuple(2.0 ** (-ALIBI_MAX_EXP * (g * DIL_HEADS + h + 1.0) / n_all) for h in range(DIL_HEADS))
    kcol, vcol, qcol = g, N_GROUPS + g, g
    if d == 1:
        h_spec = lambda v: pl.BlockSpec((None, rows, D), lambda r, i: (v, i, 0))
        spec = lambda width: pl.BlockSpec((rows, width), lambda r, i: (i, 0))
        shape = lambda width, dt: jax.ShapeDtypeStruct((S, width), dt)
    else:
        run = PERM_ROWS // d
        pieces = rows // run
        hv = hv.reshape(hv.shape[0], S // PERM_ROWS, d, run, D)
        h_spec = lambda v: pl.BlockSpec((None, pieces, None, run, D), lambda r, i: (v, i, r, 0, 0))
        spec = lambda width: pl.BlockSpec((pieces, None, run, width), lambda r, i: (i, r, 0, 0))
        shape = lambda width, dt: jax.ShapeDtypeStruct((S // PERM_ROWS, d, run, width), dt)
    w_spec = lambda col: pl.BlockSpec((D, D_INNER), lambda r, i: (0, col))
    o, m, l = pl.pallas_call(
        functools.partial(_dil_attn_kernel, slopes=slopes, dilation=d, qb=qb),
        grid=(d, n_steps),
        in_specs=[h_spec(g), w_spec(qcol), w_spec(kcol), w_spec(vcol)],
        out_specs=[spec(D_INNER), spec(LANES), spec(LANES)],
        out_shape=[shape(D_INNER, BF16), shape(LANES, F32), shape(LANES, F32)],
        scratch_shapes=[
            pltpu.VMEM((rows, D_INNER), BF16),
            pltpu.VMEM((rows + DIL_BLOCK, D_INNER), BF16),
            pltpu.VMEM((rows + DIL_BLOCK, D_INNER), BF16),
            pltpu.VMEM((DIL_HEADS, DIL_BLOCK, 2 * DIL_BLOCK), F32),
            pltpu.VMEM((rows, LANES), F32),
            pltpu.VMEM((rows, LANES), F32),
        ],
        compiler_params=pltpu.CompilerParams(
            dimension_semantics=("arbitrary", "arbitrary"), vmem_limit_bytes=VMEM_LIMIT),
        name=f"dil_attn_g{g}",
    )(hv, w_q, w_kv, w_kv)
    return o.reshape(S, D_INNER), m.reshape(S, LANES), l.reshape(S, LANES)


def _combine_kernel(o0_ref, o1_ref, o2_ref, m0_ref, m1_ref, m2_ref, l0_ref, l1_ref, l2_ref, z_ref, x_ref,
                    wout_ref, g_ref, out_ref, y_ref, ot_ref):
    tm = x_ref.shape[0]
    o_refs = (o0_ref, o1_ref, o2_ref)
    m_refs = (m0_ref, m1_ref, m2_ref)
    l_refs = (l0_ref, l1_ref, l2_ref)
    unperms = [None] + [_deinterleave_matrix(d, transpose=True) for _, d in DIL_GROUPS[1:]]

    def to_token_order(b):
        rows = slice(b * PERM_ROWS, (b + 1) * PERM_ROWS)
        ms, ls = [m_refs[0][rows, :]], [l_refs[0][rows, :]]
        for g in range(1, N_GROUPS):
            ot_ref[g - 1, rows, :] = jnp.dot(unperms[g], o_refs[g][rows, :], preferred_element_type=F32)
            for stats, ref in ((ms, m_refs[g]), (ls, l_refs[g])):
                stats.append(sum(jnp.dot(unperms[g], piece, preferred_element_type=F32)
                                 for piece in _split3(ref[rows, :])))
        return ms, ls

    def mix_and_project(rows, ms, ls):
        m = jnp.maximum(jnp.maximum(ms[0], ms[1]), ms[2])
        es = [jnp.exp2(mg - m) for mg in ms]
        den = es[0] * ls[0] + es[1] * ls[1] + es[2] * ls[2]
        lane = lax.broadcasted_iota(jnp.int32, m.shape, 1)
        ws = [jnp.where(lane < DIL_HEADS, e / den, 0.0) for e in es]
        for h in range(DIL_HEADS):
            hl = slice(h * DIL_HEAD_DIM, (h + 1) * DIL_HEAD_DIM)
            acc = ws[0][:, h:h + 1] * o_refs[0][rows, hl].astype(F32)
            for g in range(1, N_GROUPS):
                acc = acc + ws[g][:, h:h + 1] * ot_ref[g - 1, rows, hl]
            z = z_ref[rows, hl].astype(F32)
            y_ref[rows, hl] = (acc * (z * jax.nn.sigmoid(z))).astype(BF16)
        x = x_ref[rows, :] + jnp.dot(y_ref[rows, :], wout_ref[...], preferred_element_type=F32)
        y = x * lax.rsqrt(jnp.mean(x * x, axis=-1, keepdims=True) + RMS_EPS)
        out_ref[rows, :] = y * g_ref[...]

    n_blocks = tm // PERM_ROWS
    sub = COMBINE_SUB_ROWS
    stats = to_token_order(0)
    for b in range(n_blocks):
        nxt = to_token_order(b + 1) if b + 1 < n_blocks else None
        for u in range(PERM_ROWS // sub):
            local = slice(u * sub, (u + 1) * sub)
            rows = slice(b * PERM_ROWS + u * sub, b * PERM_ROWS + (u + 1) * sub)
            mix_and_project(rows, [mg[local] for mg in stats[0]], [lg[local] for lg in stats[1]])
        stats = nxt


def _combine(os_, ms, ls, z, x, w_out, g_final, *, tm):
    S, D = x.shape
    row_blk = lambda m: (m, 0)
    return pl.pallas_call(
        _combine_kernel,
        grid=(S // tm,),
        in_specs=[pl.BlockSpec((tm, D_INNER), row_blk)] * 3 + [pl.BlockSpec((tm, LANES), row_blk)] * 6 + [
            pl.BlockSpec((tm, D_INNER), row_blk),
            pl.BlockSpec((tm, D), row_blk),
            pl.BlockSpec((D_INNER, D), lambda m: (0, 0)),
            pl.BlockSpec((1, D), lambda m: (0, 0)),
        ],
        out_specs=pl.BlockSpec((tm, D), row_blk),
        out_shape=jax.ShapeDtypeStruct((S, D), F32),
        scratch_shapes=[pltpu.VMEM((tm, D_INNER), BF16), pltpu.VMEM((N_GROUPS - 1, tm, D_INNER), F32)],
        compiler_params=pltpu.CompilerParams(
            dimension_semantics=("arbitrary",), vmem_limit_bytes=VMEM_LIMIT),
        name="combine",
    )(*os_, *ms, *ls, z, x, w_out, g_final)


def kernel(x, a_norm, a_w_in, a_w_gate_up, a_b_gate, a_g_out, a_w_out, kv_norm, w_kv, b_norm, b_w_in, b_w_out,
           final_norm):
    B, S, D = x.shape
    assert B == 1 and D == D_MODEL and a_norm.shape[0] == 1 and b_norm.shape[0] == 1
    assert S % (ATTN_BLOCKS_PER_STEP * DIL_BLOCK * max(d for _, d in DIL_GROUPS)) == 0
    x0 = x.reshape(S, D)

    n_a = 2 * GLA_DK + 2 * D_INNER
    w_a = a_w_in[0].astype(BF16)
    w_lr = jnp.pad(w_a[:, n_a:], ((0, 0), (0, LANES - GLA_GATE_RANK)))
    proj_a, lr = _norm_proj(x0, a_norm, [(0, 1)], jnp.zeros((2,), jnp.int32), w_a, w_lr, n_out=n_a,
                            tm=NORM_PROJ_ROWS, tn=n_a // 2)
    w_gu = jnp.pad(a_w_gate_up[0], ((0, LANES - GLA_GATE_RANK), (0, 0))).astype(BF16)
    x1 = _gla_layer(x0, proj_a, lr, w_gu, a_b_gate, a_g_out, a_w_out[0].astype(BF16), tm=GLA_ROWS)

    q_scale = DIL_HEAD_DIM ** -0.5 * LOG2_E
    w_kv_b = _scale_cast(w_kv, kv_norm[None], jnp.zeros((2 * N_GROUPS,), jnp.int32), n_out=w_kv.shape[1], tn=D_INNER)
    w_qz = _scale_cast(b_w_in, jnp.stack([b_norm[0] * q_scale, b_norm[0]]),
                       jnp.array([0] * N_GROUPS + [1], jnp.int32), n_out=b_w_in.shape[2], tn=D_INNER)
    variants = [(-1, d) for _, d in DIL_GROUPS]
    z, hv = _norm_proj(x1, None, variants[:1], jnp.zeros((2,), jnp.int32), w_qz, None, variants,
                       n_out=D_INNER, w_col0=N_GROUPS * D_INNER, tm=NORM_PROJ_ROWS, tn=D_INNER // 2)

    outs = [_dil_attn_group(hv, w_kv_b, w_qz, g, S=S, qb=ATTN_BLOCKS_PER_STEP) for g in range(N_GROUPS)]
    out = _combine([o for o, _, _ in outs], [m for _, m, _ in outs], [l for _, _, l in outs], z, x1,
                   b_w_out[0].astype(BF16), final_norm.reshape(1, D), tm=COMBINE_ROWS)
    return out.reshape(B, S, D)
```

```python
import functools

import jax
import jax.numpy as jnp
from jax import lax
from jax.experimental import pallas as pl
from jax.experimental.pallas import tpu as pltpu

F32 = jnp.float32
BF16 = jnp.bfloat16

RMS_EPS = 1e-6
D_MODEL = 1024
D_INNER = 2048
GLA_HEADS = 4
GLA_DK_HEAD = 128
GLA_DV_HEAD = 512
GLA_DK = GLA_HEADS * GLA_DK_HEAD
GLA_GATE_RANK = 16
GLA_GATE_TAU = 16.0
GLA_CHUNK = 64
GLA_SUB = 16
GLA_CUMSUM_ROWS = 256
GLA_OUT_ROWS = 256
DIL_GROUPS = ((128, 1), (512, 4), (2048, 16))
N_GROUPS = 3
DIL_HEADS = 16
DIL_HEAD_DIM = 128
DIL_BLOCK = 128
ALIBI_MAX_EXP = 8.0
MASK_DISTANCE = 1e34
PERM_ROWS = 256
LOG2_E = 1.4426950408889634
PROJ_COLS = 256
PROJ_LOOKAHEAD = 2
SCORE_LOOKAHEAD = 8
ATTN_BLOCKS_PER_STEP = 4
COMBINE_SUB_ROWS = 128
NORM_PROJ_ROWS = 1024
GLA_ROWS = 512
COMBINE_ROWS = 512

LANES = 128
VMEM_LIMIT = 56 * 1024 * 1024

_NT = (((1,), (1,)), ((), ()))


def _scale_cast_kernel(sel_ref, w_ref, g_ref, o_ref):
    del sel_ref
    g = g_ref[...]
    for j in range(w_ref.shape[1] // LANES):
        cols = slice(j * LANES, (j + 1) * LANES)
        o_ref[:, cols] = (w_ref[:, cols] * g).astype(BF16)


def _scale_cast(w, gains, gain_of_block, *, n_out, tn):
    K = w.shape[-2]
    lead = (None,) * (w.ndim - 2)
    g_lanes = jnp.broadcast_to(gains[:, :, None], (gains.shape[0], K, LANES))
    return pl.pallas_call(
        _scale_cast_kernel,
        grid_spec=pltpu.PrefetchScalarGridSpec(
            num_scalar_prefetch=1, grid=(n_out // tn,),
            in_specs=[pl.BlockSpec(lead + (K, tn), lambda n, sel: (0,) * len(lead) + (0, n)),
                      pl.BlockSpec((None, K, LANES), lambda n, sel: (sel[n], 0, 0))],
            out_specs=pl.BlockSpec((K, tn), lambda n, sel: (0, n))),
        out_shape=jax.ShapeDtypeStruct((K, n_out), BF16),
        compiler_params=pltpu.CompilerParams(dimension_semantics=("arbitrary",), vmem_limit_bytes=VMEM_LIMIT),
        name="scale_cast",
    )(gain_of_block, w, g_lanes)


def _deinterleave_matrix(d, transpose=False):
    n = PERM_ROWS // d
    p = lax.broadcasted_iota(jnp.int32, (PERM_ROWS, PERM_ROWS), 1 if transpose else 0)
    j = lax.broadcasted_iota(jnp.int32, (PERM_ROWS, PERM_ROWS), 0 if transpose else 1)
    return (j == (p % n) * d + p // n).astype(BF16)


def _norm_proj_kernel(var_ref, x_ref, *rest, mm_variants, emit_variants, has_gains, has_aux, n_side):
    rest = list(rest)
    g_ref = rest.pop(0) if has_gains else None
    w_ref = rest.pop(0)
    waux_ref = rest.pop(0) if has_aux else None
    side_in = [(rest.pop(0), rest.pop(0)) for _ in range(n_side)]
    o_ref = rest.pop(0)
    aux_ref = rest.pop(0) if has_aux else None
    hv_ref = rest.pop(0) if emit_variants else None
    side_out = [rest.pop(0) for _ in range(n_side)]
    (h_ref,) = rest
    n = pl.program_id(1)
    tm = x_ref.shape[0]

    @pl.when(n == 0)
    def _():
        x = x_ref[...]
        y = x * lax.rsqrt(jnp.mean(x * x, axis=-1, keepdims=True) + RMS_EPS)
        targets = ([(h_ref, i, v) for i, v in enumerate(mm_variants)]
                   + [(hv_ref, i, v) for i, v in enumerate(emit_variants)])
        for gi in sorted({g for _, _, (g, _) in targets}):
            h = (y if gi < 0 else y * g_ref[gi:gi + 1, :]).astype(BF16)
            for ref, i, (g, d) in targets:
                if g != gi:
                    continue
                if d == 1:
                    ref[i] = h
                else:
                    perm = _deinterleave_matrix(d)
                    for b in range(tm // PERM_ROWS):
                        rows = slice(b * PERM_ROWS, (b + 1) * PERM_ROWS)
                        ref[i, rows, :] = jnp.dot(perm, h[rows], preferred_element_type=F32).astype(BF16)
        if has_aux:
            aux_ref[...] = jnp.dot(h_ref[0], waux_ref[...], preferred_element_type=F32)

    h = h_ref[var_ref[n]]
    o_ref[...] = jnp.dot(h, w_ref[...], preferred_element_type=F32).astype(o_ref.dtype)

    for (sw_ref, sg_ref), so_ref in zip(side_in, side_out):
        g = sg_ref[...]
        for j in range(sw_ref.shape[1] // LANES):
            cols = slice(j * LANES, (j + 1) * LANES)
            so_ref[:, cols] = (sw_ref[:, cols] * g).astype(BF16)


def _norm_proj(x, gains, mm_variants, var_of_block, w, w_aux, emit_variants=(), side=(), *, n_out, w_col0=0, tm, tn):
    S, D = x.shape
    N = n_out
    has_gains = gains is not None
    assert has_gains or all(g < 0 for g, _ in tuple(mm_variants) + tuple(emit_variants))
    has_aux = w_aux is not None
    grid = (S // tm, N // tn)
    w_blk0 = w_col0 // tn
    in_specs = [pl.BlockSpec((tm, D), lambda m, n, var: (m, 0))]
    args = [x]
    if has_gains:
        in_specs.append(pl.BlockSpec(gains.shape, lambda m, n, var: (0, 0)))
        args.append(gains)
    in_specs.append(pl.BlockSpec((D, tn), lambda m, n, var: (0, w_blk0 + n)))
    args.append(w)
    out_shape = [jax.ShapeDtypeStruct((S, N), BF16)]
    out_specs = [pl.BlockSpec((tm, tn), lambda m, n, var: (m, n))]
    if has_aux:
        in_specs.append(pl.BlockSpec((D, LANES), lambda m, n, var: (0, 0)))
        out_shape.append(jax.ShapeDtypeStruct((S, LANES), F32))
        out_specs.append(pl.BlockSpec((tm, LANES), lambda m, n, var: (m, 0)))
        args.append(w_aux)
    if emit_variants:
        out_shape.append(jax.ShapeDtypeStruct((len(emit_variants), S, D), BF16))
        out_specs.append(pl.BlockSpec((len(emit_variants), tm, D), lambda m, n, var: (0, m, 0)))
    n_steps = grid[0] * grid[1]
    step = lambda m, n: m * grid[1] + n
    for sw, sg, split_col in side:
        K, n_cols = sw.shape[-2], sw.shape[-1]
        slab = n_cols // n_steps
        assert slab * n_steps == n_cols and slab % LANES == 0 and (split_col is None or split_col % slab == 0)
        lead = (None,) * (sw.ndim - 2)
        split_blk = n_steps if split_col is None else split_col // slab
        in_specs.append(pl.BlockSpec(lead + (K, slab),
                                     lambda m, n, var, nl=len(lead): (0,) * nl + (0, step(m, n))))
        in_specs.append(pl.BlockSpec(
            (None, K, LANES),
            lambda m, n, var, sb=split_blk, last=sg.shape[0] - 1: (jnp.where(step(m, n) < sb, 0, last), 0, 0)))
        args += [sw, jnp.broadcast_to(sg[:, :, None], (sg.shape[0], K, LANES))]
        out_shape.append(jax.ShapeDtypeStruct((K, n_cols), BF16))
        out_specs.append(pl.BlockSpec((K, slab), lambda m, n, var: (0, step(m, n))))
    return pl.pallas_call(
        functools.partial(_norm_proj_kernel, mm_variants=tuple(mm_variants), emit_variants=tuple(emit_variants),
                          has_gains=has_gains, has_aux=has_aux, n_side=len(side)),
        grid_spec=pltpu.PrefetchScalarGridSpec(
            num_scalar_prefetch=1, grid=grid, in_specs=in_specs, out_specs=out_specs,
            scratch_shapes=[pltpu.VMEM((len(mm_variants), tm, D), BF16)]),
        out_shape=out_shape,
        compiler_params=pltpu.CompilerParams(
            dimension_semantics=("arbitrary", "arbitrary"), vmem_limit_bytes=VMEM_LIMIT),
        name="norm_proj_aux" if has_aux else "norm_proj",
    )(var_of_block, *args)


def _split3(a):
    hi = a.astype(BF16)
    r1 = a - hi.astype(F32)
    mid = r1.astype(BF16)
    lo = (r1 - mid.astype(F32)).astype(BF16)
    return hi, mid, lo


def _gla_scores(b_ref, q_ref, k_ref, row0, v, state):
    C, SUB, HALF = GLA_CHUNK, GLA_SUB, GLA_SUB // 2
    rows = pl.ds(row0, C)
    b, q, k = b_ref[rows, :], q_ref[rows, :], k_ref[rows, :]
    dk = q.shape[1]
    b_last = b_ref[pl.ds(row0 + C - 1, 1), :]
    o_inter = jnp.dot((q * jnp.exp2(b)).astype(BF16), state.astype(BF16), preferred_element_type=F32)
    k_dec = k * jnp.exp2(b_last - b)
    upd = jnp.dot(k_dec.T.astype(BF16), v, preferred_element_type=F32)
    e_col = jnp.broadcast_to(jnp.exp2(b_last), (dk, dk)).T
    new_state = state * jnp.concatenate([e_col] * (v.shape[1] // dk), axis=1) + upd

    ones = jnp.ones((dk, LANES), BF16)
    diag_sums, offs = [], []
    for i in range(C // SUB):
        r0 = i * SUB
        halves = [(b[r0 + u * HALF:r0 + (u + 1) * HALF], q[r0 + u * HALF:r0 + (u + 1) * HALF]) for u in range(2)]
        prods = []
        for s in range(SUB):
            bs = b_ref[pl.ds(row0 + r0 + s, 1), :]
            ks = k_ref[pl.ds(row0 + r0 + s, 1), :]
            for u in range(s // HALF, 2):
                prods.append(halves[u][1] * jnp.exp2(halves[u][0] - bs) * ks)
        diag_sums.append(jnp.dot(jnp.concatenate(prods, axis=0).astype(BF16), ones, preferred_element_type=F32))
        if i > 0:
            b0 = b_ref[pl.ds(row0 + r0, 1), :]
            qt = (q[r0:r0 + SUB] * jnp.exp2(b[r0:r0 + SUB] - b0)).astype(BF16)
            kt = jnp.concatenate([(k[:r0] * jnp.exp2(b0 - b[:r0])).astype(BF16), jnp.zeros((C - r0, dk), BF16)], axis=0)
            offs.append(lax.dot_general(qt, kt, _NT, preferred_element_type=F32))
    return o_inter, new_state, diag_sums, offs


def _gla_output(o_inter, diag_sums, offs, v):
    C, SUB, HALF = GLA_CHUNK, GLA_SUB, GLA_SUB // 2
    lane = lax.broadcasted_iota(jnp.int32, (HALF, LANES), 1)
    row = lax.broadcasted_iota(jnp.int32, (HALF, LANES), 0)
    a_rows = []
    for i in range(C // SUB):
        r0 = i * SUB
        acc = [jnp.zeros((HALF, LANES), F32) for _ in range(2)]
        piece = 0
        for s in range(SUB):
            for u in range(s // HALF, 2):
                acc[u] = jnp.where(lane == r0 + s, diag_sums[i][piece * HALF:(piece + 1) * HALF, :], acc[u])
                piece += 1
        a_i = jnp.concatenate([jnp.where(lane <= row + r0 + u * HALF, acc[u], 0.0) for u in range(2)], axis=0)[:, :C]
        a_rows.append(a_i + offs[i - 1] if i > 0 else a_i)
    a = jnp.concatenate(a_rows, axis=0).astype(BF16)
    return o_inter + jnp.dot(a, v, preferred_element_type=F32)


def _gla_kernel(q_ref, k_ref, v0_ref, v1_ref, z0_ref, z1_ref, lr_ref, x_ref, wgu_ref, bg_ref, gout_ref, wout_ref,
                o_ref, state_ref, y_ref, b_ref, q32_ref, k32_ref):
    tm = q_ref.shape[0]
    C = GLA_CHUNK
    heads_per_ref = v0_ref.shape[1] // GLA_DV_HEAD

    def head_cols(refs, h):
        u = h % heads_per_ref
        return refs[h // heads_per_ref], slice(u * GLA_DV_HEAD, (u + 1) * GLA_DV_HEAD)

    @pl.when(pl.program_id(0) == 0)
    def _():
        state_ref[...] = jnp.zeros_like(state_ref)

    pre = jnp.dot(lr_ref[...].astype(BF16), wgu_ref[...], preferred_element_type=F32) + bg_ref[...]
    log2_a = (jnp.minimum(pre, 0.0) - jnp.log1p(jnp.exp(-jnp.abs(pre)))) * (LOG2_E / GLA_GATE_TAU)
    tb = min(tm, GLA_CUMSUM_ROWS)
    r = lax.broadcasted_iota(jnp.int32, (tb, tb), 0)
    c = lax.broadcasted_iota(jnp.int32, (tb, tb), 1)
    tri = ((c <= r) & ((c // C) == (r // C))).astype(BF16)
    b_all = jnp.concatenate(
        [sum(jnp.dot(tri, piece, preferred_element_type=F32) for piece in _split3(log2_a[t * tb:(t + 1) * tb]))
         for t in range(tm // tb)], axis=0)
    for h in range(GLA_HEADS):
        kl = slice(h * GLA_DK_HEAD, (h + 1) * GLA_DK_HEAD)
        b_ref[h] = b_all[:, kl]
        q32_ref[h] = q_ref[:, kl].astype(F32) * (GLA_DK_HEAD ** -0.5)
        k32_ref[h] = k_ref[:, kl].astype(F32)

    gout = gout_ref[...]

    out_rows = GLA_OUT_ROWS
    for ci in range(tm // C):
        row0 = ci * C
        rows = slice(row0, row0 + C)
        vls = [slice(h * GLA_DV_HEAD, (h + 1) * GLA_DV_HEAD) for h in range(GLA_HEADS)]
        parts = []
        for h in range(GLA_HEADS):
            v_ref, vl = head_cols((v0_ref, v1_ref), h)
            o_inter, new_state, diag_sums, offs = _gla_scores(b_ref.at[h], q32_ref.at[h], k32_ref.at[h], row0,
                                                              v_ref[rows, vl], state_ref[h])
            state_ref[h] = new_state
            parts.append((o_inter, diag_sums, offs))
        for h in range(GLA_HEADS):
            v_ref, vl = head_cols((v0_ref, v1_ref), h)
            z_ref, zl = head_cols((z0_ref, z1_ref), h)
            o = _gla_output(*parts[h], v_ref[rows, vl])
            o = o * lax.rsqrt(jnp.mean(o * o, axis=-1, keepdims=True) + RMS_EPS) * gout
            z = z_ref[rows, zl].astype(F32)
            y_ref[rows, vls[h]] = (o * (z * jax.nn.sigmoid(z))).astype(BF16)
        if (row0 + C) % out_rows == 0:
            done = slice(row0 + C - out_rows, row0 + C)
            o_ref[done, :] = x_ref[done, :] + jnp.dot(y_ref[done, :], wout_ref[...], preferred_element_type=F32)


def _gla_layer(x, proj, lr, w_gate_up, b_gate, g_out, w_out, *, tm):
    S, D = x.shape
    half = D_INNER // 2
    v_blk, z_blk = 2 * GLA_DK // half, (2 * GLA_DK + D_INNER) // half
    return pl.pallas_call(
        _gla_kernel,
        grid=(S // tm,),
        in_specs=[
            pl.BlockSpec((tm, GLA_DK), lambda m: (m, 0)),
            pl.BlockSpec((tm, GLA_DK), lambda m: (m, 1)),
            pl.BlockSpec((tm, half), lambda m: (m, v_blk)),
            pl.BlockSpec((tm, half), lambda m: (m, v_blk + 1)),
            pl.BlockSpec((tm, half), lambda m: (m, z_blk)),
            pl.BlockSpec((tm, half), lambda m: (m, z_blk + 1)),
            pl.BlockSpec((tm, LANES), lambda m: (m, 0)),
            pl.BlockSpec((tm, D), lambda m: (m, 0)),
            pl.BlockSpec((LANES, GLA_DK), lambda m: (0, 0)),
            pl.BlockSpec((1, GLA_DK), lambda m: (0, 0)),
            pl.BlockSpec((1, GLA_DV_HEAD), lambda m: (0, 0)),
            pl.BlockSpec((D_INNER, D), lambda m: (0, 0)),
        ],
        out_specs=pl.BlockSpec((tm, D), lambda m: (m, 0)),
        out_shape=jax.ShapeDtypeStruct((S, D), F32),
        scratch_shapes=[
            pltpu.VMEM((GLA_HEADS, GLA_DK_HEAD, GLA_DV_HEAD), F32),
            pltpu.VMEM((tm, D_INNER), BF16),
            pltpu.VMEM((GLA_HEADS, tm, GLA_DK_HEAD), F32),
            pltpu.VMEM((GLA_HEADS, tm, GLA_DK_HEAD), F32),
            pltpu.VMEM((GLA_HEADS, tm, GLA_DK_HEAD), F32),
        ],
        compiler_params=pltpu.CompilerParams(
            dimension_semantics=("arbitrary",), vmem_limit_bytes=VMEM_LIMIT),
        name="gla",
    )(proj, proj, proj, proj, proj, proj, lr, x, w_gate_up, b_gate, g_out, w_out)


def _block_rows(ref):
    return ref[...].reshape(-1, ref.shape[-1])


def _dil_attn_kernel(h_ref, wq_ref, wk_ref, wv_ref, o_ref, m_ref, l_ref,
                     q_s, k_s, v_s, bias_ref, m_acc, l_acc, *, slopes, dilation, qb):
    blk = DIL_BLOCK
    i = pl.program_id(1)
    new_rows = slice(blk, (qb + 1) * blk)

    @pl.when((pl.program_id(0) == 0) & (i == 0))
    def _():
        row = lax.broadcasted_iota(jnp.int32, (blk, blk), 0)
        col = lax.broadcasted_iota(jnp.int32, (blk, blk), 1)
        dist_prev = jnp.where(col >= row, ((row + blk - col) * dilation).astype(F32), MASK_DISTANCE)
        dist_cur = jnp.where(col <= row, ((row - col) * dilation).astype(F32), MASK_DISTANCE)
        for h in range(DIL_HEADS):
            bias_ref[h, :, :blk] = -(slopes[h] * LOG2_E) * dist_prev
            bias_ref[h, :, blk:] = -(slopes[h] * LOG2_E) * dist_cur

    def all_heads(first_step):
        hq = hkv = _block_rows(h_ref)
        heads_per_chunk = PROJ_COLS // DIL_HEAD_DIM
        n_chunks = D_INNER // PROJ_COLS
        projected = [0]

        def project_through(chunk):
            while projected[0] <= min(chunk, n_chunks - 1):
                cols = slice(projected[0] * PROJ_COLS, (projected[0] + 1) * PROJ_COLS)
                q_s[:, cols] = jnp.dot(hq, wq_ref[:, cols], preferred_element_type=F32).astype(BF16)
                k_s[new_rows, cols] = jnp.dot(hkv, wk_ref[:, cols], preferred_element_type=F32).astype(BF16)
                v_s[new_rows, cols] = jnp.dot(hkv, wv_ref[:, cols], preferred_element_type=F32).astype(BF16)
                projected[0] += 1

        def key_rows(j):
            return slice((j + 1) * blk, (j + 2) * blk) if first_step and j == 0 else slice(j * blk, (j + 2) * blk)

        def scores(unit):
            h, j = divmod(unit, qb)
            project_through(h // heads_per_chunk + PROJ_LOOKAHEAD)
            hl = slice(h * DIL_HEAD_DIM, (h + 1) * DIL_HEAD_DIM)
            keys = key_rows(j)
            bias = bias_ref[h, :, 2 * blk - (keys.stop - keys.start):]
            q = q_s[j * blk:(j + 1) * blk, hl]
            return lax.dot_general(q, k_s[keys, hl], _NT, preferred_element_type=F32) + bias

        n_units = DIL_HEADS * qb
        ahead = [scores(u) for u in range(SCORE_LOOKAHEAD)]
        for unit in range(n_units):
            h, j = divmod(unit, qb)
            hl = slice(h * DIL_HEAD_DIM, (h + 1) * DIL_HEAD_DIM)
            s = ahead.pop(0)
            if unit + SCORE_LOOKAHEAD < n_units:
                ahead.append(scores(unit + SCORE_LOOKAHEAD))
            m = jnp.max(s, axis=-1, keepdims=True)
            p = jnp.exp2(s - m)
            l = jnp.sum(p, axis=-1, keepdims=True)
            o = jnp.dot(p.astype(BF16), v_s[key_rows(j), hl], preferred_element_type=F32).astype(o_ref.dtype)
            if len(o_ref.shape) == 2:
                o_ref[j * blk:(j + 1) * blk, hl] = o
            else:
                pieces = o_ref.shape[0] // qb
                o_ref[j * pieces:(j + 1) * pieces, :, hl] = o.reshape(pieces, o_ref.shape[1], DIL_HEAD_DIM)
            m_acc[j * blk:(j + 1) * blk, h:h + 1] = m
            l_acc[j * blk:(j + 1) * blk, h:h + 1] = l

    m_acc[...] = jnp.zeros_like(m_acc)
    l_acc[...] = jnp.zeros_like(l_acc)

    @pl.when(i == 0)
    def _():
        all_heads(True)

    @pl.when(i > 0)
    def _():
        all_heads(False)

    m_ref[...] = m_acc[...].reshape(m_ref.shape)
    l_ref[...] = l_acc[...].reshape(l_ref.shape)
    k_s[:blk, :] = k_s[qb * blk:, :]
    v_s[:blk, :] = v_s[qb * blk:, :]


def _dil_attn_group(hv, w_kv, w_q, g, *, S, qb):
    _, d = DIL_GROUPS[g]
    D = hv.shape[-1]
    rows = qb * DIL_BLOCK
    n_steps = S // d // rows
    n_all = N_GROUPS * DIL_HEADS
    slopes = tuple(2.0 ** (-ALIBI_MAX_EXP * (g * DIL_HEADS + h + 1.0) / n_all) for h in range(DIL_HEADS))
    kcol, vcol, qcol = g, N_GROUPS + g, g
    if d == 1:
        h_spec = lambda v: pl.BlockSpec((None, rows, D), lambda r, i: (v, i, 0))
        spec = lambda width: pl.BlockSpec((rows, width), lambda r, i: (i, 0))
        shape = lambda width, dt: jax.ShapeDtypeStruct((S, width), dt)
    else:
        run = PERM_ROWS // d
        pieces = rows // run
        hv = hv.reshape(hv.shape[0], S // PERM_ROWS, d, run, D)
        h_spec = lambda v: pl.BlockSpec((None, pieces, None, run, D), lambda r, i: (v, i, r, 0, 0))
        spec = lambda width: pl.BlockSpec((pieces, None, run, width), lambda r, i: (i, r, 0, 0))
        shape = lambda width, dt: jax.ShapeDtypeStruct((S // PERM_ROWS, d, run, width), dt)
    w_spec = lambda col: pl.BlockSpec((D, D_INNER), lambda r, i: (0, col))
    o, m, l = pl.pallas_call(
        functools.partial(_dil_attn_kernel, slopes=slopes, dilation=d, qb=qb),
        grid=(d, n_steps),
        in_specs=[h_spec(g), w_spec(qcol), w_spec(kcol), w_spec(vcol)],
        out_specs=[spec(D_INNER), spec(LANES), spec(LANES)],
        out_shape=[shape(D_INNER, BF16), shape(LANES, F32), shape(LANES, F32)],
        scratch_shapes=[
            pltpu.VMEM((rows, D_INNER), BF16),
            pltpu.VMEM((rows + DIL_BLOCK, D_INNER), BF16),
            pltpu.VMEM((rows + DIL_BLOCK, D_INNER), BF16),
            pltpu.VMEM((DIL_HEADS, DIL_BLOCK, 2 * DIL_BLOCK), F32),
            pltpu.VMEM((rows, LANES), F32),
            pltpu.VMEM((rows, LANES), F32),
        ],
        compiler_params=pltpu.CompilerParams(
            dimension_semantics=("arbitrary", "arbitrary"), vmem_limit_bytes=VMEM_LIMIT),
        name=f"dil_attn_g{g}",
    )(hv, w_q, w_kv, w_kv)
    return o.reshape(S, D_INNER), m.reshape(S, LANES), l.reshape(S, LANES)


def _combine_kernel(o0_ref, o1_ref, o2_ref, m0_ref, m1_ref, m2_ref, l0_ref, l1_ref, l2_ref, z_ref, x_ref,
                    wout_ref, g_ref, out_ref, y_ref, ot_ref):
    tm = x_ref.shape[0]
    o_refs = (o0_ref, o1_ref, o2_ref)
    m_refs = (m0_ref, m1_ref, m2_ref)
    l_refs = (l0_ref, l1_ref, l2_ref)
    unperms = [None] + [_deinterleave_matrix(d, transpose=True) for _, d in DIL_GROUPS[1:]]

    def to_token_order(b):
        rows = slice(b * PERM_ROWS, (b + 1) * PERM_ROWS)
        ms, ls = [m_refs[0][rows, :]], [l_refs[0][rows, :]]
        for g in range(1, N_GROUPS):
            ot_ref[g - 1, rows, :] = jnp.dot(unperms[g], o_refs[g][rows, :], preferred_element_type=F32)
            for stats, ref in ((ms, m_refs[g]), (ls, l_refs[g])):
                stats.append(sum(jnp.dot(unperms[g], piece, preferred_element_type=F32)
                                 for piece in _split3(ref[rows, :])))
        return ms, ls

    def mix_and_project(rows, ms, ls):
        m = jnp.maximum(jnp.maximum(ms[0], ms[1]), ms[2])
        es = [jnp.exp2(mg - m) for mg in ms]
        den = es[0] * ls[0] + es[1] * ls[1] + es[2] * ls[2]
        lane = lax.broadcasted_iota(jnp.int32, m.shape, 1)
        ws = [jnp.where(lane < DIL_HEADS, e / den, 0.0) for e in es]
        for h in range(DIL_HEADS):
            hl = slice(h * DIL_HEAD_DIM, (h + 1) * DIL_HEAD_DIM)
            acc = ws[0][:, h:h + 1] * o_refs[0][rows, hl].astype(F32)
            for g in range(1, N_GROUPS):
                acc = acc + ws[g][:, h:h + 1] * ot_ref[g - 1, rows, hl]
            z = z_ref[rows, hl].astype(F32)
            y_ref[rows, hl] = (acc * (z * jax.nn.sigmoid(z))).astype(BF16)
        x = x_ref[rows, :] + jnp.dot(y_ref[rows, :], wout_ref[...], preferred_element_type=F32)
        y = x * lax.rsqrt(jnp.mean(x * x, axis=-1, keepdims=True) + RMS_EPS)
        out_ref[rows, :] = y * g_ref[...]

    n_blocks = tm // PERM_ROWS
    sub = COMBINE_SUB_ROWS
    stats = to_token_order(0)
    for b in range(n_blocks):
        nxt = to_token_order(b + 1) if b + 1 < n_blocks else None
        for u in range(PERM_ROWS // sub):
            local = slice(u * sub, (u + 1) * sub)
            rows = slice(b * PERM_ROWS + u * sub, b * PERM_ROWS + (u + 1) * sub)
            mix_and_project(rows, [mg[local] for mg in stats[0]], [lg[local] for lg in stats[1]])
        stats = nxt


def _combine(os_, ms, ls, z, x, w_out, g_final, *, tm):
    S, D = x.shape
    row_blk = lambda m: (m, 0)
    return pl.pallas_call(
        _combine_kernel,
        grid=(S // tm,),
        in_specs=[pl.BlockSpec((tm, D_INNER), row_blk)] * 3 + [pl.BlockSpec((tm, LANES), row_blk)] * 6 + [
            pl.BlockSpec((tm, D_INNER), row_blk),
            pl.BlockSpec((tm, D), row_blk),
            pl.BlockSpec((D_INNER, D), lambda m: (0, 0)),
            pl.BlockSpec((1, D), lambda m: (0, 0)),
        ],
        out_specs=pl.BlockSpec((tm, D), row_blk),
        out_shape=jax.ShapeDtypeStruct((S, D), F32),
        scratch_shapes=[pltpu.VMEM((tm, D_INNER), BF16), pltpu.VMEM((N_GROUPS - 1, tm, D_INNER), F32)],
        compiler_params=pltpu.CompilerParams(
            dimension_semantics=("arbitrary",), vmem_limit_bytes=VMEM_LIMIT),
        name="combine",
    )(*os_, *ms, *ls, z, x, w_out, g_final)


def kernel(x, a_norm, a_w_in, a_w_gate_up, a_b_gate, a_g_out, a_w_out, kv_norm, w_kv, b_norm, b_w_in, b_w_out,
           final_norm):
    B, S, D = x.shape
    assert B == 1 and D == D_MODEL and a_norm.shape[0] == 1 and b_norm.shape[0] == 1
    assert S % (ATTN_BLOCKS_PER_STEP * DIL_BLOCK * max(d for _, d in DIL_GROUPS)) == 0
    x0 = x.reshape(S, D)

    n_a = 2 * GLA_DK + 2 * D_INNER
    w_a = a_w_in[0].astype(BF16)
    w_lr = jnp.pad(w_a[:, n_a:], ((0, 0), (0, LANES - GLA_GATE_RANK)))
    q_scale = DIL_HEAD_DIM ** -0.5 * LOG2_E
    side = [(w_kv, kv_norm[None], None),
            (b_w_in, jnp.stack([b_norm[0] * q_scale, b_norm[0]]), N_GROUPS * D_INNER)]
    proj_a, lr, w_kv_b, w_qz = _norm_proj(x0, a_norm, [(0, 1)], jnp.zeros((2,), jnp.int32), w_a, w_lr, (), side,
                                          n_out=n_a, tm=NORM_PROJ_ROWS, tn=n_a // 2)
    w_gu = jnp.pad(a_w_gate_up[0], ((0, LANES - GLA_GATE_RANK), (0, 0))).astype(BF16)
    x1 = _gla_layer(x0, proj_a, lr, w_gu, a_b_gate, a_g_out, a_w_out[0].astype(BF16), tm=GLA_ROWS)

    variants = [(-1, d) for _, d in DIL_GROUPS]
    z, hv = _norm_proj(x1, None, variants[:1], jnp.zeros((2,), jnp.int32), w_qz, None, variants,
                       n_out=D_INNER, w_col0=N_GROUPS * D_INNER, tm=NORM_PROJ_ROWS, tn=D_INNER // 2)

    outs = [_dil_attn_group(hv, w_kv_b, w_qz, g, S=S, qb=ATTN_BLOCKS_PER_STEP) for g in range(N_GROUPS)]
    out = _combine([o for o, _, _ in outs], [m for _, m, _ in outs], [l for _, _, l in outs], z, x1,
                   b_w_out[0].astype(BF16), final_norm.reshape(1, D), tm=COMBINE_ROWS)
    return out.reshape(B, S, D)
```

```python
import functools

import jax
import jax.numpy as jnp
from jax import lax
from jax.experimental import pallas as pl
from jax.experimental.pallas import tpu as pltpu

F32 = jnp.float32
BF16 = jnp.bfloat16

RMS_EPS = 1e-6
D_MODEL = 1024
D_INNER = 2048
GLA_HEADS = 4
GLA_DK_HEAD = 128
GLA_DV_HEAD = 512
GLA_DK = GLA_HEADS * GLA_DK_HEAD
GLA_GATE_RANK = 16
GLA_GATE_TAU = 16.0
GLA_CHUNK = 64
GLA_SUB = 16
GLA_CUMSUM_ROWS = 256
GLA_OUT_ROWS = 256
DIL_GROUPS = ((128, 1), (512, 4), (2048, 16))
N_GROUPS = 3
DIL_HEADS = 16
DIL_HEAD_DIM = 128
DIL_BLOCK = 128
ALIBI_MAX_EXP = 8.0
MASK_DISTANCE = 1e34
PERM_ROWS = 256
LOG2_E = 1.4426950408889634
PROJ_COLS = 256
PROJ_LOOKAHEAD = 2
SCORE_LOOKAHEAD = 8
ATTN_BLOCKS_PER_STEP = 4
COMBINE_SUB_ROWS = 128
NORM_PROJ_ROWS = 1024
GLA_ROWS = 512
COMBINE_ROWS = 512

LANES = 128
VMEM_LIMIT = 56 * 1024 * 1024

_NT = (((1,), (1,)), ((), ()))


def _scale_cast_kernel(sel_ref, w_ref, g_ref, o_ref):
    del sel_ref
    g = g_ref[...]
    for j in range(w_ref.shape[1] // LANES):
        cols = slice(j * LANES, (j + 1) * LANES)
        o_ref[:, cols] = (w_ref[:, cols] * g).astype(BF16)


def _scale_cast(w, gains, gain_of_block, *, n_out, tn):
    K = w.shape[-2]
    lead = (None,) * (w.ndim - 2)
    g_lanes = jnp.broadcast_to(gains[:, :, None], (gains.shape[0], K, LANES))
    return pl.pallas_call(
        _scale_cast_kernel,
        grid_spec=pltpu.PrefetchScalarGridSpec(
            num_scalar_prefetch=1, grid=(n_out // tn,),
            in_specs=[pl.BlockSpec(lead + (K, tn), lambda n, sel: (0,) * len(lead) + (0, n)),
                      pl.BlockSpec((None, K, LANES), lambda n, sel: (sel[n], 0, 0))],
            out_specs=pl.BlockSpec((K, tn), lambda n, sel: (0, n))),
        out_shape=jax.ShapeDtypeStruct((K, n_out), BF16),
        compiler_params=pltpu.CompilerParams(dimension_semantics=("arbitrary",), vmem_limit_bytes=VMEM_LIMIT),
        name="scale_cast",
    )(gain_of_block, w, g_lanes)


def _deinterleave_matrix(d, transpose=False):
    n = PERM_ROWS // d
    p = lax.broadcasted_iota(jnp.int32, (PERM_ROWS, PERM_ROWS), 1 if transpose else 0)
    j = lax.broadcasted_iota(jnp.int32, (PERM_ROWS, PERM_ROWS), 0 if transpose else 1)
    return (j == (p % n) * d + p // n).astype(BF16)


def _run_side_jobs(side_in, side_out):
    for (sw_ref, sg_ref), so_ref in zip(side_in, side_out):
        g = sg_ref[...]
        for j in range(sw_ref.shape[1] // LANES):
            cols = slice(j * LANES, (j + 1) * LANES)
            so_ref[:, cols] = (sw_ref[:, cols] * g).astype(BF16)


def _side_job_specs(side, n_steps, step):
    in_specs, args, out_shape, out_specs = [], [], [], []
    for sw, sg, split_col in side:
        K, n_cols = sw.shape[-2], sw.shape[-1]
        slab = n_cols // n_steps
        assert slab * n_steps == n_cols and slab % LANES == 0 and (split_col is None or split_col % slab == 0)
        lead = (None,) * (sw.ndim - 2)
        split_blk = n_steps if split_col is None else split_col // slab
        in_specs.append(pl.BlockSpec(lead + (K, slab), lambda *i, nl=len(lead): (0,) * nl + (0, step(*i))))
        in_specs.append(pl.BlockSpec(
            (None, K, LANES), lambda *i, sb=split_blk, last=sg.shape[0] - 1: (jnp.where(step(*i) < sb, 0, last), 0, 0)))
        args += [sw, jnp.broadcast_to(sg[:, :, None], (sg.shape[0], K, LANES))]
        out_shape.append(jax.ShapeDtypeStruct((K, n_cols), BF16))
        out_specs.append(pl.BlockSpec((K, slab), lambda *i: (0, step(*i))))
    return in_specs, args, out_shape, out_specs


def _norm_proj_kernel(var_ref, x_ref, *rest, mm_variants, emit_variants, has_gains, has_aux, n_side):
    rest = list(rest)
    g_ref = rest.pop(0) if has_gains else None
    w_ref = rest.pop(0)
    waux_ref = rest.pop(0) if has_aux else None
    side_in = [(rest.pop(0), rest.pop(0)) for _ in range(n_side)]
    o_ref = rest.pop(0)
    aux_ref = rest.pop(0) if has_aux else None
    hv_ref = rest.pop(0) if emit_variants else None
    side_out = [rest.pop(0) for _ in range(n_side)]
    (h_ref,) = rest
    n = pl.program_id(1)
    tm = x_ref.shape[0]

    @pl.when(n == 0)
    def _():
        x = x_ref[...]
        y = x * lax.rsqrt(jnp.mean(x * x, axis=-1, keepdims=True) + RMS_EPS)
        targets = ([(h_ref, i, v) for i, v in enumerate(mm_variants)]
                   + [(hv_ref, i, v) for i, v in enumerate(emit_variants)])
        for gi in sorted({g for _, _, (g, _) in targets}):
            h = (y if gi < 0 else y * g_ref[gi:gi + 1, :]).astype(BF16)
            for ref, i, (g, d) in targets:
                if g != gi:
                    continue
                if d == 1:
                    ref[i] = h
                else:
                    perm = _deinterleave_matrix(d)
                    for b in range(tm // PERM_ROWS):
                        rows = slice(b * PERM_ROWS, (b + 1) * PERM_ROWS)
                        ref[i, rows, :] = jnp.dot(perm, h[rows], preferred_element_type=F32).astype(BF16)
        if has_aux:
            aux_ref[...] = jnp.dot(h_ref[0], waux_ref[...], preferred_element_type=F32)

    h = h_ref[var_ref[n]]
    o_ref[...] = jnp.dot(h, w_ref[...], preferred_element_type=F32).astype(o_ref.dtype)

    _run_side_jobs(side_in, side_out)


def _norm_proj(x, gains, mm_variants, var_of_block, w, w_aux, emit_variants=(), side=(), *, n_out, w_col0=0, tm, tn):
    S, D = x.shape
    N = n_out
    has_gains = gains is not None
    assert has_gains or all(g < 0 for g, _ in tuple(mm_variants) + tuple(emit_variants))
    has_aux = w_aux is not None
    grid = (S // tm, N // tn)
    w_blk0 = w_col0 // tn
    in_specs = [pl.BlockSpec((tm, D), lambda m, n, var: (m, 0))]
    args = [x]
    if has_gains:
        in_specs.append(pl.BlockSpec(gains.shape, lambda m, n, var: (0, 0)))
        args.append(gains)
    in_specs.append(pl.BlockSpec((D, tn), lambda m, n, var: (0, w_blk0 + n)))
    args.append(w)
    out_shape = [jax.ShapeDtypeStruct((S, N), BF16)]
    out_specs = [pl.BlockSpec((tm, tn), lambda m, n, var: (m, n))]
    if has_aux:
        in_specs.append(pl.BlockSpec((D, LANES), lambda m, n, var: (0, 0)))
        out_shape.append(jax.ShapeDtypeStruct((S, LANES), F32))
        out_specs.append(pl.BlockSpec((tm, LANES), lambda m, n, var: (m, 0)))
        args.append(w_aux)
    if emit_variants:
        out_shape.append(jax.ShapeDtypeStruct((len(emit_variants), S, D), BF16))
        out_specs.append(pl.BlockSpec((len(emit_variants), tm, D), lambda m, n, var: (0, m, 0)))
    side_in, side_args, side_shape, side_out = _side_job_specs(side, grid[0] * grid[1],
                                                               lambda m, n, var: m * grid[1] + n)
    in_specs += side_in
    args += side_args
    out_shape += side_shape
    out_specs += side_out
    return pl.pallas_call(
        functools.partial(_norm_proj_kernel, mm_variants=tuple(mm_variants), emit_variants=tuple(emit_variants),
                          has_gains=has_gains, has_aux=has_aux, n_side=len(side)),
        grid_spec=pltpu.PrefetchScalarGridSpec(
            num_scalar_prefetch=1, grid=grid, in_specs=in_specs, out_specs=out_specs,
            scratch_shapes=[pltpu.VMEM((len(mm_variants), tm, D), BF16)]),
        out_shape=out_shape,
        compiler_params=pltpu.CompilerParams(
            dimension_semantics=("arbitrary", "arbitrary"), vmem_limit_bytes=VMEM_LIMIT),
        name="norm_proj_aux" if has_aux else "norm_proj",
    )(var_of_block, *args)


def _split3(a):
    hi = a.astype(BF16)
    r1 = a - hi.astype(F32)
    mid = r1.astype(BF16)
    lo = (r1 - mid.astype(F32)).astype(BF16)
    return hi, mid, lo


def _gla_scores(b_ref, q_ref, k_ref, row0, v, state):
    C, SUB, HALF = GLA_CHUNK, GLA_SUB, GLA_SUB // 2
    rows = pl.ds(row0, C)
    b, q, k = b_ref[rows, :], q_ref[rows, :], k_ref[rows, :]
    dk = q.shape[1]
    b_last = b_ref[pl.ds(row0 + C - 1, 1), :]
    o_inter = jnp.dot((q * jnp.exp2(b)).astype(BF16), state.astype(BF16), preferred_element_type=F32)
    k_dec = k * jnp.exp2(b_last - b)
    upd = jnp.dot(k_dec.T.astype(BF16), v, preferred_element_type=F32)
    e_col = jnp.broadcast_to(jnp.exp2(b_last), (dk, dk)).T
    new_state = state * jnp.concatenate([e_col] * (v.shape[1] // dk), axis=1) + upd

    ones = jnp.ones((dk, LANES), BF16)
    diag_sums, offs = [], []
    for i in range(C // SUB):
        r0 = i * SUB
        halves = [(b[r0 + u * HALF:r0 + (u + 1) * HALF], q[r0 + u * HALF:r0 + (u + 1) * HALF]) for u in range(2)]
        prods = []
        for s in range(SUB):
            bs = b_ref[pl.ds(row0 + r0 + s, 1), :]
            ks = k_ref[pl.ds(row0 + r0 + s, 1), :]
            for u in range(s // HALF, 2):
                prods.append(halves[u][1] * jnp.exp2(halves[u][0] - bs) * ks)
        diag_sums.append(jnp.dot(jnp.concatenate(prods, axis=0).astype(BF16), ones, preferred_element_type=F32))
        if i > 0:
            b0 = b_ref[pl.ds(row0 + r0, 1), :]
            qt = (q[r0:r0 + SUB] * jnp.exp2(b[r0:r0 + SUB] - b0)).astype(BF16)
            kt = jnp.concatenate([(k[:r0] * jnp.exp2(b0 - b[:r0])).astype(BF16), jnp.zeros((C - r0, dk), BF16)], axis=0)
            offs.append(lax.dot_general(qt, kt, _NT, preferred_element_type=F32))
    return o_inter, new_state, diag_sums, offs


def _gla_output(o_inter, diag_sums, offs, v):
    C, SUB, HALF = GLA_CHUNK, GLA_SUB, GLA_SUB // 2
    lane = lax.broadcasted_iota(jnp.int32, (HALF, LANES), 1)
    row = lax.broadcasted_iota(jnp.int32, (HALF, LANES), 0)
    a_rows = []
    for i in range(C // SUB):
        r0 = i * SUB
        acc = [jnp.zeros((HALF, LANES), F32) for _ in range(2)]
        piece = 0
        for s in range(SUB):
            for u in range(s // HALF, 2):
                acc[u] = jnp.where(lane == r0 + s, diag_sums[i][piece * HALF:(piece + 1) * HALF, :], acc[u])
                piece += 1
        a_i = jnp.concatenate([jnp.where(lane <= row + r0 + u * HALF, acc[u], 0.0) for u in range(2)], axis=0)[:, :C]
        a_rows.append(a_i + offs[i - 1] if i > 0 else a_i)
    a = jnp.concatenate(a_rows, axis=0).astype(BF16)
    return o_inter + jnp.dot(a, v, preferred_element_type=F32)


def _gla_kernel(q_ref, k_ref, v0_ref, v1_ref, z0_ref, z1_ref, lr_ref, x_ref, wgu_ref, bg_ref, gout_ref, wout_ref,
                *rest, n_side):
    side_in = [(rest[2 * i], rest[2 * i + 1]) for i in range(n_side)]
    o_ref, *side_out = rest[2 * n_side:3 * n_side + 1]
    state_ref, y_ref, b_ref, q32_ref, k32_ref = rest[3 * n_side + 1:]
    _run_side_jobs(side_in, side_out)
    tm = q_ref.shape[0]
    C = GLA_CHUNK
    heads_per_ref = v0_ref.shape[1] // GLA_DV_HEAD

    def head_cols(refs, h):
        u = h % heads_per_ref
        return refs[h // heads_per_ref], slice(u * GLA_DV_HEAD, (u + 1) * GLA_DV_HEAD)

    @pl.when(pl.program_id(0) == 0)
    def _():
        state_ref[...] = jnp.zeros_like(state_ref)

    pre = jnp.dot(lr_ref[...].astype(BF16), wgu_ref[...], preferred_element_type=F32) + bg_ref[...]
    log2_a = (jnp.minimum(pre, 0.0) - jnp.log1p(jnp.exp(-jnp.abs(pre)))) * (LOG2_E / GLA_GATE_TAU)
    tb = min(tm, GLA_CUMSUM_ROWS)
    r = lax.broadcasted_iota(jnp.int32, (tb, tb), 0)
    c = lax.broadcasted_iota(jnp.int32, (tb, tb), 1)
    tri = ((c <= r) & ((c // C) == (r // C))).astype(BF16)
    b_all = jnp.concatenate(
        [sum(jnp.dot(tri, piece, preferred_element_type=F32) for piece in _split3(log2_a[t * tb:(t + 1) * tb]))
         for t in range(tm // tb)], axis=0)
    for h in range(GLA_HEADS):
        kl = slice(h * GLA_DK_HEAD, (h + 1) * GLA_DK_HEAD)
        b_ref[h] = b_all[:, kl]
        q32_ref[h] = q_ref[:, kl].astype(F32) * (GLA_DK_HEAD ** -0.5)
        k32_ref[h] = k_ref[:, kl].astype(F32)

    gout = gout_ref[...]

    out_rows = GLA_OUT_ROWS
    for ci in range(tm // C):
        row0 = ci * C
        rows = slice(row0, row0 + C)
        vls = [slice(h * GLA_DV_HEAD, (h + 1) * GLA_DV_HEAD) for h in range(GLA_HEADS)]
        parts = []
        for h in range(GLA_HEADS):
            v_ref, vl = head_cols((v0_ref, v1_ref), h)
            o_inter, new_state, diag_sums, offs = _gla_scores(b_ref.at[h], q32_ref.at[h], k32_ref.at[h], row0,
                                                              v_ref[rows, vl], state_ref[h])
            state_ref[h] = new_state
            parts.append((o_inter, diag_sums, offs))
        for h in range(GLA_HEADS):
            v_ref, vl = head_cols((v0_ref, v1_ref), h)
            z_ref, zl = head_cols((z0_ref, z1_ref), h)
            o = _gla_output(*parts[h], v_ref[rows, vl])
            o = o * lax.rsqrt(jnp.mean(o * o, axis=-1, keepdims=True) + RMS_EPS) * gout
            z = z_ref[rows, zl].astype(F32)
            y_ref[rows, vls[h]] = (o * (z * jax.nn.sigmoid(z))).astype(BF16)
        if (row0 + C) % out_rows == 0:
            done = slice(row0 + C - out_rows, row0 + C)
            o_ref[done, :] = x_ref[done, :] + jnp.dot(y_ref[done, :], wout_ref[...], preferred_element_type=F32)


def _gla_layer(x, proj, lr, w_gate_up, b_gate, g_out, w_out, side=(), *, tm):
    S, D = x.shape
    side_in, side_args, side_shape, side_out = _side_job_specs(side, S // tm, lambda m: m)
    half = D_INNER // 2
    v_blk, z_blk = 2 * GLA_DK // half, (2 * GLA_DK + D_INNER) // half
    return pl.pallas_call(
        functools.partial(_gla_kernel, n_side=len(side)),
        grid=(S // tm,),
        in_specs=[
            pl.BlockSpec((tm, GLA_DK), lambda m: (m, 0)),
            pl.BlockSpec((tm, GLA_DK), lambda m: (m, 1)),
            pl.BlockSpec((tm, half), lambda m: (m, v_blk)),
            pl.BlockSpec((tm, half), lambda m: (m, v_blk + 1)),
            pl.BlockSpec((tm, half), lambda m: (m, z_blk)),
            pl.BlockSpec((tm, half), lambda m: (m, z_blk + 1)),
            pl.BlockSpec((tm, LANES), lambda m: (m, 0)),
            pl.BlockSpec((tm, D), lambda m: (m, 0)),
            pl.BlockSpec((LANES, GLA_DK), lambda m: (0, 0)),
            pl.BlockSpec((1, GLA_DK), lambda m: (0, 0)),
            pl.BlockSpec((1, GLA_DV_HEAD), lambda m: (0, 0)),
            pl.BlockSpec((D_INNER, D), lambda m: (0, 0)),
        ] + side_in,
        out_specs=[pl.BlockSpec((tm, D), lambda m: (m, 0))] + side_out,
        out_shape=[jax.ShapeDtypeStruct((S, D), F32)] + side_shape,
        scratch_shapes=[
            pltpu.VMEM((GLA_HEADS, GLA_DK_HEAD, GLA_DV_HEAD), F32),
            pltpu.VMEM((tm, D_INNER), BF16),
            pltpu.VMEM((GLA_HEADS, tm, GLA_DK_HEAD), F32),
            pltpu.VMEM((GLA_HEADS, tm, GLA_DK_HEAD), F32),
            pltpu.VMEM((GLA_HEADS, tm, GLA_DK_HEAD), F32),
        ],
        compiler_params=pltpu.CompilerParams(
            dimension_semantics=("arbitrary",), vmem_limit_bytes=VMEM_LIMIT),
        name="gla",
    )(proj, proj, proj, proj, proj, proj, lr, x, w_gate_up, b_gate, g_out, w_out, *side_args)


def _block_rows(ref):
    return ref[...].reshape(-1, ref.shape[-1])


def _dil_attn_kernel(h_ref, wq_ref, wk_ref, wv_ref, o_ref, m_ref, l_ref,
                     q_s, k_s, v_s, bias_ref, m_acc, l_acc, *, slopes, dilation, qb):
    blk = DIL_BLOCK
    i = pl.program_id(1)
    new_rows = slice(blk, (qb + 1) * blk)

    @pl.when((pl.program_id(0) == 0) & (i == 0))
    def _():
        row = lax.broadcasted_iota(jnp.int32, (blk, blk), 0)
        col = lax.broadcasted_iota(jnp.int32, (blk, blk), 1)
        dist_prev = jnp.where(col >= row, ((row + blk - col) * dilation).astype(F32), MASK_DISTANCE)
        dist_cur = jnp.where(col <= row, ((row - col) * dilation).astype(F32), MASK_DISTANCE)
        for h in range(DIL_HEADS):
            bias_ref[h, :, :blk] = -(slopes[h] * LOG2_E) * dist_prev
            bias_ref[h, :, blk:] = -(slopes[h] * LOG2_E) * dist_cur

    def all_heads(first_step):
        hq = hkv = _block_rows(h_ref)
        heads_per_chunk = PROJ_COLS // DIL_HEAD_DIM
        n_chunks = D_INNER // PROJ_COLS
        projected = [0]

        def project_through(chunk):
            while projected[0] <= min(chunk, n_chunks - 1):
                cols = slice(projected[0] * PROJ_COLS, (projected[0] + 1) * PROJ_COLS)
                q_s[:, cols] = jnp.dot(hq, wq_ref[:, cols], preferred_element_type=F32).astype(BF16)
                k_s[new_rows, cols] = jnp.dot(hkv, wk_ref[:, cols], preferred_element_type=F32).astype(BF16)
                v_s[new_rows, cols] = jnp.dot(hkv, wv_ref[:, cols], preferred_element_type=F32).astype(BF16)
                projected[0] += 1

        def key_rows(j):
            return slice((j + 1) * blk, (j + 2) * blk) if first_step and j == 0 else slice(j * blk, (j + 2) * blk)

        def scores(unit):
            h, j = divmod(unit, qb)
            project_through(h // heads_per_chunk + PROJ_LOOKAHEAD)
            hl = slice(h * DIL_HEAD_DIM, (h + 1) * DIL_HEAD_DIM)
            keys = key_rows(j)
            bias = bias_ref[h, :, 2 * blk - (keys.stop - keys.start):]
            q = q_s[j * blk:(j + 1) * blk, hl]
            return lax.dot_general(q, k_s[keys, hl], _NT, preferred_element_type=F32) + bias

        n_units = DIL_HEADS * qb
        ahead = [scores(u) for u in range(SCORE_LOOKAHEAD)]
        for unit in range(n_units):
            h, j = divmod(unit, qb)
            hl = slice(h * DIL_HEAD_DIM, (h + 1) * DIL_HEAD_DIM)
            s = ahead.pop(0)
            if unit + SCORE_LOOKAHEAD < n_units:
                ahead.append(scores(unit + SCORE_LOOKAHEAD))
            m = jnp.max(s, axis=-1, keepdims=True)
            p = jnp.exp2(s - m)
            l = jnp.sum(p, axis=-1, keepdims=True)
            o = jnp.dot(p.astype(BF16), v_s[key_rows(j), hl], preferred_element_type=F32).astype(o_ref.dtype)
            if len(o_ref.shape) == 2:
                o_ref[j * blk:(j + 1) * blk, hl] = o
            else:
                pieces = o_ref.shape[0] // qb
                o_ref[j * pieces:(j + 1) * pieces, :, hl] = o.reshape(pieces, o_ref.shape[1], DIL_HEAD_DIM)
            m_acc[j * blk:(j + 1) * blk, h:h + 1] = m
            l_acc[j * blk:(j + 1) * blk, h:h + 1] = l

    m_acc[...] = jnp.zeros_like(m_acc)
    l_acc[...] = jnp.zeros_like(l_acc)

    @pl.when(i == 0)
    def _():
        all_heads(True)

    @pl.when(i > 0)
    def _():
        all_heads(False)

    m_ref[...] = m_acc[...].reshape(m_ref.shape)
    l_ref[...] = l_acc[...].reshape(l_ref.shape)
    k_s[:blk, :] = k_s[qb * blk:, :]
    v_s[:blk, :] = v_s[qb * blk:, :]


def _dil_attn_group(hv, w_kv, w_q, g, *, S, qb):
    _, d = DIL_GROUPS[g]
    D = hv.shape[-1]
    rows = qb * DIL_BLOCK
    n_steps = S // d // rows
    n_all = N_GROUPS * DIL_HEADS
    slopes = tuple(2.0 ** (-ALIBI_MAX_EXP * (g * DIL_HEADS + h + 1.0) / n_all) for h in range(DIL_HEADS))
    kcol, vcol, qcol = g, N_GROUPS + g, g
    if d == 1:
        h_spec = lambda v: pl.BlockSpec((None, rows, D), lambda r, i: (v, i, 0))
        spec = lambda width: pl.BlockSpec((rows, width), lambda r, i: (i, 0))
        shape = lambda width, dt: jax.ShapeDtypeStruct((S, width), dt)
    else:
        run = PERM_ROWS // d
        pieces = rows // run
        hv = hv.reshape(hv.shape[0], S // PERM_ROWS, d, run, D)
        h_spec = lambda v: pl.BlockSpec((None, pieces, None, run, D), lambda r, i: (v, i, r, 0, 0))
        spec = lambda width: pl.BlockSpec((pieces, None, run, width), lambda r, i: (i, r, 0, 0))
        shape = lambda width, dt: jax.ShapeDtypeStruct((S // PERM_ROWS, d, run, width), dt)
    w_spec = lambda col: pl.BlockSpec((D, D_INNER), lambda r, i: (0, col))
    o, m, l = pl.pallas_call(
        functools.partial(_dil_attn_kernel, slopes=slopes, dilation=d, qb=qb),
        grid=(d, n_steps),
        in_specs=[h_spec(g), w_spec(qcol), w_spec(kcol), w_spec(vcol)],
        out_specs=[spec(D_INNER), spec(LANES), spec(LANES)],
        out_shape=[shape(D_INNER, BF16), shape(LANES, F32), shape(LANES, F32)],
        scratch_shapes=[
            pltpu.VMEM((rows, D_INNER), BF16),
            pltpu.VMEM((rows + DIL_BLOCK, D_INNER), BF16),
            pltpu.VMEM((rows + DIL_BLOCK, D_INNER), BF16),
            pltpu.VMEM((DIL_HEADS, DIL_BLOCK, 2 * DIL_BLOCK), F32),
            pltpu.VMEM((rows, LANES), F32),
            pltpu.VMEM((rows, LANES), F32),
        ],
        compiler_params=pltpu.CompilerParams(
            dimension_semantics=("arbitrary", "arbitrary"), vmem_limit_bytes=VMEM_LIMIT),
        name=f"dil_attn_g{g}",
    )(hv, w_q, w_kv, w_kv)
    return o.reshape(S, D_INNER), m.reshape(S, LANES), l.reshape(S, LANES)


def _combine_kernel(o0_ref, o1_ref, o2_ref, m0_ref, m1_ref, m2_ref, l0_ref, l1_ref, l2_ref, z_ref, x_ref,
                    wout_ref, g_ref, out_ref, y_ref, ot_ref):
    tm = x_ref.shape[0]
    o_refs = (o0_ref, o1_ref, o2_ref)
    m_refs = (m0_ref, m1_ref, m2_ref)
    l_refs = (l0_ref, l1_ref, l2_ref)
    unperms = [None] + [_deinterleave_matrix(d, transpose=True) for _, d in DIL_GROUPS[1:]]

    def to_token_order(b):
        rows = slice(b * PERM_ROWS, (b + 1) * PERM_ROWS)
        ms, ls = [m_refs[0][rows, :]], [l_refs[0][rows, :]]
        for g in range(1, N_GROUPS):
            ot_ref[g - 1, rows, :] = jnp.dot(unperms[g], o_refs[g][rows, :], preferred_element_type=F32)
            for stats, ref in ((ms, m_refs[g]), (ls, l_refs[g])):
                stats.append(sum(jnp.dot(unperms[g], piece, preferred_element_type=F32)
                                 for piece in _split3(ref[rows, :])))
        return ms, ls

    def mix_and_project(rows, ms, ls):
        m = jnp.maximum(jnp.maximum(ms[0], ms[1]), ms[2])
        es = [jnp.exp2(mg - m) for mg in ms]
        den = es[0] * ls[0] + es[1] * ls[1] + es[2] * ls[2]
        lane = lax.broadcasted_iota(jnp.int32, m.shape, 1)
        ws = [jnp.where(lane < DIL_HEADS, e / den, 0.0) for e in es]
        for h in range(DIL_HEADS):
            hl = slice(h * DIL_HEAD_DIM, (h + 1) * DIL_HEAD_DIM)
            acc = ws[0][:, h:h + 1] * o_refs[0][rows, hl].astype(F32)
            for g in range(1, N_GROUPS):
                acc = acc + ws[g][:, h:h + 1] * ot_ref[g - 1, rows, hl]
            z = z_ref[rows, hl].astype(F32)
            y_ref[rows, hl] = (acc * (z * jax.nn.sigmoid(z))).astype(BF16)
        x = x_ref[rows, :] + jnp.dot(y_ref[rows, :], wout_ref[...], preferred_element_type=F32)
        y = x * lax.rsqrt(jnp.mean(x * x, axis=-1, keepdims=True) + RMS_EPS)
        out_ref[rows, :] = y * g_ref[...]

    n_blocks = tm // PERM_ROWS
    sub = COMBINE_SUB_ROWS
    stats = to_token_order(0)
    for b in range(n_blocks):
        nxt = to_token_order(b + 1) if b + 1 < n_blocks else None
        for u in range(PERM_ROWS // sub):
            local = slice(u * sub, (u + 1) * sub)
            rows = slice(b * PERM_ROWS + u * sub, b * PERM_ROWS + (u + 1) * sub)
            mix_and_project(rows, [mg[local] for mg in stats[0]], [lg[local] for lg in stats[1]])
        stats = nxt


def _combine(os_, ms, ls, z, x, w_out, g_final, *, tm):
    S, D = x.shape
    row_blk = lambda m: (m, 0)
    return pl.pallas_call(
        _combine_kernel,
        grid=(S // tm,),
        in_specs=[pl.BlockSpec((tm, D_INNER), row_blk)] * 3 + [pl.BlockSpec((tm, LANES), row_blk)] * 6 + [
            pl.BlockSpec((tm, D_INNER), row_blk),
            pl.BlockSpec((tm, D), row_blk),
            pl.BlockSpec((D_INNER, D), lambda m: (0, 0)),
            pl.BlockSpec((1, D), lambda m: (0, 0)),
        ],
        out_specs=pl.BlockSpec((tm, D), row_blk),
        out_shape=jax.ShapeDtypeStruct((S, D), F32),
        scratch_shapes=[pltpu.VMEM((tm, D_INNER), BF16), pltpu.VMEM((N_GROUPS - 1, tm, D_INNER), F32)],
        compiler_params=pltpu.CompilerParams(
            dimension_semantics=("arbitrary",), vmem_limit_bytes=VMEM_LIMIT),
        name="combine",
    )(*os_, *ms, *ls, z, x, w_out, g_final)


def kernel(x, a_norm, a_w_in, a_w_gate_up, a_b_gate, a_g_out, a_w_out, kv_norm, w_kv, b_norm, b_w_in, b_w_out,
           final_norm):
    B, S, D = x.shape
    assert B == 1 and D == D_MODEL and a_norm.shape[0] == 1 and b_norm.shape[0] == 1
    assert S % (ATTN_BLOCKS_PER_STEP * DIL_BLOCK * max(d for _, d in DIL_GROUPS)) == 0
    x0 = x.reshape(S, D)

    n_a = 2 * GLA_DK + 2 * D_INNER
    w_a = a_w_in[0].astype(BF16)
    w_lr = jnp.pad(w_a[:, n_a:], ((0, 0), (0, LANES - GLA_GATE_RANK)))
    proj_a, lr = _norm_proj(x0, a_norm, [(0, 1)], jnp.zeros((2,), jnp.int32), w_a, w_lr, n_out=n_a,
                            tm=NORM_PROJ_ROWS, tn=n_a // 2)
    w_gu = jnp.pad(a_w_gate_up[0], ((0, LANES - GLA_GATE_RANK), (0, 0))).astype(BF16)
    q_scale = DIL_HEAD_DIM ** -0.5 * LOG2_E
    side = [(w_kv, kv_norm[None], None),
            (b_w_in, jnp.stack([b_norm[0] * q_scale, b_norm[0]]), N_GROUPS * D_INNER)]
    x1, w_kv_b, w_qz = _gla_layer(x0, proj_a, lr, w_gu, a_b_gate, a_g_out, a_w_out[0].astype(BF16), side, tm=GLA_ROWS)

    variants = [(-1, d) for _, d in DIL_GROUPS]
    z, hv = _norm_proj(x1, None, variants[:1], jnp.zeros((2,), jnp.int32), w_qz, None, variants,
                       n_out=D_INNER, w_col0=N_GROUPS * D_INNER, tm=NORM_PROJ_ROWS, tn=D_INNER // 2)

    outs = [_dil_attn_group(hv, w_kv_b, w_qz, g, S=S, qb=ATTN_BLOCKS_PER_STEP) for g in range(N_GROUPS)]
    out = _combine([o for o, _, _ in outs], [m for _, m, _ in outs], [l for _, _, l in outs], z, x1,
                   b_w_out[0].astype(BF16), final_norm.reshape(1, D), tm=COMBINE_ROWS)
    return out.reshape(B, S, D)
```

```python
import functools

import jax
import jax.numpy as jnp
from jax import lax
from jax.experimental import pallas as pl
from jax.experimental.pallas import tpu as pltpu

F32 = jnp.float32
BF16 = jnp.bfloat16

RMS_EPS = 1e-6
D_MODEL = 1024
D_INNER = 2048
GLA_HEADS = 4
GLA_DK_HEAD = 128
GLA_DV_HEAD = 512
GLA_DK = GLA_HEADS * GLA_DK_HEAD
GLA_GATE_RANK = 16
GLA_GATE_TAU = 16.0
GLA_CHUNK = 64
GLA_SUB = 16
GLA_CUMSUM_ROWS = 256
GLA_OUT_ROWS = 256
DIL_GROUPS = ((128, 1), (512, 4), (2048, 16))
N_GROUPS = 3
DIL_HEADS = 16
DIL_HEAD_DIM = 128
DIL_BLOCK = 128
ALIBI_MAX_EXP = 8.0
MASK_DISTANCE = 1e34
PERM_ROWS = 256
LOG2_E = 1.4426950408889634
PROJ_COLS = 256
PROJ_LOOKAHEAD = 2
SCORE_LOOKAHEAD = 8
ATTN_BLOCKS_PER_STEP = 4
COMBINE_SUB_ROWS = 128
NORM_PROJ_ROWS = 1024
GLA_ROWS = 512
COMBINE_ROWS = 512

LANES = 128
VMEM_LIMIT = 56 * 1024 * 1024

_NT = (((1,), (1,)), ((), ()))


def _deinterleave_matrix(d, transpose=False):
    n = PERM_ROWS // d
    p = lax.broadcasted_iota(jnp.int32, (PERM_ROWS, PERM_ROWS), 1 if transpose else 0)
    j = lax.broadcasted_iota(jnp.int32, (PERM_ROWS, PERM_ROWS), 0 if transpose else 1)
    return (j == (p % n) * d + p // n).astype(BF16)


def _run_side_jobs(side_in, side_out):
    for (sw_ref, sg_ref), so_ref in zip(side_in, side_out):
        g = sg_ref[...]
        for j in range(sw_ref.shape[1] // LANES):
            cols = slice(j * LANES, (j + 1) * LANES)
            so_ref[:, cols] = (sw_ref[:, cols] * g).astype(BF16)


def _side_job_specs(side, n_steps, step):
    in_specs, args, out_shape, out_specs = [], [], [], []
    for sw, sg, split_col in side:
        K, n_cols = sw.shape[-2], sw.shape[-1]
        slab = n_cols // n_steps
        assert slab * n_steps == n_cols and slab % LANES == 0 and (split_col is None or split_col % slab == 0)
        lead = (None,) * (sw.ndim - 2)
        split_blk = n_steps if split_col is None else split_col // slab
        in_specs.append(pl.BlockSpec(lead + (K, slab), lambda *i, nl=len(lead): (0,) * nl + (0, step(*i))))
        in_specs.append(pl.BlockSpec(
            (None, K, LANES), lambda *i, sb=split_blk, last=sg.shape[0] - 1: (jnp.where(step(*i) < sb, 0, last), 0, 0)))
        args += [sw, jnp.broadcast_to(sg[:, :, None], (sg.shape[0], K, LANES))]
        out_shape.append(jax.ShapeDtypeStruct((K, n_cols), BF16))
        out_specs.append(pl.BlockSpec((K, slab), lambda *i: (0, step(*i))))
    return in_specs, args, out_shape, out_specs


def _norm_proj_kernel(var_ref, x_ref, *rest, mm_variants, emit_variants, has_gains, has_aux, n_side):
    rest = list(rest)
    g_ref = rest.pop(0) if has_gains else None
    w_ref = rest.pop(0)
    waux_ref = rest.pop(0) if has_aux else None
    side_in = [(rest.pop(0), rest.pop(0)) for _ in range(n_side)]
    o_ref = rest.pop(0)
    aux_ref = rest.pop(0) if has_aux else None
    hv_ref = rest.pop(0) if emit_variants else None
    side_out = [rest.pop(0) for _ in range(n_side)]
    (h_ref,) = rest
    n = pl.program_id(1)
    tm = x_ref.shape[0]

    @pl.when(n == 0)
    def _():
        x = x_ref[...]
        y = x * lax.rsqrt(jnp.mean(x * x, axis=-1, keepdims=True) + RMS_EPS)
        targets = ([(h_ref, i, v) for i, v in enumerate(mm_variants)]
                   + [(hv_ref, i, v) for i, v in enumerate(emit_variants)])
        for gi in sorted({g for _, _, (g, _) in targets}):
            h = (y if gi < 0 else y * g_ref[gi:gi + 1, :]).astype(BF16)
            for ref, i, (g, d) in targets:
                if g != gi:
                    continue
                if d == 1:
                    ref[i] = h
                else:
                    perm = _deinterleave_matrix(d)
                    for b in range(tm // PERM_ROWS):
                        rows = slice(b * PERM_ROWS, (b + 1) * PERM_ROWS)
                        ref[i, rows, :] = jnp.dot(perm, h[rows], preferred_element_type=F32).astype(BF16)
        if has_aux:
            aux_ref[...] = jnp.dot(h_ref[0], waux_ref[...], preferred_element_type=F32)

    h = h_ref[var_ref[n]]
    o_ref[...] = jnp.dot(h, w_ref[...], preferred_element_type=F32).astype(o_ref.dtype)

    _run_side_jobs(side_in, side_out)


def _norm_proj(x, gains, mm_variants, var_of_block, w, w_aux, emit_variants=(), side=(), *, n_out, w_col0=0, tm, tn):
    S, D = x.shape
    N = n_out
    has_gains = gains is not None
    assert has_gains or all(g < 0 for g, _ in tuple(mm_variants) + tuple(emit_variants))
    has_aux = w_aux is not None
    grid = (S // tm, N // tn)
    w_blk0 = w_col0 // tn
    in_specs = [pl.BlockSpec((tm, D), lambda m, n, var: (m, 0))]
    args = [x]
    if has_gains:
        in_specs.append(pl.BlockSpec(gains.shape, lambda m, n, var: (0, 0)))
        args.append(gains)
    in_specs.append(pl.BlockSpec((D, tn), lambda m, n, var: (0, w_blk0 + n)))
    args.append(w)
    out_shape = [jax.ShapeDtypeStruct((S, N), BF16)]
    out_specs = [pl.BlockSpec((tm, tn), lambda m, n, var: (m, n))]
    if has_aux:
        in_specs.append(pl.BlockSpec((D, LANES), lambda m, n, var: (0, 0)))
        out_shape.append(jax.ShapeDtypeStruct((S, LANES), F32))
        out_specs.append(pl.BlockSpec((tm, LANES), lambda m, n, var: (m, 0)))
        args.append(w_aux)
    if emit_variants:
        out_shape.append(jax.ShapeDtypeStruct((len(emit_variants), S, D), BF16))
        out_specs.append(pl.BlockSpec((len(emit_variants), tm, D), lambda m, n, var: (0, m, 0)))
    side_in, side_args, side_shape, side_out = _side_job_specs(side, grid[0] * grid[1],
                                                               lambda m, n, var: m * grid[1] + n)
    in_specs += side_in
    args += side_args
    out_shape += side_shape
    out_specs += side_out
    return pl.pallas_call(
        functools.partial(_norm_proj_kernel, mm_variants=tuple(mm_variants), emit_variants=tuple(emit_variants),
                          has_gains=has_gains, has_aux=has_aux, n_side=len(side)),
        grid_spec=pltpu.PrefetchScalarGridSpec(
            num_scalar_prefetch=1, grid=grid, in_specs=in_specs, out_specs=out_specs,
            scratch_shapes=[pltpu.VMEM((len(mm_variants), tm, D), BF16)]),
        out_shape=out_shape,
        compiler_params=pltpu.CompilerParams(
            dimension_semantics=("arbitrary", "arbitrary"), vmem_limit_bytes=VMEM_LIMIT),
        name="norm_proj_aux" if has_aux else "norm_proj",
    )(var_of_block, *args)


def _split3(a):
    hi = a.astype(BF16)
    r1 = a - hi.astype(F32)
    mid = r1.astype(BF16)
    lo = (r1 - mid.astype(F32)).astype(BF16)
    return hi, mid, lo


def _gla_scores(b_ref, q_ref, k_ref, row0, v, state):
    C, SUB, HALF = GLA_CHUNK, GLA_SUB, GLA_SUB // 2
    rows = pl.ds(row0, C)
    b, q, k = b_ref[rows, :], q_ref[rows, :], k_ref[rows, :]
    dk = q.shape[1]
    b_last = b_ref[pl.ds(row0 + C - 1, 1), :]
    o_inter = jnp.dot((q * jnp.exp2(b)).astype(BF16), state.astype(BF16), preferred_element_type=F32)
    k_dec = k * jnp.exp2(b_last - b)
    upd = jnp.dot(k_dec.T.astype(BF16), v, preferred_element_type=F32)
    e_col = jnp.broadcast_to(jnp.exp2(b_last), (dk, dk)).T
    new_state = state * jnp.concatenate([e_col] * (v.shape[1] // dk), axis=1) + upd

    ones = jnp.ones((dk, LANES), BF16)
    diag_sums, offs = [], []
    for i in range(C // SUB):
        r0 = i * SUB
        halves = [(b[r0 + u * HALF:r0 + (u + 1) * HALF], q[r0 + u * HALF:r0 + (u + 1) * HALF]) for u in range(2)]
        prods = []
        for s in range(SUB):
            bs = b_ref[pl.ds(row0 + r0 + s, 1), :]
            ks = k_ref[pl.ds(row0 + r0 + s, 1), :]
            for u in range(s // HALF, 2):
                prods.append(halves[u][1] * jnp.exp2(halves[u][0] - bs) * ks)
        diag_sums.append(jnp.dot(jnp.concatenate(prods, axis=0).astype(BF16), ones, preferred_element_type=F32))
        if i > 0:
            b0 = b_ref[pl.ds(row0 + r0, 1), :]
            qt = (q[r0:r0 + SUB] * jnp.exp2(b[r0:r0 + SUB] - b0)).astype(BF16)
            kt = jnp.concatenate([(k[:r0] * jnp.exp2(b0 - b[:r0])).astype(BF16), jnp.zeros((C - r0, dk), BF16)], axis=0)
            offs.append(lax.dot_general(qt, kt, _NT, preferred_element_type=F32))
    return o_inter, new_state, diag_sums, offs


def _gla_output(o_inter, diag_sums, offs, v):
    C, SUB, HALF = GLA_CHUNK, GLA_SUB, GLA_SUB // 2
    lane = lax.broadcasted_iota(jnp.int32, (HALF, LANES), 1)
    row = lax.broadcasted_iota(jnp.int32, (HALF, LANES), 0)
    a_rows = []
    for i in range(C // SUB):
        r0 = i * SUB
        acc = [jnp.zeros((HALF, LANES), F32) for _ in range(2)]
        piece = 0
        for s in range(SUB):
            for u in range(s // HALF, 2):
                acc[u] = jnp.where(lane == r0 + s, diag_sums[i][piece * HALF:(piece + 1) * HALF, :], acc[u])
                piece += 1
        a_i = jnp.concatenate([jnp.where(lane <= row + r0 + u * HALF, acc[u], 0.0) for u in range(2)], axis=0)[:, :C]
        a_rows.append(a_i + offs[i - 1] if i > 0 else a_i)
    a = jnp.concatenate(a_rows, axis=0).astype(BF16)
    return o_inter + jnp.dot(a, v, preferred_element_type=F32)


def _gla_kernel(q_ref, k_ref, v0_ref, v1_ref, z0_ref, z1_ref, lr_ref, x_ref, wgu_ref, bg_ref, gout_ref, wout_ref,
                *rest, n_side):
    side_in = [(rest[2 * i], rest[2 * i + 1]) for i in range(n_side)]
    o_ref, *side_out = rest[2 * n_side:3 * n_side + 1]
    state_ref, y_ref, b_ref, q32_ref, k32_ref = rest[3 * n_side + 1:]
    _run_side_jobs(side_in, side_out)
    tm = q_ref.shape[0]
    C = GLA_CHUNK
    heads_per_ref = v0_ref.shape[1] // GLA_DV_HEAD

    def head_cols(refs, h):
        u = h % heads_per_ref
        return refs[h // heads_per_ref], slice(u * GLA_DV_HEAD, (u + 1) * GLA_DV_HEAD)

    @pl.when(pl.program_id(0) == 0)
    def _():
        state_ref[...] = jnp.zeros_like(state_ref)

    pre = jnp.dot(lr_ref[...].astype(BF16), wgu_ref[...], preferred_element_type=F32) + bg_ref[...]
    log2_a = (jnp.minimum(pre, 0.0) - jnp.log1p(jnp.exp(-jnp.abs(pre)))) * (LOG2_E / GLA_GATE_TAU)
    tb = min(tm, GLA_CUMSUM_ROWS)
    r = lax.broadcasted_iota(jnp.int32, (tb, tb), 0)
    c = lax.broadcasted_iota(jnp.int32, (tb, tb), 1)
    tri = ((c <= r) & ((c // C) == (r // C))).astype(BF16)
    b_all = jnp.concatenate(
        [sum(jnp.dot(tri, piece, preferred_element_type=F32) for piece in _split3(log2_a[t * tb:(t + 1) * tb]))
         for t in range(tm // tb)], axis=0)
    for h in range(GLA_HEADS):
        kl = slice(h * GLA_DK_HEAD, (h + 1) * GLA_DK_HEAD)
        b_ref[h] = b_all[:, kl]
        q32_ref[h] = q_ref[:, kl].astype(F32) * (GLA_DK_HEAD ** -0.5)
        k32_ref[h] = k_ref[:, kl].astype(F32)

    gout = gout_ref[...]

    out_rows = GLA_OUT_ROWS
    for ci in range(tm // C):
        row0 = ci * C
        rows = slice(row0, row0 + C)
        vls = [slice(h * GLA_DV_HEAD, (h + 1) * GLA_DV_HEAD) for h in range(GLA_HEADS)]
        parts = []
        for h in range(GLA_HEADS):
            v_ref, vl = head_cols((v0_ref, v1_ref), h)
            o_inter, new_state, diag_sums, offs = _gla_scores(b_ref.at[h], q32_ref.at[h], k32_ref.at[h], row0,
                                                              v_ref[rows, vl], state_ref[h])
            state_ref[h] = new_state
            parts.append((o_inter, diag_sums, offs))
        for h in range(GLA_HEADS):
            v_ref, vl = head_cols((v0_ref, v1_ref), h)
            z_ref, zl = head_cols((z0_ref, z1_ref), h)
            o = _gla_output(*parts[h], v_ref[rows, vl])
            o = o * lax.rsqrt(jnp.mean(o * o, axis=-1, keepdims=True) + RMS_EPS) * gout
            z = z_ref[rows, zl].astype(F32)
            y_ref[rows, vls[h]] = (o * (z * jax.nn.sigmoid(z))).astype(BF16)
        if (row0 + C) % out_rows == 0:
            done = slice(row0 + C - out_rows, row0 + C)
            o_ref[done, :] = x_ref[done, :] + jnp.dot(y_ref[done, :], wout_ref[...], preferred_element_type=F32)


def _gla_layer(x, proj, lr, w_gate_up, b_gate, g_out, w_out, side=(), *, tm):
    S, D = x.shape
    side_in, side_args, side_shape, side_out = _side_job_specs(side, S // tm, lambda m: m)
    half = D_INNER // 2
    v_blk, z_blk = 2 * GLA_DK // half, (2 * GLA_DK + D_INNER) // half
    return pl.pallas_call(
        functools.partial(_gla_kernel, n_side=len(side)),
        grid=(S // tm,),
        in_specs=[
            pl.BlockSpec((tm, GLA_DK), lambda m: (m, 0)),
            pl.BlockSpec((tm, GLA_DK), lambda m: (m, 1)),
            pl.BlockSpec((tm, half), lambda m: (m, v_blk)),
            pl.BlockSpec((tm, half), lambda m: (m, v_blk + 1)),
            pl.BlockSpec((tm, half), lambda m: (m, z_blk)),
            pl.BlockSpec((tm, half), lambda m: (m, z_blk + 1)),
            pl.BlockSpec((tm, LANES), lambda m: (m, 0)),
            pl.BlockSpec((tm, D), lambda m: (m, 0)),
            pl.BlockSpec((LANES, GLA_DK), lambda m: (0, 0)),
            pl.BlockSpec((1, GLA_DK), lambda m: (0, 0)),
            pl.BlockSpec((1, GLA_DV_HEAD), lambda m: (0, 0)),
            pl.BlockSpec((D_INNER, D), lambda m: (0, 0)),
        ] + side_in,
        out_specs=[pl.BlockSpec((tm, D), lambda m: (m, 0))] + side_out,
        out_shape=[jax.ShapeDtypeStruct((S, D), F32)] + side_shape,
        scratch_shapes=[
            pltpu.VMEM((GLA_HEADS, GLA_DK_HEAD, GLA_DV_HEAD), F32),
            pltpu.VMEM((tm, D_INNER), BF16),
            pltpu.VMEM((GLA_HEADS, tm, GLA_DK_HEAD), F32),
            pltpu.VMEM((GLA_HEADS, tm, GLA_DK_HEAD), F32),
            pltpu.VMEM((GLA_HEADS, tm, GLA_DK_HEAD), F32),
        ],
        compiler_params=pltpu.CompilerParams(
            dimension_semantics=("arbitrary",), vmem_limit_bytes=VMEM_LIMIT),
        name="gla",
    )(proj, proj, proj, proj, proj, proj, lr, x, w_gate_up, b_gate, g_out, w_out, *side_args)


def _block_rows(ref):
    return ref[...].reshape(-1, ref.shape[-1])


def _dil_attn_kernel(h_ref, wq_ref, wk_ref, wv_ref, o_ref, m_ref, l_ref,
                     q_s, k_s, v_s, bias_ref, m_acc, l_acc, *, slopes, dilation, qb):
    blk = DIL_BLOCK
    i = pl.program_id(1)
    new_rows = slice(blk, (qb + 1) * blk)

    @pl.when((pl.program_id(0) == 0) & (i == 0))
    def _():
        row = lax.broadcasted_iota(jnp.int32, (blk, blk), 0)
        col = lax.broadcasted_iota(jnp.int32, (blk, blk), 1)
        dist_prev = jnp.where(col >= row, ((row + blk - col) * dilation).astype(F32), MASK_DISTANCE)
        dist_cur = jnp.where(col <= row, ((row - col) * dilation).astype(F32), MASK_DISTANCE)
        for h in range(DIL_HEADS):
            bias_ref[h, :, :blk] = -(slopes[h] * LOG2_E) * dist_prev
            bias_ref[h, :, blk:] = -(slopes[h] * LOG2_E) * dist_cur

    def all_heads(first_step):
        hq = hkv = _block_rows(h_ref)
        heads_per_chunk = PROJ_COLS // DIL_HEAD_DIM
        n_chunks = D_INNER // PROJ_COLS
        projected = [0]

        def project_through(chunk):
            while projected[0] <= min(chunk, n_chunks - 1):
                cols = slice(projected[0] * PROJ_COLS, (projected[0] + 1) * PROJ_COLS)
                q_s[:, cols] = jnp.dot(hq, wq_ref[:, cols], preferred_element_type=F32).astype(BF16)
                k_s[new_rows, cols] = jnp.dot(hkv, wk_ref[:, cols], preferred_element_type=F32).astype(BF16)
                v_s[new_rows, cols] = jnp.dot(hkv, wv_ref[:, cols], preferred_element_type=F32).astype(BF16)
                projected[0] += 1

        def key_rows(j):
            return slice((j + 1) * blk, (j + 2) * blk) if first_step and j == 0 else slice(j * blk, (j + 2) * blk)

        def scores(unit):
            h, j = divmod(unit, qb)
            project_through(h // heads_per_chunk + PROJ_LOOKAHEAD)
            hl = slice(h * DIL_HEAD_DIM, (h + 1) * DIL_HEAD_DIM)
            keys = key_rows(j)
            bias = bias_ref[h, :, 2 * blk - (keys.stop - keys.start):]
            q = q_s[j * blk:(j + 1) * blk, hl]
            return lax.dot_general(q, k_s[keys, hl], _NT, preferred_element_type=F32) + bias

        n_units = DIL_HEADS * qb
        ahead = [scores(u) for u in range(SCORE_LOOKAHEAD)]
        for unit in range(n_units):
            h, j = divmod(unit, qb)
            hl = slice(h * DIL_HEAD_DIM, (h + 1) * DIL_HEAD_DIM)
            s = ahead.pop(0)
            if unit + SCORE_LOOKAHEAD < n_units:
                ahead.append(scores(unit + SCORE_LOOKAHEAD))
            m = jnp.max(s, axis=-1, keepdims=True)
            p = jnp.exp2(s - m)
            l = jnp.sum(p, axis=-1, keepdims=True)
            o = jnp.dot(p.astype(BF16), v_s[key_rows(j), hl], preferred_element_type=F32).astype(o_ref.dtype)
            if len(o_ref.shape) == 2:
                o_ref[j * blk:(j + 1) * blk, hl] = o
            else:
                pieces = o_ref.shape[0] // qb
                o_ref[j * pieces:(j + 1) * pieces, :, hl] = o.reshape(pieces, o_ref.shape[1], DIL_HEAD_DIM)
            m_acc[j * blk:(j + 1) * blk, h:h + 1] = m
            l_acc[j * blk:(j + 1) * blk, h:h + 1] = l

    m_acc[...] = jnp.zeros_like(m_acc)
    l_acc[...] = jnp.zeros_like(l_acc)

    @pl.when(i == 0)
    def _():
        all_heads(True)

    @pl.when(i > 0)
    def _():
        all_heads(False)

    m_ref[...] = m_acc[...].reshape(m_ref.shape)
    l_ref[...] = l_acc[...].reshape(l_ref.shape)
    k_s[:blk, :] = k_s[qb * blk:, :]
    v_s[:blk, :] = v_s[qb * blk:, :]


def _dil_attn_group(hv, w_kv, w_q, g, *, S, qb):
    _, d = DIL_GROUPS[g]
    D = hv.shape[-1]
    rows = qb * DIL_BLOCK
    n_steps = S // d // rows
    n_all = N_GROUPS * DIL_HEADS
    slopes = tuple(2.0 ** (-ALIBI_MAX_EXP * (g * DIL_HEADS + h + 1.0) / n_all) for h in range(DIL_HEADS))
    kcol, vcol, qcol = g, N_GROUPS + g, g
    if d == 1:
        h_spec = lambda v: pl.BlockSpec((None, rows, D), lambda r, i: (v, i, 0))
        spec = lambda width: pl.BlockSpec((rows, width), lambda r, i: (i, 0))
        shape = lambda width, dt: jax.ShapeDtypeStruct((S, width), dt)
    else:
        run = PERM_ROWS // d
        pieces = rows // run
        hv = hv.reshape(hv.shape[0], S // PERM_ROWS, d, run, D)
        h_spec = lambda v: pl.BlockSpec((None, pieces, None, run, D), lambda r, i: (v, i, r, 0, 0))
        spec = lambda width: pl.BlockSpec((pieces, None, run, width), lambda r, i: (i, r, 0, 0))
        shape = lambda width, dt: jax.ShapeDtypeStruct((S // PERM_ROWS, d, run, width), dt)
    w_spec = lambda col: pl.BlockSpec((D, D_INNER), lambda r, i: (0, col))
    o, m, l = pl.pallas_call(
        functools.partial(_dil_attn_kernel, slopes=slopes, dilation=d, qb=qb),
        grid=(d, n_steps),
        in_specs=[h_spec(g), w_spec(qcol), w_spec(kcol), w_spec(vcol)],
        out_specs=[spec(D_INNER), spec(LANES), spec(LANES)],
        out_shape=[shape(D_INNER, BF16), shape(LANES, F32), shape(LANES, F32)],
        scratch_shapes=[
            pltpu.VMEM((rows, D_INNER), BF16),
            pltpu.VMEM((rows + DIL_BLOCK, D_INNER), BF16),
            pltpu.VMEM((rows + DIL_BLOCK, D_INNER), BF16),
            pltpu.VMEM((DIL_HEADS, DIL_BLOCK, 2 * DIL_BLOCK), F32),
            pltpu.VMEM((rows, LANES), F32),
            pltpu.VMEM((rows, LANES), F32),
        ],
        compiler_params=pltpu.CompilerParams(
            dimension_semantics=("arbitrary", "arbitrary"), vmem_limit_bytes=VMEM_LIMIT),
        name=f"dil_attn_g{g}",
    )(hv, w_q, w_kv, w_kv)
    return o.reshape(S, D_INNER), m.reshape(S, LANES), l.reshape(S, LANES)


def _combine_kernel(o0_ref, o1_ref, o2_ref, m0_ref, m1_ref, m2_ref, l0_ref, l1_ref, l2_ref, z_ref, x_ref,
                    wout_ref, g_ref, out_ref, y_ref, ot_ref):
    tm = x_ref.shape[0]
    o_refs = (o0_ref, o1_ref, o2_ref)
    m_refs = (m0_ref, m1_ref, m2_ref)
    l_refs = (l0_ref, l1_ref, l2_ref)
    unperms = [None] + [_deinterleave_matrix(d, transpose=True) for _, d in DIL_GROUPS[1:]]

    def to_token_order(b):
        rows = slice(b * PERM_ROWS, (b + 1) * PERM_ROWS)
        ms, ls = [m_refs[0][rows, :]], [l_refs[0][rows, :]]
        for g in range(1, N_GROUPS):
            ot_ref[g - 1, rows, :] = jnp.dot(unperms[g], o_refs[g][rows, :], preferred_element_type=F32)
            for stats, ref in ((ms, m_refs[g]), (ls, l_refs[g])):
                stats.append(sum(jnp.dot(unperms[g], piece, preferred_element_type=F32)
                                 for piece in _split3(ref[rows, :])))
        return ms, ls

    def mix_and_project(rows, ms, ls):
        m = jnp.maximum(jnp.maximum(ms[0], ms[1]), ms[2])
        es = [jnp.exp2(mg - m) for mg in ms]
        den = es[0] * ls[0] + es[1] * ls[1] + es[2] * ls[2]
        lane = lax.broadcasted_iota(jnp.int32, m.shape, 1)
        ws = [jnp.where(lane < DIL_HEADS, e / den, 0.0) for e in es]
        for h in range(DIL_HEADS):
            hl = slice(h * DIL_HEAD_DIM, (h + 1) * DIL_HEAD_DIM)
            acc = ws[0][:, h:h + 1] * o_refs[0][rows, hl].astype(F32)
            for g in range(1, N_GROUPS):
                acc = acc + ws[g][:, h:h + 1] * ot_ref[g - 1, rows, hl]
            z = z_ref[rows, hl].astype(F32)
            y_ref[rows, hl] = (acc * (z * jax.nn.sigmoid(z))).astype(BF16)
        x = x_ref[rows, :] + jnp.dot(y_ref[rows, :], wout_ref[...], preferred_element_type=F32)
        y = x * lax.rsqrt(jnp.mean(x * x, axis=-1, keepdims=True) + RMS_EPS)
        out_ref[rows, :] = y * g_ref[...]

    n_blocks = tm // PERM_ROWS
    sub = COMBINE_SUB_ROWS
    stats = to_token_order(0)
    for b in range(n_blocks):
        nxt = to_token_order(b + 1) if b + 1 < n_blocks else None
        for u in range(PERM_ROWS // sub):
            local = slice(u * sub, (u + 1) * sub)
            rows = slice(b * PERM_ROWS + u * sub, b * PERM_ROWS + (u + 1) * sub)
            mix_and_project(rows, [mg[local] for mg in stats[0]], [lg[local] for lg in stats[1]])
        stats = nxt


def _combine(os_, ms, ls, z, x, w_out, g_final, *, tm):
    S, D = x.shape
    row_blk = lambda m: (m, 0)
    return pl.pallas_call(
        _combine_kernel,
        grid=(S // tm,),
        in_specs=[pl.BlockSpec((tm, D_INNER), row_blk)] * 3 + [pl.BlockSpec((tm, LANES), row_blk)] * 6 + [
            pl.BlockSpec((tm, D_INNER), row_blk),
            pl.BlockSpec((tm, D), row_blk),
            pl.BlockSpec((D_INNER, D), lambda m: (0, 0)),
            pl.BlockSpec((1, D), lambda m: (0, 0)),
        ],
        out_specs=pl.BlockSpec((tm, D), row_blk),
        out_shape=jax.ShapeDtypeStruct((S, D), F32),
        scratch_shapes=[pltpu.VMEM((tm, D_INNER), BF16), pltpu.VMEM((N_GROUPS - 1, tm, D_INNER), F32)],
        compiler_params=pltpu.CompilerParams(
            dimension_semantics=("arbitrary",), vmem_limit_bytes=VMEM_LIMIT),
        name="combine",
    )(*os_, *ms, *ls, z, x, w_out, g_final)


def kernel(x, a_norm, a_w_in, a_w_gate_up, a_b_gate, a_g_out, a_w_out, kv_norm, w_kv, b_norm, b_w_in, b_w_out,
           final_norm):
    B, S, D = x.shape
    assert B == 1 and D == D_MODEL and a_norm.shape[0] == 1 and b_norm.shape[0] == 1
    assert S % (ATTN_BLOCKS_PER_STEP * DIL_BLOCK * max(d for _, d in DIL_GROUPS)) == 0
    x0 = x.reshape(S, D)

    n_a = 2 * GLA_DK + 2 * D_INNER
    w_a = a_w_in[0].astype(BF16)
    w_lr = jnp.pad(w_a[:, n_a:], ((0, 0), (0, LANES - GLA_GATE_RANK)))
    proj_a, lr = _norm_proj(x0, a_norm, [(0, 1)], jnp.zeros((2,), jnp.int32), w_a, w_lr, n_out=n_a,
                            tm=NORM_PROJ_ROWS, tn=n_a // 2)
    w_gu = jnp.pad(a_w_gate_up[0], ((0, LANES - GLA_GATE_RANK), (0, 0))).astype(BF16)
    q_scale = DIL_HEAD_DIM ** -0.5 * LOG2_E
    side = [(w_kv, kv_norm[None], None),
            (b_w_in, jnp.stack([b_norm[0] * q_scale, b_norm[0]]), N_GROUPS * D_INNER)]
    x1, w_kv_b, w_qz = _gla_layer(x0, proj_a, lr, w_gu, a_b_gate, a_g_out, a_w_out[0].astype(BF16), side, tm=GLA_ROWS)

    variants = [(-1, d) for _, d in DIL_GROUPS]
    z, hv = _norm_proj(x1, None, variants[:1], jnp.zeros((2,), jnp.int32), w_qz, None, variants,
                       n_out=D_INNER, w_col0=N_GROUPS * D_INNER, tm=NORM_PROJ_ROWS, tn=D_INNER // 2)

    outs = [_dil_attn_group(hv, w_kv_b, w_qz, g, S=S, qb=ATTN_BLOCKS_PER_STEP) for g in range(N_GROUPS)]
    out = _combine([o for o, _, _ in outs], [m for _, m, _ in outs], [l for _, _, l in outs], z, x1,
                   b_w_out[0].astype(BF16), final_norm.reshape(1, D), tm=COMBINE_ROWS)
    return out.reshape(B, S, D)
```

```python
import functools

import jax
import jax.numpy as jnp
from jax import lax
from jax.experimental import pallas as pl
from jax.experimental.pallas import tpu as pltpu

F32 = jnp.float32
BF16 = jnp.bfloat16

RMS_EPS = 1e-6
D_MODEL = 1024
D_INNER = 2048
GLA_HEADS = 4
GLA_DK_HEAD = 128
GLA_DV_HEAD = 512
GLA_DK = GLA_HEADS * GLA_DK_HEAD
GLA_GATE_RANK = 16
GLA_GATE_TAU = 16.0
GLA_CHUNK = 64
GLA_SUB = 16
GLA_CUMSUM_ROWS = 256
GLA_OUT_ROWS = 256
GLA_UNITS_AHEAD = 1
DIL_GROUPS = ((128, 1), (512, 4), (2048, 16))
N_GROUPS = 3
DIL_HEADS = 16
DIL_HEAD_DIM = 128
DIL_BLOCK = 128
ALIBI_MAX_EXP = 8.0
MASK_DISTANCE = 1e34
PERM_ROWS = 256
LOG2_E = 1.4426950408889634
PROJ_COLS = 256
PROJ_LOOKAHEAD = 2
SCORE_LOOKAHEAD = 8
ATTN_BLOCKS_PER_STEP = 4
COMBINE_SUB_ROWS = 128
NORM_PROJ_ROWS = 1024
GLA_ROWS = 512
COMBINE_ROWS = 512

LANES = 128
VMEM_LIMIT = 56 * 1024 * 1024

_NT = (((1,), (1,)), ((), ()))


def _deinterleave_matrix(d, transpose=False):
    n = PERM_ROWS // d
    p = lax.broadcasted_iota(jnp.int32, (PERM_ROWS, PERM_ROWS), 1 if transpose else 0)
    j = lax.broadcasted_iota(jnp.int32, (PERM_ROWS, PERM_ROWS), 0 if transpose else 1)
    return (j == (p % n) * d + p // n).astype(BF16)


def _run_side_jobs(side_in, side_out):
    for (sw_ref, sg_ref), so_ref in zip(side_in, side_out):
        g = sg_ref[...]
        for j in range(sw_ref.shape[1] // LANES):
            cols = slice(j * LANES, (j + 1) * LANES)
            so_ref[:, cols] = (sw_ref[:, cols] * g).astype(BF16)


def _side_job_specs(side, n_steps, step):
    in_specs, args, out_shape, out_specs = [], [], [], []
    for sw, sg, split_col in side:
        K, n_cols = sw.shape[-2], sw.shape[-1]
        slab = n_cols // n_steps
        assert slab * n_steps == n_cols and slab % LANES == 0 and (split_col is None or split_col % slab == 0)
        lead = (None,) * (sw.ndim - 2)
        split_blk = n_steps if split_col is None else split_col // slab
        in_specs.append(pl.BlockSpec(lead + (K, slab), lambda *i, nl=len(lead): (0,) * nl + (0, step(*i))))
        in_specs.append(pl.BlockSpec(
            (None, K, LANES), lambda *i, sb=split_blk, last=sg.shape[0] - 1: (jnp.where(step(*i) < sb, 0, last), 0, 0)))
        args += [sw, jnp.broadcast_to(sg[:, :, None], (sg.shape[0], K, LANES))]
        out_shape.append(jax.ShapeDtypeStruct((K, n_cols), BF16))
        out_specs.append(pl.BlockSpec((K, slab), lambda *i: (0, step(*i))))
    return in_specs, args, out_shape, out_specs


def _norm_proj_kernel(var_ref, x_ref, *rest, mm_variants, emit_variants, has_gains, has_aux, n_side):
    rest = list(rest)
    g_ref = rest.pop(0) if has_gains else None
    w_ref = rest.pop(0)
    waux_ref = rest.pop(0) if has_aux else None
    side_in = [(rest.pop(0), rest.pop(0)) for _ in range(n_side)]
    o_ref = rest.pop(0)
    aux_ref = rest.pop(0) if has_aux else None
    hv_ref = rest.pop(0) if emit_variants else None
    side_out = [rest.pop(0) for _ in range(n_side)]
    (h_ref,) = rest
    n = pl.program_id(1)
    tm = x_ref.shape[0]

    @pl.when(n == 0)
    def _():
        x = x_ref[...]
        y = x * lax.rsqrt(jnp.mean(x * x, axis=-1, keepdims=True) + RMS_EPS)
        targets = ([(h_ref, i, v) for i, v in enumerate(mm_variants)]
                   + [(hv_ref, i, v) for i, v in enumerate(emit_variants)])
        for gi in sorted({g for _, _, (g, _) in targets}):
            h = (y if gi < 0 else y * g_ref[gi:gi + 1, :]).astype(BF16)
            for ref, i, (g, d) in targets:
                if g != gi:
                    continue
                if d == 1:
                    ref[i] = h
                else:
                    perm = _deinterleave_matrix(d)
                    for b in range(tm // PERM_ROWS):
                        rows = slice(b * PERM_ROWS, (b + 1) * PERM_ROWS)
                        ref[i, rows, :] = jnp.dot(perm, h[rows], preferred_element_type=F32).astype(BF16)
        if has_aux:
            aux_ref[...] = jnp.dot(h_ref[0], waux_ref[...], preferred_element_type=F32)

    h = h_ref[var_ref[n]]
    o_ref[...] = jnp.dot(h, w_ref[...], preferred_element_type=F32).astype(o_ref.dtype)

    _run_side_jobs(side_in, side_out)


def _norm_proj(x, gains, mm_variants, var_of_block, w, w_aux, emit_variants=(), side=(), *, n_out, w_col0=0, tm, tn):
    S, D = x.shape
    N = n_out
    has_gains = gains is not None
    assert has_gains or all(g < 0 for g, _ in tuple(mm_variants) + tuple(emit_variants))
    has_aux = w_aux is not None
    grid = (S // tm, N // tn)
    w_blk0 = w_col0 // tn
    in_specs = [pl.BlockSpec((tm, D), lambda m, n, var: (m, 0))]
    args = [x]
    if has_gains:
        in_specs.append(pl.BlockSpec(gains.shape, lambda m, n, var: (0, 0)))
        args.append(gains)
    in_specs.append(pl.BlockSpec((D, tn), lambda m, n, var: (0, w_blk0 + n)))
    args.append(w)
    out_shape = [jax.ShapeDtypeStruct((S, N), BF16)]
    out_specs = [pl.BlockSpec((tm, tn), lambda m, n, var: (m, n))]
    if has_aux:
        in_specs.append(pl.BlockSpec((D, LANES), lambda m, n, var: (0, 0)))
        out_shape.append(jax.ShapeDtypeStruct((S, LANES), F32))
        out_specs.append(pl.BlockSpec((tm, LANES), lambda m, n, var: (m, 0)))
        args.append(w_aux)
    if emit_variants:
        out_shape.append(jax.ShapeDtypeStruct((len(emit_variants), S, D), BF16))
        out_specs.append(pl.BlockSpec((len(emit_variants), tm, D), lambda m, n, var: (0, m, 0)))
    side_in, side_args, side_shape, side_out = _side_job_specs(side, grid[0] * grid[1],
                                                               lambda m, n, var: m * grid[1] + n)
    in_specs += side_in
    args += side_args
    out_shape += side_shape
    out_specs += side_out
    return pl.pallas_call(
        functools.partial(_norm_proj_kernel, mm_variants=tuple(mm_variants), emit_variants=tuple(emit_variants),
                          has_gains=has_gains, has_aux=has_aux, n_side=len(side)),
        grid_spec=pltpu.PrefetchScalarGridSpec(
            num_scalar_prefetch=1, grid=grid, in_specs=in_specs, out_specs=out_specs,
            scratch_shapes=[pltpu.VMEM((len(mm_variants), tm, D), BF16)]),
        out_shape=out_shape,
        compiler_params=pltpu.CompilerParams(
            dimension_semantics=("arbitrary", "arbitrary"), vmem_limit_bytes=VMEM_LIMIT),
        name="norm_proj_aux" if has_aux else "norm_proj",
    )(var_of_block, *args)


def _split3(a):
    hi = a.astype(BF16)
    r1 = a - hi.astype(F32)
    mid = r1.astype(BF16)
    lo = (r1 - mid.astype(F32)).astype(BF16)
    return hi, mid, lo


def _gla_scores(b_ref, q_ref, k_ref, row0, v, state):
    C, SUB, HALF = GLA_CHUNK, GLA_SUB, GLA_SUB // 2
    rows = pl.ds(row0, C)
    b, q, k = b_ref[rows, :], q_ref[rows, :], k_ref[rows, :]
    dk = q.shape[1]
    b_last = b_ref[pl.ds(row0 + C - 1, 1), :]
    o_inter = jnp.dot((q * jnp.exp2(b)).astype(BF16), state.astype(BF16), preferred_element_type=F32)
    k_dec = k * jnp.exp2(b_last - b)
    upd = jnp.dot(k_dec.T.astype(BF16), v, preferred_element_type=F32)
    e_col = jnp.broadcast_to(jnp.exp2(b_last), (dk, dk)).T
    new_state = state * jnp.concatenate([e_col] * (v.shape[1] // dk), axis=1) + upd

    ones = jnp.ones((dk, LANES), BF16)
    diag_sums, offs = [], []
    for i in range(C // SUB):
        r0 = i * SUB
        halves = [(b[r0 + u * HALF:r0 + (u + 1) * HALF], q[r0 + u * HALF:r0 + (u + 1) * HALF]) for u in range(2)]
        prods = []
        for s in range(SUB):
            bs = b_ref[pl.ds(row0 + r0 + s, 1), :]
            ks = k_ref[pl.ds(row0 + r0 + s, 1), :]
            for u in range(s // HALF, 2):
                prods.append(halves[u][1] * jnp.exp2(halves[u][0] - bs) * ks)
        diag_sums.append(jnp.dot(jnp.concatenate(prods, axis=0).astype(BF16), ones, preferred_element_type=F32))
        if i > 0:
            b0 = b_ref[pl.ds(row0 + r0, 1), :]
            qt = (q[r0:r0 + SUB] * jnp.exp2(b[r0:r0 + SUB] - b0)).astype(BF16)
            kt = jnp.concatenate([(k[:r0] * jnp.exp2(b0 - b[:r0])).astype(BF16), jnp.zeros((C - r0, dk), BF16)], axis=0)
            offs.append(lax.dot_general(qt, kt, _NT, preferred_element_type=F32))
    return o_inter, new_state, diag_sums, offs


def _gla_output(o_inter, diag_sums, offs, v):
    C, SUB, HALF = GLA_CHUNK, GLA_SUB, GLA_SUB // 2
    lane = lax.broadcasted_iota(jnp.int32, (HALF, LANES), 1)
    row = lax.broadcasted_iota(jnp.int32, (HALF, LANES), 0)
    a_rows = []
    for i in range(C // SUB):
        r0 = i * SUB
        acc = [jnp.zeros((HALF, LANES), F32) for _ in range(2)]
        piece = 0
        for s in range(SUB):
            for u in range(s // HALF, 2):
                acc[u] = jnp.where(lane == r0 + s, diag_sums[i][piece * HALF:(piece + 1) * HALF, :], acc[u])
                piece += 1
        a_i = jnp.concatenate([jnp.where(lane <= row + r0 + u * HALF, acc[u], 0.0) for u in range(2)], axis=0)[:, :C]
        a_rows.append(a_i + offs[i - 1] if i > 0 else a_i)
    a = jnp.concatenate(a_rows, axis=0).astype(BF16)
    return o_inter + jnp.dot(a, v, preferred_element_type=F32)


def _gla_kernel(q_ref, k_ref, v0_ref, v1_ref, z0_ref, z1_ref, lr_ref, x_ref, wgu_ref, bg_ref, gout_ref, wout_ref,
                *rest, n_side):
    side_in = [(rest[2 * i], rest[2 * i + 1]) for i in range(n_side)]
    o_ref, *side_out = rest[2 * n_side:3 * n_side + 1]
    state_ref, y_ref, b_ref, q32_ref, k32_ref = rest[3 * n_side + 1:]
    _run_side_jobs(side_in, side_out)
    tm = q_ref.shape[0]
    C = GLA_CHUNK
    heads_per_ref = v0_ref.shape[1] // GLA_DV_HEAD

    def head_cols(refs, h):
        u = h % heads_per_ref
        return refs[h // heads_per_ref], slice(u * GLA_DV_HEAD, (u + 1) * GLA_DV_HEAD)

    @pl.when(pl.program_id(0) == 0)
    def _():
        state_ref[...] = jnp.zeros_like(state_ref)

    pre = jnp.dot(lr_ref[...].astype(BF16), wgu_ref[...], preferred_element_type=F32) + bg_ref[...]
    log2_a = (jnp.minimum(pre, 0.0) - jnp.log1p(jnp.exp(-jnp.abs(pre)))) * (LOG2_E / GLA_GATE_TAU)
    tb = min(tm, GLA_CUMSUM_ROWS)
    r = lax.broadcasted_iota(jnp.int32, (tb, tb), 0)
    c = lax.broadcasted_iota(jnp.int32, (tb, tb), 1)
    tri = ((c <= r) & ((c // C) == (r // C))).astype(BF16)
    b_all = jnp.concatenate(
        [sum(jnp.dot(tri, piece, preferred_element_type=F32) for piece in _split3(log2_a[t * tb:(t + 1) * tb]))
         for t in range(tm // tb)], axis=0)
    for h in range(GLA_HEADS):
        kl = slice(h * GLA_DK_HEAD, (h + 1) * GLA_DK_HEAD)
        b_ref[h] = b_all[:, kl]
        q32_ref[h] = q_ref[:, kl].astype(F32) * (GLA_DK_HEAD ** -0.5)
        k32_ref[h] = k_ref[:, kl].astype(F32)

    gout = gout_ref[...]

    out_rows = GLA_OUT_ROWS
    def scores_stage(unit):
        ci, h = divmod(unit, GLA_HEADS)
        rows = slice(ci * C, (ci + 1) * C)
        v_ref, vl = head_cols((v0_ref, v1_ref), h)
        o_inter, new_state, diag_sums, offs = _gla_scores(b_ref.at[h], q32_ref.at[h], k32_ref.at[h], ci * C,
                                                          v_ref[rows, vl], state_ref[h])
        state_ref[h] = new_state
        return o_inter, diag_sums, offs

    def output_stage(unit, parts):
        ci, h = divmod(unit, GLA_HEADS)
        rows = slice(ci * C, (ci + 1) * C)
        v_ref, vl = head_cols((v0_ref, v1_ref), h)
        z_ref, zl = head_cols((z0_ref, z1_ref), h)
        o = _gla_output(*parts, v_ref[rows, vl])
        o = o * lax.rsqrt(jnp.mean(o * o, axis=-1, keepdims=True) + RMS_EPS) * gout
        z = z_ref[rows, zl].astype(F32)
        y_ref[rows, h * GLA_DV_HEAD:(h + 1) * GLA_DV_HEAD] = (o * (z * jax.nn.sigmoid(z))).astype(BF16)
        if h == GLA_HEADS - 1 and (ci + 1) * C % out_rows == 0:
            done = slice((ci + 1) * C - out_rows, (ci + 1) * C)
            o_ref[done, :] = x_ref[done, :] + jnp.dot(y_ref[done, :], wout_ref[...], preferred_element_type=F32)

    n_units = (tm // C) * GLA_HEADS
    ahead = [scores_stage(u) for u in range(GLA_UNITS_AHEAD)]
    for unit in range(n_units):
        if unit + GLA_UNITS_AHEAD < n_units:
            ahead.append(scores_stage(unit + GLA_UNITS_AHEAD))
        output_stage(unit, ahead.pop(0))


def _gla_layer(x, proj, lr, w_gate_up, b_gate, g_out, w_out, side=(), *, tm):
    S, D = x.shape
    side_in, side_args, side_shape, side_out = _side_job_specs(side, S // tm, lambda m: m)
    half = D_INNER // 2
    v_blk, z_blk = 2 * GLA_DK // half, (2 * GLA_DK + D_INNER) // half
    return pl.pallas_call(
        functools.partial(_gla_kernel, n_side=len(side)),
        grid=(S // tm,),
        in_specs=[
            pl.BlockSpec((tm, GLA_DK), lambda m: (m, 0)),
            pl.BlockSpec((tm, GLA_DK), lambda m: (m, 1)),
            pl.BlockSpec((tm, half), lambda m: (m, v_blk)),
            pl.BlockSpec((tm, half), lambda m: (m, v_blk + 1)),
            pl.BlockSpec((tm, half), lambda m: (m, z_blk)),
            pl.BlockSpec((tm, half), lambda m: (m, z_blk + 1)),
            pl.BlockSpec((tm, LANES), lambda m: (m, 0)),
            pl.BlockSpec((tm, D), lambda m: (m, 0)),
            pl.BlockSpec((LANES, GLA_DK), lambda m: (0, 0)),
            pl.BlockSpec((1, GLA_DK), lambda m: (0, 0)),
            pl.BlockSpec((1, GLA_DV_HEAD), lambda m: (0, 0)),
            pl.BlockSpec((D_INNER, D), lambda m: (0, 0)),
        ] + side_in,
        out_specs=[pl.BlockSpec((tm, D), lambda m: (m, 0))] + side_out,
        out_shape=[jax.ShapeDtypeStruct((S, D), F32)] + side_shape,
        scratch_shapes=[
            pltpu.VMEM((GLA_HEADS, GLA_DK_HEAD, GLA_DV_HEAD), F32),
            pltpu.VMEM((tm, D_INNER), BF16),
            pltpu.VMEM((GLA_HEADS, tm, GLA_DK_HEAD), F32),
            pltpu.VMEM((GLA_HEADS, tm, GLA_DK_HEAD), F32),
            pltpu.VMEM((GLA_HEADS, tm, GLA_DK_HEAD), F32),
        ],
        compiler_params=pltpu.CompilerParams(
            dimension_semantics=("arbitrary",), vmem_limit_bytes=VMEM_LIMIT),
        name="gla",
    )(proj, proj, proj, proj, proj, proj, lr, x, w_gate_up, b_gate, g_out, w_out, *side_args)


def _block_rows(ref):
    return ref[...].reshape(-1, ref.shape[-1])


def _dil_attn_kernel(h_ref, wq_ref, wk_ref, wv_ref, o_ref, m_ref, l_ref,
                     q_s, k_s, v_s, bias_ref, m_acc, l_acc, *, slopes, dilation, qb):
    blk = DIL_BLOCK
    i = pl.program_id(1)
    new_rows = slice(blk, (qb + 1) * blk)

    @pl.when((pl.program_id(0) == 0) & (i == 0))
    def _():
        row = lax.broadcasted_iota(jnp.int32, (blk, blk), 0)
        col = lax.broadcasted_iota(jnp.int32, (blk, blk), 1)
        dist_prev = jnp.where(col >= row, ((row + blk - col) * dilation).astype(F32), MASK_DISTANCE)
        dist_cur = jnp.where(col <= row, ((row - col) * dilation).astype(F32), MASK_DISTANCE)
        for h in range(DIL_HEADS):
            bias_ref[h, :, :blk] = -(slopes[h] * LOG2_E) * dist_prev
            bias_ref[h, :, blk:] = -(slopes[h] * LOG2_E) * dist_cur

    def all_heads(first_step):
        hq = hkv = _block_rows(h_ref)
        heads_per_chunk = PROJ_COLS // DIL_HEAD_DIM
        n_chunks = D_INNER // PROJ_COLS
        projected = [0]

        def project_through(chunk):
            while projected[0] <= min(chunk, n_chunks - 1):
                cols = slice(projected[0] * PROJ_COLS, (projected[0] + 1) * PROJ_COLS)
                q_s[:, cols] = jnp.dot(hq, wq_ref[:, cols], preferred_element_type=F32).astype(BF16)
                k_s[new_rows, cols] = jnp.dot(hkv, wk_ref[:, cols], preferred_element_type=F32).astype(BF16)
                v_s[new_rows, cols] = jnp.dot(hkv, wv_ref[:, cols], preferred_element_type=F32).astype(BF16)
                projected[0] += 1

        def key_rows(j):
            return slice((j + 1) * blk, (j + 2) * blk) if first_step and j == 0 else slice(j * blk, (j + 2) * blk)

        def scores(unit):
            h, j = divmod(unit, qb)
            project_through(h // heads_per_chunk + PROJ_LOOKAHEAD)
            hl = slice(h * DIL_HEAD_DIM, (h + 1) * DIL_HEAD_DIM)
            keys = key_rows(j)
            bias = bias_ref[h, :, 2 * blk - (keys.stop - keys.start):]
            q = q_s[j * blk:(j + 1) * blk, hl]
            return lax.dot_general(q, k_s[keys, hl], _NT, preferred_element_type=F32) + bias

        n_units = DIL_HEADS * qb
        ahead = [scores(u) for u in range(SCORE_LOOKAHEAD)]
        for unit in range(n_units):
            h, j = divmod(unit, qb)
            hl = slice(h * DIL_HEAD_DIM, (h + 1) * DIL_HEAD_DIM)
            s = ahead.pop(0)
            if unit + SCORE_LOOKAHEAD < n_units:
                ahead.append(scores(unit + SCORE_LOOKAHEAD))
            m = jnp.max(s, axis=-1, keepdims=True)
            p = jnp.exp2(s - m)
            l = jnp.sum(p, axis=-1, keepdims=True)
            o = jnp.dot(p.astype(BF16), v_s[key_rows(j), hl], preferred_element_type=F32).astype(o_ref.dtype)
            if len(o_ref.shape) == 2:
                o_ref[j * blk:(j + 1) * blk, hl] = o
            else:
                pieces = o_ref.shape[0] // qb
                o_ref[j * pieces:(j + 1) * pieces, :, hl] = o.reshape(pieces, o_ref.shape[1], DIL_HEAD_DIM)
            m_acc[j * blk:(j + 1) * blk, h:h + 1] = m
            l_acc[j * blk:(j + 1) * blk, h:h + 1] = l

    m_acc[...] = jnp.zeros_like(m_acc)
    l_acc[...] = jnp.zeros_like(l_acc)

    @pl.when(i == 0)
    def _():
        all_heads(True)

    @pl.when(i > 0)
    def _():
        all_heads(False)

    m_ref[...] = m_acc[...].reshape(m_ref.shape)
    l_ref[...] = l_acc[...].reshape(l_ref.shape)
    k_s[:blk, :] = k_s[qb * blk:, :]
    v_s[:blk, :] = v_s[qb * blk:, :]


def _dil_attn_group(hv, w_kv, w_q, g, *, S, qb):
    _, d = DIL_GROUPS[g]
    D = hv.shape[-1]
    rows = qb * DIL_BLOCK
    n_steps = S // d // rows
    n_all = N_GROUPS * DIL_HEADS
    slopes = tuple(2.0 ** (-ALIBI_MAX_EXP * (g * DIL_HEADS + h + 1.0) / n_all) for h in range(DIL_HEADS))
    kcol, vcol, qcol = g, N_GROUPS + g, g
    if d == 1:
        h_spec = lambda v: pl.BlockSpec((None, rows, D), lambda r, i: (v, i, 0))
        spec = lambda width: pl.BlockSpec((rows, width), lambda r, i: (i, 0))
        shape = lambda width, dt: jax.ShapeDtypeStruct((S, width), dt)
    else:
        run = PERM_ROWS // d
        pieces = rows // run
        hv = hv.reshape(hv.shape[0], S // PERM_ROWS, d, run, D)
        h_spec = lambda v: pl.BlockSpec((None, pieces, None, run, D), lambda r, i: (v, i, r, 0, 0))
        spec = lambda width: pl.BlockSpec((pieces, None, run, width), lambda r, i: (i, r, 0, 0))
        shape = lambda width, dt: jax.ShapeDtypeStruct((S // PERM_ROWS, d, run, width), dt)
    w_spec = lambda col: pl.BlockSpec((D, D_INNER), lambda r, i: (0, col))
    o, m, l = pl.pallas_call(
        functools.partial(_dil_attn_kernel, slopes=slopes, dilation=d, qb=qb),
        grid=(d, n_steps),
        in_specs=[h_spec(g), w_spec(qcol), w_spec(kcol), w_spec(vcol)],
        out_specs=[spec(D_INNER), spec(LANES), spec(LANES)],
        out_shape=[shape(D_INNER, BF16), shape(LANES, F32), shape(LANES, F32)],
        scratch_shapes=[
            pltpu.VMEM((rows, D_INNER), BF16),
            pltpu.VMEM((rows + DIL_BLOCK, D_INNER), BF16),
            pltpu.VMEM((rows + DIL_BLOCK, D_INNER), BF16),
            pltpu.VMEM((DIL_HEADS, DIL_BLOCK, 2 * DIL_BLOCK), F32),
            pltpu.VMEM((rows, LANES), F32),
            pltpu.VMEM((rows, LANES), F32),
        ],
        compiler_params=pltpu.CompilerParams(
            dimension_semantics=("arbitrary", "arbitrary"), vmem_limit_bytes=VMEM_LIMIT),
        name=f"dil_attn_g{g}",
    )(hv, w_q, w_kv, w_kv)
    return o.reshape(S, D_INNER), m.reshape(S, LANES), l.reshape(S, LANES)


def _combine_kernel(o0_ref, o1_ref, o2_ref, m0_ref, m1_ref, m2_ref, l0_ref, l1_ref, l2_ref, z_ref, x_ref,
                    wout_ref, g_ref, out_ref, y_ref, ot_ref):
    tm = x_ref.shape[0]
    o_refs = (o0_ref, o1_ref, o2_ref)
    m_refs = (m0_ref, m1_ref, m2_ref)
    l_refs = (l0_ref, l1_ref, l2_ref)
    unperms = [None] + [_deinterleave_matrix(d, transpose=True) for _, d in DIL_GROUPS[1:]]

    def to_token_order(b):
        rows = slice(b * PERM_ROWS, (b + 1) * PERM_ROWS)
        ms, ls = [m_refs[0][rows, :]], [l_refs[0][rows, :]]
        for g in range(1, N_GROUPS):
            ot_ref[g - 1, rows, :] = jnp.dot(unperms[g], o_refs[g][rows, :], preferred_element_type=F32)
            for stats, ref in ((ms, m_refs[g]), (ls, l_refs[g])):
                stats.append(sum(jnp.dot(unperms[g], piece, preferred_element_type=F32)
                                 for piece in _split3(ref[rows, :])))
        return ms, ls

    def mix_and_project(rows, ms, ls):
        m = jnp.maximum(jnp.maximum(ms[0], ms[1]), ms[2])
        es = [jnp.exp2(mg - m) for mg in ms]
        den = es[0] * ls[0] + es[1] * ls[1] + es[2] * ls[2]
        lane = lax.broadcasted_iota(jnp.int32, m.shape, 1)
        ws = [jnp.where(lane < DIL_HEADS, e / den, 0.0) for e in es]
        for h in range(DIL_HEADS):
            hl = slice(h * DIL_HEAD_DIM, (h + 1) * DIL_HEAD_DIM)
            acc = ws[0][:, h:h + 1] * o_refs[0][rows, hl].astype(F32)
            for g in range(1, N_GROUPS):
                acc = acc + ws[g][:, h:h + 1] * ot_ref[g - 1, rows, hl]
            z = z_ref[rows, hl].astype(F32)
            y_ref[rows, hl] = (acc * (z * jax.nn.sigmoid(z))).astype(BF16)
        x = x_ref[rows, :] + jnp.dot(y_ref[rows, :], wout_ref[...], preferred_element_type=F32)
        y = x * lax.rsqrt(jnp.mean(x * x, axis=-1, keepdims=True) + RMS_EPS)
        out_ref[rows, :] = y * g_ref[...]

    n_blocks = tm // PERM_ROWS
    sub = COMBINE_SUB_ROWS
    stats = to_token_order(0)
    for b in range(n_blocks):
        nxt = to_token_order(b + 1) if b + 1 < n_blocks else None
        for u in range(PERM_ROWS // sub):
            local = slice(u * sub, (u + 1) * sub)
            rows = slice(b * PERM_ROWS + u * sub, b * PERM_ROWS + (u + 1) * sub)
            mix_and_project(rows, [mg[local] for mg in stats[0]], [lg[local] for lg in stats[1]])
        stats = nxt


def _combine(os_, ms, ls, z, x, w_out, g_final, *, tm):
    S, D = x.shape
    row_blk = lambda m: (m, 0)
    return pl.pallas_call(
        _combine_kernel,
        grid=(S // tm,),
        in_specs=[pl.BlockSpec((tm, D_INNER), row_blk)] * 3 + [pl.BlockSpec((tm, LANES), row_blk)] * 6 + [
            pl.BlockSpec((tm, D_INNER), row_blk),
            pl.BlockSpec((tm, D), row_blk),
            pl.BlockSpec((D_INNER, D), lambda m: (0, 0)),
            pl.BlockSpec((1, D), lambda m: (0, 0)),
        ],
        out_specs=pl.BlockSpec((tm, D), row_blk),
        out_shape=jax.ShapeDtypeStruct((S, D), F32),
        scratch_shapes=[pltpu.VMEM((tm, D_INNER), BF16), pltpu.VMEM((N_GROUPS - 1, tm, D_INNER), F32)],
        compiler_params=pltpu.CompilerParams(
            dimension_semantics=("arbitrary",), vmem_limit_bytes=VMEM_LIMIT),
        name="combine",
    )(*os_, *ms, *ls, z, x, w_out, g_final)


def kernel(x, a_norm, a_w_in, a_w_gate_up, a_b_gate, a_g_out, a_w_out, kv_norm, w_kv, b_norm, b_w_in, b_w_out,
           final_norm):
    B, S, D = x.shape
    assert B == 1 and D == D_MODEL and a_norm.shape[0] == 1 and b_norm.shape[0] == 1
    assert S % (ATTN_BLOCKS_PER_STEP * DIL_BLOCK * max(d for _, d in DIL_GROUPS)) == 0
    x0 = x.reshape(S, D)

    n_a = 2 * GLA_DK + 2 * D_INNER
    w_a = a_w_in[0].astype(BF16)
    w_lr = jnp.pad(w_a[:, n_a:], ((0, 0), (0, LANES - GLA_GATE_RANK)))
    proj_a, lr = _norm_proj(x0, a_norm, [(0, 1)], jnp.zeros((2,), jnp.int32), w_a, w_lr, n_out=n_a,
                            tm=NORM_PROJ_ROWS, tn=n_a // 2)
    w_gu = jnp.pad(a_w_gate_up[0], ((0, LANES - GLA_GATE_RANK), (0, 0))).astype(BF16)
    q_scale = DIL_HEAD_DIM ** -0.5 * LOG2_E
    side = [(w_kv, kv_norm[None], None),
            (b_w_in, jnp.stack([b_norm[0] * q_scale, b_norm[0]]), N_GROUPS * D_INNER)]
    x1, w_kv_b, w_qz = _gla_layer(x0, proj_a, lr, w_gu, a_b_gate, a_g_out, a_w_out[0].astype(BF16), side, tm=GLA_ROWS)

    variants = [(-1, d) for _, d in DIL_GROUPS]
    z, hv = _norm_proj(x1, None, variants[:1], jnp.zeros((2,), jnp.int32), w_qz, None, variants,
                       n_out=D_INNER, w_col0=N_GROUPS * D_INNER, tm=NORM_PROJ_ROWS, tn=D_INNER // 2)

    outs = [_dil_attn_group(hv, w_kv_b, w_qz, g, S=S, qb=ATTN_BLOCKS_PER_STEP) for g in range(N_GROUPS)]
    out = _combine([o for o, _, _ in outs], [m for _, m, _ in outs], [l for _, _, l in outs], z, x1,
                   b_w_out[0].astype(BF16), final_norm.reshape(1, D), tm=COMBINE_ROWS)
    return out.reshape(B, S, D)
```
